```python
import jax, jax.numpy as jnp
from jax import lax
import numpy as np

D_MODEL = 1024
BATCH = 4
SEQ = 4096
DEPTH = 1

D_MIX = 2 * D_MODEL
D_POOL = D_MIX // 2
D_MLSTM = D_MIX - D_POOL
POOL_WINDOWS = (2, 4, 8, 16)
N_POOL_GROUPS = len(POOL_WINDOWS)
POOL_GROUP_DIM = D_POOL // N_POOL_GROUPS
N_HEADS = 4
HEAD_DIM = D_MLSTM // N_HEADS
QKV_BLOCK = 4
N_QKV_BLOCKS = D_MLSTM // QKV_BLOCK
CONV_WIDTH = 5
CHUNK = 128
N_DIRS = 2
EPS = 1e-6
NEG_INF = -1e30

kernel_name = "hymba_pool_mlstm_bidir_block"


def rms_norm(x, g):
    xf = x.astype(jnp.float32)
    y = xf * lax.rsqrt(jnp.mean(xf * xf, axis=-1, keepdims=True) + EPS)
    return (y * g.astype(jnp.float32)).astype(x.dtype)


def window_mean_minus_self(u, w):
    S = u.shape[1]
    left = (w - 1) // 2
    right = w - 1 - left
    uf = u.astype(jnp.float32)
    cs = jnp.concatenate([jnp.zeros_like(uf[:, :1]), jnp.cumsum(uf, axis=1)], axis=1)
    t = np.arange(S)
    lo = np.maximum(t - left, 0)
    hi = np.minimum(t + right, S - 1)
    total = cs[:, hi + 1] - cs[:, lo]
    count = (hi - lo + 1).astype(np.float32)[None, :, None]
    return (total / count - uf).astype(u.dtype)


def pool_mixer(u, pool_w, pool_scale):
    B, S, _ = u.shape
    ug = u.reshape(B, S, N_POOL_GROUPS, POOL_GROUP_DIM)
    pooled = jnp.stack([window_mean_minus_self(ug[:, :, g], w) for g, w in enumerate(POOL_WINDOWS)], axis=2)
    mixed = jnp.einsum('bsgc,gcd->bsgd', pooled, pool_w).reshape(B, S, D_POOL)
    return mixed * pool_scale


def centred_depthwise_conv(u, w, b):
    K, C = w.shape
    pad = K // 2
    y = lax.conv_general_dilated(u, w[:, None, :].astype(u.dtype), window_strides=(1,),
                                 padding=[(pad, pad)], dimension_numbers=('NWC', 'WIO', 'NWC'),
                                 feature_group_count=C)
    return y + b


def headwise(u, w):
    B, S, _ = u.shape
    return jnp.einsum('bsni,nio->bsno', u.reshape(B, S, w.shape[0], w.shape[1]), w).reshape(B, S, -1)


def mlstm_chunkwise(q, k, v, log_i, log_f):
    B, H, S, d = q.shape
    nc = S // CHUNK

    def to_chunks(a):
        return jnp.moveaxis(a.reshape(B, H, nc, CHUNK, *a.shape[3:]), 2, 0)

    qc, kc, vc, ic, fc = map(to_chunks, (q, k, v, log_i, log_f))
    lower_tri = jnp.tril(jnp.ones((CHUNK, CHUNK), dtype=bool))

    def step(carry, xs):
        C, n, m = carry
        qj, kj, vj, ij, fj = xs
        b = jnp.cumsum(fj, axis=-1)
        log_inter = b + m[..., None]
        log_D = b[..., :, None] - b[..., None, :] + ij[..., None, :]
        log_D = jnp.where(lower_tri, log_D, NEG_INF)
        m_t = jnp.maximum(log_inter, jnp.max(log_D, axis=-1))
        inter_w = jnp.exp(log_inter - m_t)
        s = jnp.einsum('bhtd,bhsd->bhts', qj, kj) * jnp.exp(log_D - m_t[..., None])
        num = jnp.einsum('bhts,bhse->bhte', s, vj) + inter_w[..., None] * jnp.einsum('bhtd,bhde->bhte', qj, C)
        den = jnp.sum(s, axis=-1) + inter_w * jnp.einsum('bhtd,bhd->bht', qj, n)
        h = num / jnp.maximum(jnp.abs(den), jnp.exp(-m_t))[..., None]
        bL = b[..., -1]
        log_s = bL[..., None] - b + ij
        m_new = jnp.maximum(bL + m, jnp.max(log_s, axis=-1))
        decay = jnp.exp(bL + m - m_new)
        ws = jnp.exp(log_s - m_new[..., None])
        C_new = decay[..., None, None] * C + jnp.einsum('bhs,bhsd,bhse->bhde', ws, kj, vj)
        n_new = decay[..., None] * n + jnp.einsum('bhs,bhsd->bhd', ws, kj)
        return (C_new, n_new, m_new), h

    init = (jnp.zeros((B, H, d, d), jnp.float32), jnp.zeros((B, H, d), jnp.float32),
            jnp.zeros((B, H), jnp.float32))
    _, hs = lax.scan(step, init, (qc, kc, vc, ic, fc))
    return jnp.moveaxis(hs, 0, 2).reshape(B, H, S, d)


def mlstm_branch(u, conv_w, conv_b, w_q, w_k, w_v, w_gates, b_gates, mh_norm_w, skip_w):
    B, S, _ = u.shape
    c = jax.nn.silu(centred_depthwise_conv(u, conv_w, conv_b))
    q = headwise(c, w_q)
    k = headwise(c, w_k)
    v = headwise(u, w_v)
    qkv = jnp.concatenate([q, k, v], axis=-1).astype(jnp.float32)
    gates = jnp.einsum('bsc,ncg->nbsg', qkv, w_gates.astype(jnp.float32)) \
        + b_gates.astype(jnp.float32)[:, None, None, :]
    gates = jnp.transpose(gates, (0, 1, 3, 2))
    log_i = gates[:, :, :N_HEADS]
    log_f = jax.nn.log_sigmoid(gates[:, :, N_HEADS:])

    def heads(a):
        return a.reshape(B, S, N_HEADS, HEAD_DIM).transpose(0, 2, 1, 3).astype(jnp.float32)

    qh, kh, vh = heads(q), heads(k) * (HEAD_DIM ** -0.5), heads(v)
    h_fwd = mlstm_chunkwise(qh, kh, vh, log_i[0], log_f[0])
    h_bwd = jnp.flip(mlstm_chunkwise(jnp.flip(qh, 2), jnp.flip(kh, 2), jnp.flip(vh, 2),
                                     jnp.flip(log_i[1], -1), jnp.flip(log_f[1], -1)), 2)
    h = h_fwd + h_bwd
    mu = jnp.mean(h, axis=-1, keepdims=True)
    var = jnp.mean(jnp.square(h - mu), axis=-1, keepdims=True)
    h = (h - mu) * lax.rsqrt(var + EPS)
    h = h.transpose(0, 2, 1, 3).reshape(B, S, D_MLSTM)
    out = h * mh_norm_w.astype(jnp.float32) + skip_w.astype(jnp.float32) * c.astype(jnp.float32)
    return out.astype(u.dtype)


def setup_inputs(seed: int = 0) -> dict:
    key = jax.random.key(seed)
    ks = jax.random.split(key, 20)
    nrm = jax.random.normal
    f32 = jnp.float32
    x = nrm(ks[0], (BATCH, SEQ, D_MODEL), f32)
    norm_in_g = 1.0 + 0.1 * nrm(ks[1], (DEPTH, D_MODEL), f32)
    w_in = nrm(ks[2], (DEPTH, D_MODEL, 2 * D_MIX), f32) * D_MODEL ** -0.5
    pool_w = nrm(ks[3], (DEPTH, N_POOL_GROUPS, POOL_GROUP_DIM, POOL_GROUP_DIM), f32) * POOL_GROUP_DIM ** -0.5
    pool_scale = 1.0 + 0.1 * nrm(ks[4], (DEPTH, D_POOL), f32)
    conv_w = nrm(ks[5], (DEPTH, CONV_WIDTH, D_MLSTM), f32) * CONV_WIDTH ** -0.5
    conv_b = 0.01 * nrm(ks[6], (DEPTH, D_MLSTM), f32)
    w_q = nrm(ks[7], (DEPTH, N_QKV_BLOCKS, QKV_BLOCK, QKV_BLOCK), f32) * QKV_BLOCK ** -0.5
    w_k = nrm(ks[8], (DEPTH, N_QKV_BLOCKS, QKV_BLOCK, QKV_BLOCK), f32) * QKV_BLOCK ** -0.5
    w_v = nrm(ks[9], (DEPTH, N_QKV_BLOCKS, QKV_BLOCK, QKV_BLOCK), f32) * QKV_BLOCK ** -0.5
    w_gates = nrm(ks[10], (DEPTH, N_DIRS, 3 * D_MLSTM, 2 * N_HEADS), f32) * (3 * D_MLSTM) ** -0.5
    b_i = 0.1 * nrm(ks[11], (DEPTH, N_DIRS, N_HEADS), f32)
    b_f = jnp.linspace(3.0, 6.0, N_HEADS, dtype=f32) + 0.1 * nrm(ks[12], (DEPTH, N_DIRS, N_HEADS), f32)
    b_gates = jnp.concatenate([b_i, b_f], axis=-1)
    mh_norm_w = 1.0 + 0.1 * nrm(ks[13], (DEPTH, D_MLSTM), f32)
    skip_w = 1.0 + 0.1 * nrm(ks[14], (DEPTH, D_MLSTM), f32)
    w_out = nrm(ks[15], (DEPTH, D_MIX, D_MODEL), f32) * D_MIX ** -0.5
    norm_out_g = 1.0 + 0.1 * nrm(ks[16], (D_MODEL,), f32)
    return {"x": x, "norm_in_g": norm_in_g, "w_in": w_in, "pool_w": pool_w, "pool_scale": pool_scale,
            "conv_w": conv_w, "conv_b": conv_b, "w_q": w_q, "w_k": w_k, "w_v": w_v,
            "w_gates": w_gates, "b_gates": b_gates, "mh_norm_w": mh_norm_w, "skip_w": skip_w,
            "w_out": w_out, "norm_out_g": norm_out_g}


def reference(x, norm_in_g, w_in, pool_w, pool_scale, conv_w, conv_b, w_q, w_k, w_v,
              w_gates, b_gates, mh_norm_w, skip_w, w_out, norm_out_g):
    h = x
    for l in range(DEPTH):
        u = rms_norm(h, norm_in_g[l])
        proj = jnp.einsum('bsd,de->bse', u, w_in[l])
        pool_x, pool_z, m_x, m_z = jnp.split(proj, [D_POOL, 2 * D_POOL, 2 * D_POOL + D_MLSTM], axis=-1)
        y_pool = pool_mixer(pool_x, pool_w[l], pool_scale[l]) * jax.nn.silu(pool_z)
        y_m = mlstm_branch(m_x, conv_w[l], conv_b[l], w_q[l], w_k[l], w_v[l], w_gates[l],
                           b_gates[l], mh_norm_w[l], skip_w[l]) * jax.nn.silu(m_z)
        y = jnp.concatenate([y_pool.astype(h.dtype), y_m.astype(h.dtype)], axis=-1)
        h = h + jnp.einsum('bse,ed->bsd', y, w_out[l])
    return rms_norm(h, norm_out_g)
```

```python
import functools

import numpy as np
import jax
import jax.numpy as jnp
from jax import lax
from jax.experimental import pallas as pl
from jax.experimental.pallas import tpu as pltpu

D_MODEL = 1024
D_POOL = 1024
D_MLSTM = 1024
POOL_WINDOWS = (2, 4, 8, 16)
N_HEADS = 4
HEAD_DIM = 256
QKV_BLOCK = 4
CONV_WIDTH = 5
CHUNK = 128
EPS = 1e-6

HALO = 16
GATE_LANES = 128
G_I_FWD, G_F_FWD, G_I_BWD, G_F_BWD = 0, 8, 16, 24
IN_TILE = 512
VMEM_LIMIT = 48 * 1024 * 1024

F32 = jnp.float32
BF16 = jnp.bfloat16


def _inproj_kernel(x_ref, g_ref, w_ref, o_ref):
    x = x_ref[...]
    ms = jnp.mean(x * x, axis=-1, keepdims=True)
    u = (x * lax.rsqrt(ms + EPS) * g_ref[...]).astype(BF16)
    for n in range(4):
        cols = slice(n * D_MODEL, (n + 1) * D_MODEL)
        o_ref[:, cols] = jnp.dot(u, w_ref[:, cols], preferred_element_type=F32).astype(BF16)


def _silu(z):
    return z * (1.0 / (1.0 + jnp.exp(-z)))


def _log_sigmoid(g):
    return jnp.minimum(g, 0.0) - jnp.log1p(jnp.exp(-jnp.abs(g)))


def _lane_scan(x, op, reverse):
    lane = lax.broadcasted_iota(jnp.int32, x.shape, 1)
    k = 1
    while k < CHUNK:
        if reverse:
            shifted = pltpu.roll(x, CHUNK - k, 1)
            valid = lane < CHUNK - k
        else:
            shifted = pltpu.roll(x, k, 1)
            valid = lane >= k
        x = jnp.where(valid, op(x, shifted), x)
        k *= 2
    return x


def _lane_bcast(x, lane_idx):
    return jnp.broadcast_to(x[:, lane_idx:lane_idx + 1], x.shape)


def _mlstm_chunk(q_bf, k_bf, v_bf, q32, v32, gi, gf, c_sc, n_sc, m_sc, reverse):
    last = 0 if reverse else CHUNK - 1
    lf = _log_sigmoid(gf)
    b = _lane_scan(lf, jnp.add, reverse)
    a = gi - b
    m_prev = m_sc[...]
    big_m = jnp.maximum(m_prev, _lane_scan(a, jnp.maximum, reverse))
    inter_w = jnp.exp(m_prev - big_m)
    m_t = b + big_m
    exp_neg_mt = jnp.exp(-m_t)
    m_last = _lane_bcast(big_m, last)
    ws = jnp.exp(a - m_last)
    decay = jnp.exp(m_prev - m_last)
    m_sc[...] = _lane_bcast(m_t, last)

    rows = jnp.concatenate(
        [big_m, inter_w, exp_neg_mt, ws, jnp.zeros((CHUNK - 32, CHUNK), F32)], axis=0)
    cols = rows.T

    t_idx = lax.broadcasted_iota(jnp.int32, (CHUNK, CHUNK), 0)
    s_idx = lax.broadcasted_iota(jnp.int32, (CHUNK, CHUNK), 1)
    causal = (s_idx >= t_idx) if reverse else (s_idx <= t_idx)

    outs = []
    for h in range(N_HEADS):
        hs = slice(h * HEAD_DIM, (h + 1) * HEAD_DIM)
        m_col = cols[:, h:h + 1]
        iw_col = cols[:, 8 + h:9 + h]
        en_col = cols[:, 16 + h:17 + h]
        ws_col = cols[:, 24 + h:25 + h]
        qh, kh, vh = q_bf[:, hs], k_bf[:, hs], v_bf[:, hs]

        dmat = jnp.where(causal, jnp.exp(a[h:h + 1, :] - m_col), 0.0)
        s = lax.dot_general(qh, kh, (((1,), (1,)), ((), ())), preferred_element_type=F32) * dmat
        c_old = c_sc[h]
        q_c = jnp.dot(qh, c_old.astype(BF16), preferred_element_type=F32)
        num = jnp.dot(s.astype(BF16), vh, preferred_element_type=F32) + iw_col * q_c
        q_n = jnp.sum(q32[:, hs] * n_sc[h][0:1, :], axis=-1, keepdims=True)
        den = jnp.sum(s, axis=-1, keepdims=True) + iw_col * q_n
        outs.append(num / jnp.maximum(jnp.abs(den), en_col))

        dec = decay[h:h + 1, 0:1]
        vw = (v32[:, hs] * ws_col).astype(BF16)
        kv = lax.dot_general(kh, vw, (((0,), (0,)), ((), ())), preferred_element_type=F32)
        c_sc[h] = dec * c_old + kv
        ws8 = jnp.broadcast_to(ws[h:h + 1, :], (8, CHUNK)).astype(BF16)
        n_sc[h] = dec * n_sc[h] + jnp.dot(ws8, kh, preferred_element_type=F32)
    return outs


def _fwd_kernel(mxp_ref, mx_ref, mxn_ref, convw_ref, convb_ref, wq_ref, wk_ref, wv_ref,
                wg_ref, bg_ref,
                q_ref, k_ref, v_ref, c_ref, gb_ref, hf_ref,
                ext_sc, c_sc, n_sc, m_sc):
    j = pl.program_id(1)
    nc = pl.num_programs(1)

    @pl.when(j == 0)
    def _():
        c_sc[...] = jnp.zeros_like(c_sc)
        n_sc[...] = jnp.zeros_like(n_sc)
        m_sc[...] = jnp.zeros_like(m_sc)

    mx_bf = mx_ref[0]
    mx32 = mx_bf.astype(F32)
    ext_sc[0:HALO, :] = jnp.where(j == 0, 0.0, mxp_ref[0].astype(F32))
    ext_sc[HALO:HALO + CHUNK, :] = mx32
    ext_sc[HALO + CHUNK:, :] = jnp.where(j == nc - 1, 0.0, mxn_ref[0].astype(F32))

    pad = CONV_WIDTH // 2
    conv = jnp.broadcast_to(convb_ref[...], (CHUNK, D_MLSTM))
    for tap in range(CONV_WIDTH):
        conv = conv + ext_sc[pl.ds(HALO - pad + tap, CHUNK), :] * convw_ref[tap:tap + 1, :]
    c32 = _silu(conv)
    c_bf = c32.astype(BF16)
    c_ref[0] = c_bf

    q_parts, k_parts, v_parts = [], [], []
    gates = jnp.broadcast_to(bg_ref[...], (CHUNK, GATE_LANES))
    for h in range(N_HEADS):
        hs = slice(h * HEAD_DIM, (h + 1) * HEAD_DIM)
        qh = jnp.dot(c_bf[:, hs], wq_ref[h], preferred_element_type=F32)
        kh = jnp.dot(c_bf[:, hs], wk_ref[h], preferred_element_type=F32)
        vh = jnp.dot(mx_bf[:, hs], wv_ref[h], preferred_element_type=F32)
        q_parts.append(qh)
        k_parts.append(kh)
        v_parts.append(vh)
    q32 = jnp.concatenate(q_parts, axis=1)
    k32 = jnp.concatenate(k_parts, axis=1)
    v32 = jnp.concatenate(v_parts, axis=1)
    q_bf, k_bf, v_bf = q32.astype(BF16), k32.astype(BF16), v32.astype(BF16)
    q_ref[0], k_ref[0], v_ref[0] = q_bf, k_bf, v_bf
    gates = gates + jnp.dot(q_bf, wg_ref[0], preferred_element_type=F32)
    gates = gates + jnp.dot(k_bf, wg_ref[1], preferred_element_type=F32)
    gates = gates + jnp.dot(v_bf, wg_ref[2], preferred_element_type=F32)
    gates_t = gates.T
    gb_ref[0, 0] = gates_t[G_I_BWD:G_I_BWD + 16, :]

    outs = _mlstm_chunk(q_bf, k_bf, v_bf, q32, v32,
                        gates_t[G_I_FWD:G_I_FWD + 8, :], gates_t[G_F_FWD:G_F_FWD + 8, :],
                        c_sc, n_sc, m_sc, reverse=False)
    for h in range(N_HEADS):
        hf_ref[0, :, h * HEAD_DIM:(h + 1) * HEAD_DIM] = outs[h]


def _bwd_kernel(q_ref, k_ref, v_ref, c_ref, gb_ref, hf_ref,
                pxp_ref, px_ref, pxn_ref, pz_ref, mz_ref, x_ref,
                pmat_ref, poolw_ref, pscale_ref, mhw_ref, skipw_ref, wout_ref, gout_ref,
                o_ref,
                c_sc, n_sc, m_sc):
    j = pl.program_id(1)
    nc = pl.num_programs(1)
    chunk = nc - 1 - j

    @pl.when(j == 0)
    def _():
        c_sc[...] = jnp.zeros_like(c_sc)
        n_sc[...] = jnp.zeros_like(n_sc)
        m_sc[...] = jnp.zeros_like(m_sc)

    q_bf, k_bf, v_bf = q_ref[0], k_ref[0], v_ref[0]
    gb = gb_ref[0, 0]
    outs = _mlstm_chunk(q_bf, k_bf, v_bf, q_bf.astype(F32), v_bf.astype(F32),
                        gb[0:8, :], gb[8:16, :], c_sc, n_sc, m_sc, reverse=True)

    y_m_parts = []
    for h in range(N_HEADS):
        hs = slice(h * HEAD_DIM, (h + 1) * HEAD_DIM)
        ht = hf_ref[0, :, hs] + outs[h]
        mu = jnp.mean(ht, axis=-1, keepdims=True)
        d = ht - mu
        var = jnp.mean(d * d, axis=-1, keepdims=True)
        hn = d * lax.rsqrt(var + EPS)
        y = hn * mhw_ref[:, hs] + skipw_ref[:, hs] * c_ref[0, :, hs].astype(F32)
        y = y * _silu(mz_ref[0, :, hs].astype(F32))
        y_m_parts.append(y.astype(BF16))
    y_m = jnp.concatenate(y_m_parts, axis=1)

    zero_halo = jnp.zeros((HALO, D_POOL), BF16)
    px_main = px_ref[0]
    ext = jnp.concatenate(
        [jnp.where(chunk == 0, zero_halo, pxp_ref[0]), px_main,
         jnp.where(chunk == nc - 1, zero_halo, pxn_ref[0]),
         jnp.zeros((2 * CHUNK - CHUNK - 2 * HALO, D_POOL), BF16)], axis=0)
    t_glob = chunk * CHUNK + lax.broadcasted_iota(jnp.int32, (CHUNK, 1), 0)
    seq_last = nc * CHUNK - 1
    y_p_parts = []
    for g, w in enumerate(POOL_WINDOWS):
        gs = slice(g * HEAD_DIM, (g + 1) * HEAD_DIM)
        left = (w - 1) // 2
        right = w - 1 - left
        count = jnp.minimum(t_glob + right, seq_last) - jnp.maximum(t_glob - left, 0) + 1
        total = jnp.dot(pmat_ref[g], ext[:, gs], preferred_element_type=F32)
        pooled = total / count.astype(F32) - px_main[:, gs].astype(F32)
        mixed = jnp.dot(pooled.astype(BF16), poolw_ref[g], preferred_element_type=F32)
        y = mixed * pscale_ref[:, gs] * _silu(pz_ref[0, :, gs].astype(F32))
        y_p_parts.append(y.astype(BF16))
    y_p = jnp.concatenate(y_p_parts, axis=1)

    hres = x_ref[0] + jnp.dot(y_p, wout_ref[0:D_POOL, :], preferred_element_type=F32) \
        + jnp.dot(y_m, wout_ref[D_POOL:, :], preferred_element_type=F32)
    ms = jnp.mean(hres * hres, axis=-1, keepdims=True)
    o_ref[0] = hres * lax.rsqrt(ms + EPS) * gout_ref[...]


def _block_diag_tiles(w):
    per_tile = HEAD_DIM // QKV_BLOCK
    w4 = w.reshape(N_HEADS, per_tile, QKV_BLOCK, QKV_BLOCK)
    eye = jnp.eye(per_tile, dtype=w.dtype)
    dense = jnp.einsum('tnio,nm->tnimo', w4, eye)
    return dense.reshape(N_HEADS, HEAD_DIM, HEAD_DIM)


def _gate_weights(w_gates, b_gates):
    wg = jnp.zeros((3 * D_MLSTM, GATE_LANES), F32)
    bg = jnp.zeros((1, GATE_LANES), F32)
    for d, (gi, gf) in enumerate(((G_I_FWD, G_F_FWD), (G_I_BWD, G_F_BWD))):
        wg = wg.at[:, gi:gi + N_HEADS].set(w_gates[d, :, :N_HEADS])
        wg = wg.at[:, gf:gf + N_HEADS].set(w_gates[d, :, N_HEADS:])
        bg = bg.at[0, gi:gi + N_HEADS].set(b_gates[d, :N_HEADS])
        bg = bg.at[0, gf:gf + N_HEADS].set(b_gates[d, N_HEADS:])
    wg = wg.reshape(3, D_MLSTM, GATE_LANES)
    wg = wg.at[1].multiply(float(HEAD_DIM) ** 0.5)
    return wg, bg


def _pool_band_matrices():
    t = np.arange(CHUNK)[:, None]
    r = np.arange(2 * CHUNK)[None, :] - HALO
    mats = []
    for w in POOL_WINDOWS:
        left = (w - 1) // 2
        right = w - 1 - left
        mats.append(((r >= t - left) & (r <= t + right) & (r < CHUNK + HALO)).astype(np.float32))
    return jnp.asarray(np.stack(mats), dtype=BF16)


def _full(shape):
    return pl.BlockSpec(shape, lambda b, j: (0,) * len(shape))


def kernel(x, norm_in_g, w_in, pool_w, pool_scale, conv_w, conv_b, w_q, w_k, w_v, w_gates,
           b_gates, mh_norm_w, skip_w, w_out, norm_out_g):
    B, S, D = x.shape
    assert D == D_MODEL and S % CHUNK == 0 and (B * S) % IN_TILE == 0
    assert norm_in_g.shape[0] == 1, "single-layer block"
    nc = S // CHUNK
    tokens = B * S

    proj = pl.pallas_call(
        _inproj_kernel,
        grid=(tokens // IN_TILE,),
        in_specs=[pl.BlockSpec((IN_TILE, D), lambda i: (i, 0)),
                  pl.BlockSpec((1, D), lambda i: (0, 0)),
                  pl.BlockSpec((D, 4 * D), lambda i: (0, 0))],
        out_specs=pl.BlockSpec((IN_TILE, 4 * D), lambda i: (i, 0)),
        out_shape=jax.ShapeDtypeStruct((tokens, 4 * D), BF16),
        compiler_params=pltpu.CompilerParams(dimension_semantics=("arbitrary",),
                                             vmem_limit_bytes=VMEM_LIMIT),
        name="inproj",
    )(x.reshape(tokens, D), norm_in_g[0][None, :], w_in[0].astype(BF16))
    proj = proj.reshape(B, S, 4 * D)

    hpc = CHUNK // HALO
    n_halo = S // HALO
    wq_t = _block_diag_tiles(w_q[0]).astype(BF16)
    wk_t = (_block_diag_tiles(w_k[0]) * (float(HEAD_DIM) ** -0.5)).astype(BF16)
    wv_t = _block_diag_tiles(w_v[0]).astype(BF16)
    wg, bg = _gate_weights(w_gates[0], b_gates[0])
    conv_w8 = jnp.zeros((8, D_MLSTM), F32).at[:CONV_WIDTH].set(conv_w[0])

    def col_main(c):
        return pl.BlockSpec((1, CHUNK, D), lambda b, j: (b, j, c))

    def col_prev(c):
        return pl.BlockSpec((1, HALO, D), lambda b, j: (b, jnp.maximum(j * hpc - 1, 0), c))

    def col_next(c):
        return pl.BlockSpec((1, HALO, D), lambda b, j: (b, jnp.minimum((j + 1) * hpc, n_halo - 1), c))

    seq_spec = pl.BlockSpec((1, CHUNK, D), lambda b, j: (b, j, 0))
    state_scratch = [pltpu.VMEM((N_HEADS, HEAD_DIM, HEAD_DIM), F32),
                     pltpu.VMEM((N_HEADS, 8, HEAD_DIM), F32),
                     pltpu.VMEM((8, CHUNK), F32)]
    seq_bf = jax.ShapeDtypeStruct((B, S, D), BF16)
    q, k, v, c, gates_b, h_fwd = pl.pallas_call(
        _fwd_kernel,
        grid=(B, nc),
        in_specs=[col_prev(2), col_main(2), col_next(2),
                  _full((8, D)), _full((1, D)),
                  _full((N_HEADS, HEAD_DIM, HEAD_DIM)), _full((N_HEADS, HEAD_DIM, HEAD_DIM)),
                  _full((N_HEADS, HEAD_DIM, HEAD_DIM)),
                  _full((3, D, GATE_LANES)), _full((1, GATE_LANES))],
        out_specs=[seq_spec, seq_spec, seq_spec, seq_spec,
                   pl.BlockSpec((1, 1, 16, CHUNK), lambda b, j: (b, j, 0, 0)),
                   seq_spec],
        out_shape=[seq_bf, seq_bf, seq_bf, seq_bf,
                   jax.ShapeDtypeStruct((B, nc, 16, CHUNK), F32),
                   jax.ShapeDtypeStruct((B, S, D), F32)],
        scratch_shapes=[pltpu.VMEM((CHUNK + 2 * HALO, D), F32)] + state_scratch,
        compiler_params=pltpu.CompilerParams(dimension_semantics=("arbitrary", "arbitrary"),
                                             vmem_limit_bytes=VMEM_LIMIT),
        name="mlstm_fwd",
    )(proj, proj, proj, conv_w8, conv_b[0][None, :], wq_t, wk_t, wv_t, wg.astype(BF16), bg)

    def rev(j):
        return nc - 1 - j

    rseq_spec = pl.BlockSpec((1, CHUNK, D), lambda b, j: (b, rev(j), 0))

    def rcol_main(c):
        return pl.BlockSpec((1, CHUNK, D), lambda b, j: (b, rev(j), c))

    rcol_prev = pl.BlockSpec((1, HALO, D), lambda b, j: (b, jnp.maximum(rev(j) * hpc - 1, 0), 0))
    rcol_next = pl.BlockSpec(
        (1, HALO, D), lambda b, j: (b, jnp.minimum((rev(j) + 1) * hpc, n_halo - 1), 0))

    out = pl.pallas_call(
        _bwd_kernel,
        grid=(B, nc),
        in_specs=[rseq_spec, rseq_spec, rseq_spec, rseq_spec,
                  pl.BlockSpec((1, 1, 16, CHUNK), lambda b, j: (b, rev(j), 0, 0)),
                  rseq_spec,
                  rcol_prev, rcol_main(0), rcol_next, rcol_main(1), rcol_main(3),
                  rseq_spec,
                  _full((len(POOL_WINDOWS), CHUNK, 2 * CHUNK)),
                  _full((len(POOL_WINDOWS), HEAD_DIM, HEAD_DIM)),
                  _full((1, D)), _full((1, D)), _full((1, D)),
                  _full((2 * D, D)), _full((1, D))],
        out_specs=rseq_spec,
        out_shape=jax.ShapeDtypeStruct((B, S, D), F32),
        scratch_shapes=state_scratch,
        compiler_params=pltpu.CompilerParams(dimension_semantics=("arbitrary", "arbitrary"),
                                             vmem_limit_bytes=VMEM_LIMIT),
        name="mlstm_bwd_out",
    )(q, k, v, c, gates_b, h_fwd, proj, proj, proj, proj, proj, x,
      _pool_band_matrices(), pool_w[0].astype(BF16), pool_scale[0][None, :],
      mh_norm_w[0][None, :], skip_w[0][None, :], w_out[0].astype(BF16), norm_out_g[None, :])
    return out
```

```python
import numpy as np
import jax
import jax.numpy as jnp
from jax import lax
from jax.experimental import pallas as pl
from jax.experimental.pallas import tpu as pltpu

D_MODEL = 1024
D_POOL = 1024
D_MLSTM = 1024
POOL_WINDOWS = (2, 4, 8, 16)
POOL_GROUP_DIM = D_POOL // len(POOL_WINDOWS)
N_HEADS = 4
HEAD_DIM = 256
QKV_BLOCK = 4
CONV_WIDTH = 5
CHUNK = 128
N_DIRS = 2
EPS = 1e-6

HALO = 16
DIR_LANES = 128
I_LANE, F_LANE, CM_LANE = 0, 8, 16
SCAN_ROWS = 24
IN_TILE = 512
PREP_TILE = 256
OUT_TILE = 256
VMEM_LIMIT = 48 * 1024 * 1024

F32 = jnp.float32
BF16 = jnp.bfloat16


def _silu(z):
    return z * (1.0 / (1.0 + jnp.exp(-z)))


def _log_sigmoid(g):
    return jnp.minimum(g, 0.0) - jnp.log1p(jnp.exp(-jnp.abs(g)))


def _inproj_kernel(x_ref, g_ref, w_ref, o_ref):
    x = x_ref[...]
    ms = jnp.mean(x * x, axis=-1, keepdims=True)
    u = (x * lax.rsqrt(ms + EPS) * g_ref[...]).astype(BF16)
    for n in range(4):
        cols = slice(n * D_MODEL, (n + 1) * D_MODEL)
        o_ref[:, cols] = jnp.dot(u, w_ref[:, cols], preferred_element_type=F32).astype(BF16)


def _token_scan(x, op, reverse):
    t = lax.broadcasted_iota(jnp.int32, x.shape, 0)
    k = 1
    while k < CHUNK:
        if reverse:
            shifted = pltpu.roll(x, CHUNK - k, 0)
            valid = t < CHUNK - k
        else:
            shifted = pltpu.roll(x, k, 0)
            valid = t >= k
        x = jnp.where(valid, op(x, shifted), x)
        k *= 2
    return x


def _prep_kernel(mxp_ref, mx_ref, mxn_ref, convw_ref, convb_ref, wq_ref, wk_ref, wv_ref,
                 wg_ref, bg_ref,
                 q_ref, k_ref, v_ref, c_ref, colsf_ref, colsb_ref, rowsf_ref, rowsb_ref,
                 ext_sc):
    i = pl.program_id(1)
    n_tiles = pl.num_programs(1)
    tile = PREP_TILE

    mx_bf = mx_ref[0]
    ext_sc[0:HALO, :] = jnp.where(i == 0, 0.0, mxp_ref[0].astype(F32))
    ext_sc[HALO:HALO + tile, :] = mx_bf.astype(F32)
    ext_sc[HALO + tile:, :] = jnp.where(i == n_tiles - 1, 0.0, mxn_ref[0].astype(F32))

    pad = CONV_WIDTH // 2
    conv = jnp.broadcast_to(convb_ref[...], (tile, D_MLSTM))
    for tap in range(CONV_WIDTH):
        conv = conv + ext_sc[pl.ds(HALO - pad + tap, tile), :] * convw_ref[tap:tap + 1, :]
    c_bf = _silu(conv).astype(BF16)
    c_ref[0] = c_bf

    gates = jnp.broadcast_to(bg_ref[...], (tile, N_DIRS * DIR_LANES))
    for h in range(N_HEADS):
        hs = slice(h * HEAD_DIM, (h + 1) * HEAD_DIM)
        qh = jnp.dot(c_bf[:, hs], wq_ref[h], preferred_element_type=F32).astype(BF16)
        kh = jnp.dot(c_bf[:, hs], wk_ref[h], preferred_element_type=F32).astype(BF16)
        vh = jnp.dot(mx_bf[:, hs], wv_ref[h], preferred_element_type=F32).astype(BF16)
        q_ref[0, :, hs], k_ref[0, :, hs], v_ref[0, :, hs] = qh, kh, vh
        gates = gates + jnp.dot(qh, wg_ref[0, hs, :], preferred_element_type=F32)
        gates = gates + jnp.dot(kh, wg_ref[1, hs, :], preferred_element_type=F32)
        gates = gates + jnp.dot(vh, wg_ref[2, hs, :], preferred_element_type=F32)

    lane = lax.broadcasted_iota(jnp.int32, (CHUNK, DIR_LANES), 1)
    for ch in range(tile // CHUNK):
        rs = slice(ch * CHUNK, (ch + 1) * CHUNK)
        for d, (cols_ref, rows_ref) in enumerate(((colsf_ref, rowsf_ref), (colsb_ref, rowsb_ref))):
            g = gates[rs, d * DIR_LANES:(d + 1) * DIR_LANES]
            b = _token_scan(_log_sigmoid(g), jnp.add, reverse=(d == 1))
            a = g - pltpu.roll(b, DIR_LANES - (F_LANE - I_LANE), 1)
            cm = _token_scan(a, jnp.maximum, reverse=(d == 1))
            scan = jnp.where(lane < F_LANE, a,
                             jnp.where(lane < CM_LANE, b, pltpu.roll(cm, CM_LANE - I_LANE, 1)))
            cols_ref[0, rs, :] = scan
            rows_ref[0, ch] = scan.T[0:SCAN_ROWS, :]


def _sweep_direction(d, q_ref, k_ref, v_ref, cols_ref, rows_ref, h_ref, sc_ref, sc_base,
                     c_sc, n_sc, m_sc):
    reverse = d == 1
    t_idx = lax.broadcasted_iota(jnp.int32, (CHUNK, CHUNK), 0)
    s_idx = lax.broadcasted_iota(jnp.int32, (CHUNK, CHUNK), 1)
    causal = (s_idx >= t_idx) if reverse else (s_idx <= t_idx)
    cols = cols_ref[0]
    for h in range(N_HEADS):
        hs = slice(h * HEAD_DIM, (h + 1) * HEAD_DIM)
        st = d * N_HEADS + h
        m = m_sc[st]
        b_last = sc_ref[sc_base + h]
        cm_last = sc_ref[sc_base + N_HEADS + h]
        m_last = jnp.maximum(m, cm_last)
        m_sc[st] = b_last + m_last

        a_col = cols[:, I_LANE + h:I_LANE + h + 1]
        b_col = cols[:, F_LANE + h:F_LANE + h + 1]
        cm_col = cols[:, CM_LANE + h:CM_LANE + h + 1]
        a_row = rows_ref[0, 0, I_LANE + h:I_LANE + h + 1, :]
        big_m = jnp.maximum(cm_col, m)
        dmat = jnp.where(causal, jnp.exp(a_row - big_m), 0.0)
        inter_w = jnp.exp(m - big_m)
        exp_neg_mt = jnp.exp(-(b_col + big_m))
        ws_col = jnp.exp(a_col - m_last)
        ws_row = jnp.exp(a_row - m_last)
        decay = jnp.exp(jnp.full((1, HEAD_DIM), m - m_last, F32))

        qh, kh, vh = q_ref[0, :, hs], k_ref[0, :, hs], v_ref[0, :, hs]
        s = lax.dot_general(qh, kh, (((1,), (1,)), ((), ())), preferred_element_type=F32) * dmat
        c_old = c_sc[st]
        n_old = n_sc[st]
        q_c = jnp.dot(qh, c_old.astype(BF16), preferred_element_type=F32)
        num = jnp.dot(s.astype(BF16), vh, preferred_element_type=F32) + inter_w * q_c
        q_n = jnp.sum(qh.astype(F32) * n_old[0:1, :], axis=-1, keepdims=True)
        den = jnp.sum(s, axis=-1, keepdims=True) + inter_w * q_n
        h_ref[0, :, hs] = num / jnp.maximum(jnp.abs(den), exp_neg_mt)

        vw = (vh.astype(F32) * ws_col).astype(BF16)
        kv = lax.dot_general(kh, vw, (((0,), (0,)), ((), ())), preferred_element_type=F32)
        c_sc[st] = decay * c_old + kv
        ws8 = jnp.broadcast_to(ws_row, (8, CHUNK)).astype(BF16)
        n_sc[st] = decay * n_old + jnp.dot(ws8, kh, preferred_element_type=F32)


def _sweep_kernel(sc_ref,
                  qf_ref, kf_ref, vf_ref, colsf_ref, rowsf_ref,
                  qb_ref, kb_ref, vb_ref, colsb_ref, rowsb_ref,
                  hf_ref, hb_ref,
                  c_sc, n_sc, m_sc):
    b = pl.program_id(0)
    j = pl.program_id(1)
    nc = pl.num_programs(1)

    @pl.when(j == 0)
    def _():
        c_sc[...] = jnp.zeros_like(c_sc)
        n_sc[...] = jnp.zeros_like(n_sc)
        for st in range(N_DIRS * N_HEADS):
            m_sc[st] = jnp.float32(0.0)

    per_chunk = N_DIRS * 2 * N_HEADS
    base_f = (b * nc + j) * per_chunk
    base_b = (b * nc + (nc - 1 - j)) * per_chunk + 2 * N_HEADS
    _sweep_direction(0, qf_ref, kf_ref, vf_ref, colsf_ref, rowsf_ref, hf_ref, sc_ref, base_f,
                     c_sc, n_sc, m_sc)
    _sweep_direction(1, qb_ref, kb_ref, vb_ref, colsb_ref, rowsb_ref, hb_ref, sc_ref, base_b,
                     c_sc, n_sc, m_sc)


def _combine_kernel(hf_ref, hb_ref, c_ref, mz_ref, pxp_ref, px_ref, pxn_ref, pz_ref, x_ref,
                    pmat_ref, poolw_ref, pscale_ref, mhw_ref, skipw_ref, wout_ref, gout_ref,
                    o_ref):
    i = pl.program_id(1)
    n_tiles = pl.num_programs(1)
    tile = OUT_TILE

    y_m_parts = []
    for h in range(N_HEADS):
        hs = slice(h * HEAD_DIM, (h + 1) * HEAD_DIM)
        ht = hf_ref[0, :, hs] + hb_ref[0, :, hs]
        mu = jnp.mean(ht, axis=-1, keepdims=True)
        dlt = ht - mu
        var = jnp.mean(dlt * dlt, axis=-1, keepdims=True)
        hn = dlt * lax.rsqrt(var + EPS)
        y = hn * mhw_ref[:, hs] + skipw_ref[:, hs] * c_ref[0, :, hs].astype(F32)
        y = y * _silu(mz_ref[0, :, hs].astype(F32))
        y_m_parts.append(y.astype(BF16))
    y_m = jnp.concatenate(y_m_parts, axis=1)

    zero_halo = jnp.zeros((HALO, D_POOL), BF16)
    px_main = px_ref[0]
    ext = jnp.concatenate(
        [jnp.where(i == 0, zero_halo, pxp_ref[0]), px_main,
         jnp.where(i == n_tiles - 1, zero_halo, pxn_ref[0]),
         jnp.zeros((CHUNK - 2 * HALO, D_POOL), BF16)], axis=0)
    t_glob = i * tile + lax.broadcasted_iota(jnp.int32, (tile, 1), 0)
    seq_last = n_tiles * tile - 1
    y_p_parts = []
    for g, w in enumerate(POOL_WINDOWS):
        gs = slice(g * POOL_GROUP_DIM, (g + 1) * POOL_GROUP_DIM)
        left = (w - 1) // 2
        right = w - 1 - left
        count = jnp.minimum(t_glob + right, seq_last) - jnp.maximum(t_glob - left, 0) + 1
        total = jnp.concatenate(
            [jnp.dot(pmat_ref[g], ext[ch * CHUNK:(ch + 2) * CHUNK, gs], preferred_element_type=F32)
             for ch in range(tile // CHUNK)], axis=0)
        pooled = total / count.astype(F32) - px_main[:, gs].astype(F32)
        mixed = jnp.dot(pooled.astype(BF16), poolw_ref[g], preferred_element_type=F32)
        y = mixed * pscale_ref[:, gs] * _silu(pz_ref[0, :, gs].astype(F32))
        y_p_parts.append(y.astype(BF16))
    y_p = jnp.concatenate(y_p_parts, axis=1)

    hres = x_ref[0] + jnp.dot(y_p, wout_ref[0:D_POOL, :], preferred_element_type=F32) \
        + jnp.dot(y_m, wout_ref[D_POOL:, :], preferred_element_type=F32)
    ms = jnp.mean(hres * hres, axis=-1, keepdims=True)
    o_ref[0] = hres * lax.rsqrt(ms + EPS) * gout_ref[...]


def _block_diag_tiles(w):
    rows = w.reshape(N_HEADS, HEAD_DIM, QKV_BLOCK)
    col = np.arange(HEAD_DIM)
    spread = jnp.asarray((col[None, :] % QKV_BLOCK == np.arange(QKV_BLOCK)[:, None]), w.dtype)
    tiled = jnp.einsum('tro,oc->trc', rows, spread, precision=lax.Precision.HIGHEST)
    same_block = jnp.asarray(col[:, None] // QKV_BLOCK == col[None, :] // QKV_BLOCK)
    return jnp.where(same_block[None], tiled, 0.0)


def _gate_weights(w_gates, b_gates):
    wg = jnp.zeros((3 * D_MLSTM, N_DIRS * DIR_LANES), F32)
    bg = jnp.zeros((1, N_DIRS * DIR_LANES), F32)
    for d in range(N_DIRS):
        li, lf = d * DIR_LANES + I_LANE, d * DIR_LANES + F_LANE
        wg = wg.at[:, li:li + N_HEADS].set(w_gates[d, :, :N_HEADS])
        wg = wg.at[:, lf:lf + N_HEADS].set(w_gates[d, :, N_HEADS:])
        bg = bg.at[0, li:li + N_HEADS].set(b_gates[d, :N_HEADS])
        bg = bg.at[0, lf:lf + N_HEADS].set(b_gates[d, N_HEADS:])
    wg = wg.reshape(3, D_MLSTM, N_DIRS * DIR_LANES)
    wg = wg.at[1].multiply(float(HEAD_DIM) ** 0.5)
    return wg, bg


def _pool_band_matrices():
    t = np.arange(CHUNK)[:, None]
    r = np.arange(2 * CHUNK)[None, :] - HALO
    mats = []
    for w in POOL_WINDOWS:
        left = (w - 1) // 2
        right = w - 1 - left
        mats.append(((r >= t - left) & (r <= t + right)).astype(np.float32))
    return jnp.asarray(np.stack(mats), dtype=BF16)


def _full(shape):
    return pl.BlockSpec(shape, lambda b, j: (0,) * len(shape))


def _halo_specs(tile, n_halo, col):
    per = tile // HALO
    prev = pl.BlockSpec((1, HALO, D_MODEL), lambda b, i: (b, jnp.maximum(i * per - 1, 0), col))
    nxt = pl.BlockSpec((1, HALO, D_MODEL),
                       lambda b, i: (b, jnp.minimum((i + 1) * per, n_halo - 1), col))
    return prev, nxt


def kernel(x, norm_in_g, w_in, pool_w, pool_scale, conv_w, conv_b, w_q, w_k, w_v, w_gates,
           b_gates, mh_norm_w, skip_w, w_out, norm_out_g):
    B, S, D = x.shape
    assert D == D_MODEL and (B * S) % IN_TILE == 0
    assert S % PREP_TILE == 0 and S % OUT_TILE == 0 and PREP_TILE % CHUNK == 0
    assert norm_in_g.shape[0] == 1, "single-layer block"
    nc = S // CHUNK
    tokens = B * S
    n_halo = S // HALO
    arb2 = pltpu.CompilerParams(dimension_semantics=("arbitrary", "arbitrary"),
                                vmem_limit_bytes=VMEM_LIMIT)

    proj = pl.pallas_call(
        _inproj_kernel,
        grid=(tokens // IN_TILE,),
        in_specs=[pl.BlockSpec((IN_TILE, D), lambda i: (i, 0)),
                  pl.BlockSpec((1, D), lambda i: (0, 0)),
                  pl.BlockSpec((D, 4 * D), lambda i: (0, 0))],
        out_specs=pl.BlockSpec((IN_TILE, 4 * D), lambda i: (i, 0)),
        out_shape=jax.ShapeDtypeStruct((tokens, 4 * D), BF16),
        compiler_params=pltpu.CompilerParams(dimension_semantics=("arbitrary",),
                                             vmem_limit_bytes=VMEM_LIMIT),
        name="inproj",
    )(x.reshape(tokens, D), norm_in_g[0][None, :], w_in[0].astype(BF16))
    proj = proj.reshape(B, S, 4 * D)

    wq_t = _block_diag_tiles(w_q[0]).astype(BF16)
    wk_t = (_block_diag_tiles(w_k[0]) * (float(HEAD_DIM) ** -0.5)).astype(BF16)
    wv_t = _block_diag_tiles(w_v[0]).astype(BF16)
    wg, bg = _gate_weights(w_gates[0], b_gates[0])
    conv_w8 = jnp.zeros((8, D_MLSTM), F32).at[:CONV_WIDTH].set(conv_w[0])

    mx_prev, mx_next = _halo_specs(PREP_TILE, n_halo, 2)
    prep_seq = pl.BlockSpec((1, PREP_TILE, D), lambda b, i: (b, i, 0))
    prep_cols = pl.BlockSpec((1, PREP_TILE, DIR_LANES), lambda b, i: (b, i, 0))
    cpt = PREP_TILE // CHUNK
    prep_rows = pl.BlockSpec((1, cpt, SCAN_ROWS, CHUNK), lambda b, i: (b, i, 0, 0))
    seq_bf = jax.ShapeDtypeStruct((B, S, D), BF16)
    cols_shape = jax.ShapeDtypeStruct((B, S, DIR_LANES), F32)
    rows_shape = jax.ShapeDtypeStruct((B, nc, SCAN_ROWS, CHUNK), F32)
    q, k, v, c, cols_f, cols_b, rows_f, rows_b = pl.pallas_call(
        _prep_kernel,
        grid=(B, S // PREP_TILE),
        in_specs=[mx_prev, pl.BlockSpec((1, PREP_TILE, D), lambda b, i: (b, i, 2)), mx_next,
                  _full((8, D)), _full((1, D)),
                  _full((N_HEADS, HEAD_DIM, HEAD_DIM)), _full((N_HEADS, HEAD_DIM, HEAD_DIM)),
                  _full((N_HEADS, HEAD_DIM, HEAD_DIM)),
                  _full((3, D, N_DIRS * DIR_LANES)), _full((1, N_DIRS * DIR_LANES))],
        out_specs=[prep_seq, prep_seq, prep_seq, prep_seq, prep_cols, prep_cols, prep_rows, prep_rows],
        out_shape=[seq_bf, seq_bf, seq_bf, seq_bf, cols_shape, cols_shape, rows_shape, rows_shape],
        scratch_shapes=[pltpu.VMEM((PREP_TILE + 2 * HALO, D), F32)],
        compiler_params=arb2,
        name="prep",
    )(proj, proj, proj, conv_w8, conv_b[0][None, :], wq_t, wk_t, wv_t, wg.astype(BF16), bg)

    def last(rows, lane):
        return jnp.concatenate([rows[:, :, F_LANE:F_LANE + N_HEADS, lane],
                                rows[:, :, CM_LANE:CM_LANE + N_HEADS, lane]], axis=-1)

    chunk_scalars = jnp.concatenate([last(rows_f, CHUNK - 1), last(rows_b, 0)], axis=-1).reshape(-1)

    def fwd_idx(b, j):
        return j

    def bwd_idx(b, j):
        return nc - 1 - j

    def sweep_specs(idx):
        seq = pl.BlockSpec((1, CHUNK, D), lambda b, j: (b, idx(b, j), 0))
        cols = pl.BlockSpec((1, CHUNK, DIR_LANES), lambda b, j: (b, idx(b, j), 0))
        rows = pl.BlockSpec((1, 1, 8, CHUNK), lambda b, j: (b, idx(b, j), 0, 0))
        return seq, [seq, seq, seq, cols, rows]

    seq_f, in_f = sweep_specs(fwd_idx)
    seq_b, in_b = sweep_specs(bwd_idx)
    n_state = N_DIRS * N_HEADS
    h_fwd, h_bwd = pl.pallas_call(
        _sweep_kernel,
        grid=(B, nc),
        in_specs=[pl.BlockSpec(memory_space=pltpu.SMEM)] + in_f + in_b,
        out_specs=[seq_f, seq_b],
        out_shape=[jax.ShapeDtypeStruct((B, S, D), F32)] * 2,
        scratch_shapes=[pltpu.VMEM((n_state, HEAD_DIM, HEAD_DIM), F32),
                        pltpu.VMEM((n_state, 8, HEAD_DIM), F32),
                        pltpu.SMEM((n_state,), F32)],
        compiler_params=arb2,
        name="sweep",
    )(chunk_scalars, q, k, v, cols_f, rows_f, q, k, v, cols_b, rows_b)

    px_prev, px_next = _halo_specs(OUT_TILE, n_halo, 0)

    def out_col(col):
        return pl.BlockSpec((1, OUT_TILE, D), lambda b, i: (b, i, col))

    out_seq = out_col(0)
    out = pl.pallas_call(
        _combine_kernel,
        grid=(B, S // OUT_TILE),
        in_specs=[out_seq, out_seq, out_seq, out_col(3),
                  px_prev, out_col(0), px_next, out_col(1), out_seq,
                  _full((len(POOL_WINDOWS), CHUNK, 2 * CHUNK)),
                  _full((len(POOL_WINDOWS), POOL_GROUP_DIM, POOL_GROUP_DIM)),
                  _full((1, D)), _full((1, D)), _full((1, D)),
                  _full((2 * D, D)), _full((1, D))],
        out_specs=out_seq,
        out_shape=jax.ShapeDtypeStruct((B, S, D), F32),
        compiler_params=arb2,
        name="combine",
    )(h_fwd, h_bwd, c, proj, proj, proj, proj, proj, x,
      _pool_band_matrices(), pool_w[0].astype(BF16), pool_scale[0][None, :],
      mh_norm_w[0][None, :], skip_w[0][None, :], w_out[0].astype(BF16), norm_out_g[None, :])
    return out
```

```python
import numpy as np
import jax
import jax.numpy as jnp
from jax import lax
from jax.experimental import pallas as pl
from jax.experimental.pallas import tpu as pltpu

D_MODEL = 1024
D_POOL = 1024
D_MLSTM = 1024
POOL_WINDOWS = (2, 4, 8, 16)
POOL_GROUP_DIM = D_POOL // len(POOL_WINDOWS)
N_HEADS = 4
HEAD_DIM = 256
QKV_BLOCK = 4
CONV_WIDTH = 5
CHUNK = 128
N_DIRS = 2
EPS = 1e-6

HALO = 16
DIR_LANES = 128
I_LANE, F_LANE, CM_LANE = 0, 8, 16
SCAN_ROWS = 24
IN_TILE = 512
PREP_TILE = 256
OUT_TILE = 256
VMEM_LIMIT = 48 * 1024 * 1024

F32 = jnp.float32
BF16 = jnp.bfloat16


def _silu(z):
    return z * (1.0 / (1.0 + jnp.exp(-z)))


def _log_sigmoid(g):
    return jnp.minimum(g, 0.0) - jnp.log1p(jnp.exp(-jnp.abs(g)))


def _inproj_kernel(x_ref, g_ref, w_ref, o_ref):
    x = x_ref[...]
    ms = jnp.mean(x * x, axis=-1, keepdims=True)
    u = (x * lax.rsqrt(ms + EPS) * g_ref[...]).astype(BF16)
    for n in range(4):
        cols = slice(n * D_MODEL, (n + 1) * D_MODEL)
        o_ref[:, cols] = jnp.dot(u, w_ref[:, cols], preferred_element_type=F32).astype(BF16)


def _token_scan(x, op, reverse):
    t = lax.broadcasted_iota(jnp.int32, x.shape, 0)
    k = 1
    while k < CHUNK:
        if reverse:
            shifted = pltpu.roll(x, CHUNK - k, 0)
            valid = t < CHUNK - k
        else:
            shifted = pltpu.roll(x, k, 0)
            valid = t >= k
        x = jnp.where(valid, op(x, shifted), x)
        k *= 2
    return x


def _prep_kernel(mxp_ref, mx_ref, mxn_ref, convw_ref, convb_ref, wq_ref, wk_ref, wv_ref,
                 wg_ref, bg_ref,
                 q_ref, k_ref, kt_ref, v_ref, c_ref, colsf_ref, colsb_ref, rowsf_ref, rowsb_ref,
                 ext_sc):
    i = pl.program_id(1)
    n_tiles = pl.num_programs(1)
    tile = PREP_TILE

    mx_bf = mx_ref[0]
    ext_sc[0:HALO, :] = jnp.where(i == 0, 0.0, mxp_ref[0].astype(F32))
    ext_sc[HALO:HALO + tile, :] = mx_bf.astype(F32)
    ext_sc[HALO + tile:, :] = jnp.where(i == n_tiles - 1, 0.0, mxn_ref[0].astype(F32))

    pad = CONV_WIDTH // 2
    conv = jnp.broadcast_to(convb_ref[...], (tile, D_MLSTM))
    for tap in range(CONV_WIDTH):
        conv = conv + ext_sc[pl.ds(HALO - pad + tap, tile), :] * convw_ref[tap:tap + 1, :]
    c_bf = _silu(conv).astype(BF16)
    c_ref[0] = c_bf

    gates = jnp.broadcast_to(bg_ref[...], (tile, N_DIRS * DIR_LANES))
    for h in range(N_HEADS):
        hs = slice(h * HEAD_DIM, (h + 1) * HEAD_DIM)
        qh = jnp.dot(c_bf[:, hs], wq_ref[h], preferred_element_type=F32).astype(BF16)
        kh32 = jnp.dot(c_bf[:, hs], wk_ref[h], preferred_element_type=F32)
        kh = kh32.astype(BF16)
        for ch in range(tile // CHUNK):
            kt_ref[0, ch, hs, :] = kh32[ch * CHUNK:(ch + 1) * CHUNK, :].T.astype(BF16)
        vh = jnp.dot(mx_bf[:, hs], wv_ref[h], preferred_element_type=F32).astype(BF16)
        q_ref[0, :, hs], k_ref[0, :, hs], v_ref[0, :, hs] = qh, kh, vh
        gates = gates + jnp.dot(qh, wg_ref[0, hs, :], preferred_element_type=F32)
        gates = gates + jnp.dot(kh, wg_ref[1, hs, :], preferred_element_type=F32)
        gates = gates + jnp.dot(vh, wg_ref[2, hs, :], preferred_element_type=F32)

    lane = lax.broadcasted_iota(jnp.int32, (CHUNK, DIR_LANES), 1)
    for ch in range(tile // CHUNK):
        rs = slice(ch * CHUNK, (ch + 1) * CHUNK)
        for d, (cols_ref, rows_ref) in enumerate(((colsf_ref, rowsf_ref), (colsb_ref, rowsb_ref))):
            g = gates[rs, d * DIR_LANES:(d + 1) * DIR_LANES]
            b = _token_scan(_log_sigmoid(g), jnp.add, reverse=(d == 1))
            a = g - pltpu.roll(b, DIR_LANES - (F_LANE - I_LANE), 1)
            cm = _token_scan(a, jnp.maximum, reverse=(d == 1))
            scan = jnp.where(lane < F_LANE, a,
                             jnp.where(lane < CM_LANE, b, pltpu.roll(cm, CM_LANE - I_LANE, 1)))
            cols_ref[0, rs, :] = scan
            rows_ref[0, ch] = scan.T[0:SCAN_ROWS, :]


def _lane_bcast(tile, lane):
    return jnp.broadcast_to(tile[:, lane:lane + 1], tile.shape)


def _sweep_direction(d, q_ref, k_ref, kt_ref, v_ref, cols_ref, rows_ref, h_ref, sc_ref, sc_base,
                     c_sc, n_sc, m_sc):
    reverse = d == 1
    t_idx = lax.broadcasted_iota(jnp.int32, (CHUNK, CHUNK), 0)
    s_idx = lax.broadcasted_iota(jnp.int32, (CHUNK, CHUNK), 1)
    causal = (s_idx >= t_idx) if reverse else (s_idx <= t_idx)
    cols = cols_ref[0]
    for h in range(N_HEADS):
        hs = slice(h * HEAD_DIM, (h + 1) * HEAD_DIM)
        st = d * N_HEADS + h
        m = m_sc[st]
        b_last = sc_ref[sc_base + h]
        cm_last = sc_ref[sc_base + N_HEADS + h]
        m_last = jnp.maximum(m, cm_last)
        m_sc[st] = b_last + m_last

        cm_b = _lane_bcast(cols, CM_LANE + h)
        b_b = _lane_bcast(cols, F_LANE + h)
        a_row = rows_ref[0, 0, I_LANE + h:I_LANE + h + 1, :]
        big_m = jnp.maximum(cm_b, m)
        dmat = jnp.where(causal, jnp.exp(a_row - big_m), 0.0)
        inter_w = jnp.exp(m - big_m)
        exp_neg_mt = jnp.exp(-(b_b + big_m))
        ws_row = jnp.exp(a_row - m_last)
        decay = jnp.exp(jnp.full((1, HEAD_DIM), m - m_last, F32))

        qh, kh, vh = q_ref[0, :, hs], k_ref[0, :, hs], v_ref[0, :, hs]
        kt = kt_ref[0, 0, hs, :]
        s = jnp.dot(qh, kt, preferred_element_type=F32) * dmat
        c_old = c_sc[st]
        n_old = n_sc[st]
        q_c = jnp.dot(qh, c_old.astype(BF16), preferred_element_type=F32)
        q_n = qh.astype(F32) * n_old[0:1, :]
        den = jnp.sum(s + inter_w * (q_n[:, :CHUNK] + q_n[:, CHUNK:]), axis=-1, keepdims=True)
        inv = 1.0 / jnp.maximum(jnp.abs(den), exp_neg_mt[:, 0:1])
        inv_b = jnp.broadcast_to(inv, (CHUNK, CHUNK))
        num = jnp.dot(s.astype(BF16), vh, preferred_element_type=F32)
        h_ref[0, :, hs] = (num + jnp.concatenate([inter_w, inter_w], axis=1) * q_c) \
            * jnp.concatenate([inv_b, inv_b], axis=1)

        kts = (kt.astype(F32) * ws_row).astype(BF16)
        kv = jnp.dot(kts, vh, preferred_element_type=F32)
        c_sc[st] = decay * c_old + kv
        ws8 = jnp.broadcast_to(ws_row, (8, CHUNK)).astype(BF16)
        n_sc[st] = decay * n_old + jnp.dot(ws8, kh, preferred_element_type=F32)


def _sweep_kernel(sc_ref,
                  qf_ref, kf_ref, ktf_ref, vf_ref, colsf_ref, rowsf_ref,
                  qb_ref, kb_ref, ktb_ref, vb_ref, colsb_ref, rowsb_ref,
                  hf_ref, hb_ref,
                  c_sc, n_sc, m_sc):
    b = pl.program_id(0)
    j = pl.program_id(1)
    nc = pl.num_programs(1)

    @pl.when(j == 0)
    def _():
        c_sc[...] = jnp.zeros_like(c_sc)
        n_sc[...] = jnp.zeros_like(n_sc)
        for st in range(N_DIRS * N_HEADS):
            m_sc[st] = jnp.float32(0.0)

    per_chunk = N_DIRS * 2 * N_HEADS
    base_f = (b * nc + j) * per_chunk
    base_b = (b * nc + (nc - 1 - j)) * per_chunk + 2 * N_HEADS
    _sweep_direction(0, qf_ref, kf_ref, ktf_ref, vf_ref, colsf_ref, rowsf_ref, hf_ref, sc_ref,
                     base_f, c_sc, n_sc, m_sc)
    _sweep_direction(1, qb_ref, kb_ref, ktb_ref, vb_ref, colsb_ref, rowsb_ref, hb_ref, sc_ref,
                     base_b, c_sc, n_sc, m_sc)


def _combine_kernel(hf_ref, hb_ref, c_ref, mz_ref, pxp_ref, px_ref, pxn_ref, pz_ref, x_ref,
                    pmat_ref, poolw_ref, pscale_ref, mhw_ref, skipw_ref, wout_ref, gout_ref,
                    o_ref):
    i = pl.program_id(1)
    n_tiles = pl.num_programs(1)
    tile = OUT_TILE

    y_m_parts = []
    for h in range(N_HEADS):
        hs = slice(h * HEAD_DIM, (h + 1) * HEAD_DIM)
        ht = hf_ref[0, :, hs] + hb_ref[0, :, hs]
        mu = jnp.mean(ht, axis=-1, keepdims=True)
        dlt = ht - mu
        var = jnp.mean(dlt * dlt, axis=-1, keepdims=True)
        hn = dlt * lax.rsqrt(var + EPS)
        y = hn * mhw_ref[:, hs] + skipw_ref[:, hs] * c_ref[0, :, hs].astype(F32)
        y = y * _silu(mz_ref[0, :, hs].astype(F32))
        y_m_parts.append(y.astype(BF16))
    y_m = jnp.concatenate(y_m_parts, axis=1)

    zero_halo = jnp.zeros((HALO, D_POOL), BF16)
    px_main = px_ref[0]
    ext = jnp.concatenate(
        [jnp.where(i == 0, zero_halo, pxp_ref[0]), px_main,
         jnp.where(i == n_tiles - 1, zero_halo, pxn_ref[0]),
         jnp.zeros((CHUNK - 2 * HALO, D_POOL), BF16)], axis=0)
    t_glob = i * tile + lax.broadcasted_iota(jnp.int32, (tile, 1), 0)
    seq_last = n_tiles * tile - 1
    y_p_parts = []
    for g, w in enumerate(POOL_WINDOWS):
        gs = slice(g * POOL_GROUP_DIM, (g + 1) * POOL_GROUP_DIM)
        left = (w - 1) // 2
        right = w - 1 - left
        count = jnp.minimum(t_glob + right, seq_last) - jnp.maximum(t_glob - left, 0) + 1
        total = jnp.concatenate(
            [jnp.dot(pmat_ref[g], ext[ch * CHUNK:(ch + 2) * CHUNK, gs], preferred_element_type=F32)
             for ch in range(tile // CHUNK)], axis=0)
        pooled = total / count.astype(F32) - px_main[:, gs].astype(F32)
        mixed = jnp.dot(pooled.astype(BF16), poolw_ref[g], preferred_element_type=F32)
        y = mixed * pscale_ref[:, gs] * _silu(pz_ref[0, :, gs].astype(F32))
        y_p_parts.append(y.astype(BF16))
    y_p = jnp.concatenate(y_p_parts, axis=1)

    hres = x_ref[0] + jnp.dot(y_p, wout_ref[0:D_POOL, :], preferred_element_type=F32) \
        + jnp.dot(y_m, wout_ref[D_POOL:, :], preferred_element_type=F32)
    ms = jnp.mean(hres * hres, axis=-1, keepdims=True)
    o_ref[0] = hres * lax.rsqrt(ms + EPS) * gout_ref[...]


def _block_diag_tiles(w):
    rows = w.reshape(N_HEADS, HEAD_DIM, QKV_BLOCK)
    col = np.arange(HEAD_DIM)
    spread = jnp.asarray((col[None, :] % QKV_BLOCK == np.arange(QKV_BLOCK)[:, None]), w.dtype)
    tiled = jnp.einsum('tro,oc->trc', rows, spread, precision=lax.Precision.HIGHEST)
    same_block = jnp.asarray(col[:, None] // QKV_BLOCK == col[None, :] // QKV_BLOCK)
    return jnp.where(same_block[None], tiled, 0.0)


def _gate_weights(w_gates, b_gates):
    wg = jnp.zeros((3 * D_MLSTM, N_DIRS * DIR_LANES), F32)
    bg = jnp.zeros((1, N_DIRS * DIR_LANES), F32)
    for d in range(N_DIRS):
        li, lf = d * DIR_LANES + I_LANE, d * DIR_LANES + F_LANE
        wg = wg.at[:, li:li + N_HEADS].set(w_gates[d, :, :N_HEADS])
        wg = wg.at[:, lf:lf + N_HEADS].set(w_gates[d, :, N_HEADS:])
        bg = bg.at[0, li:li + N_HEADS].set(b_gates[d, :N_HEADS])
        bg = bg.at[0, lf:lf + N_HEADS].set(b_gates[d, N_HEADS:])
    wg = wg.reshape(3, D_MLSTM, N_DIRS * DIR_LANES)
    wg = wg.at[1].multiply(float(HEAD_DIM) ** 0.5)
    return wg, bg


def _pool_band_matrices():
    t = np.arange(CHUNK)[:, None]
    r = np.arange(2 * CHUNK)[None, :] - HALO
    mats = []
    for w in POOL_WINDOWS:
        left = (w - 1) // 2
        right = w - 1 - left
        mats.append(((r >= t - left) & (r <= t + right)).astype(np.float32))
    return jnp.asarray(np.stack(mats), dtype=BF16)


def _full(shape):
    return pl.BlockSpec(shape, lambda b, j: (0,) * len(shape))


def _halo_specs(tile, n_halo, col):
    per = tile // HALO
    prev = pl.BlockSpec((1, HALO, D_MODEL), lambda b, i: (b, jnp.maximum(i * per - 1, 0), col))
    nxt = pl.BlockSpec((1, HALO, D_MODEL),
                       lambda b, i: (b, jnp.minimum((i + 1) * per, n_halo - 1), col))
    return prev, nxt


def kernel(x, norm_in_g, w_in, pool_w, pool_scale, conv_w, conv_b, w_q, w_k, w_v, w_gates,
           b_gates, mh_norm_w, skip_w, w_out, norm_out_g):
    B, S, D = x.shape
    assert D == D_MODEL and (B * S) % IN_TILE == 0
    assert S % PREP_TILE == 0 and S % OUT_TILE == 0 and PREP_TILE % CHUNK == 0
    assert norm_in_g.shape[0] == 1, "single-layer block"
    nc = S // CHUNK
    tokens = B * S
    n_halo = S // HALO
    arb2 = pltpu.CompilerParams(dimension_semantics=("arbitrary", "arbitrary"),
                                vmem_limit_bytes=VMEM_LIMIT)

    proj = pl.pallas_call(
        _inproj_kernel,
        grid=(tokens // IN_TILE,),
        in_specs=[pl.BlockSpec((IN_TILE, D), lambda i: (i, 0)),
                  pl.BlockSpec((1, D), lambda i: (0, 0)),
                  pl.BlockSpec((D, 4 * D), lambda i: (0, 0))],
        out_specs=pl.BlockSpec((IN_TILE, 4 * D), lambda i: (i, 0)),
        out_shape=jax.ShapeDtypeStruct((tokens, 4 * D), BF16),
        compiler_params=pltpu.CompilerParams(dimension_semantics=("arbitrary",),
                                             vmem_limit_bytes=VMEM_LIMIT),
        name="inproj",
    )(x.reshape(tokens, D), norm_in_g[0][None, :], w_in[0].astype(BF16))
    proj = proj.reshape(B, S, 4 * D)

    wq_t = _block_diag_tiles(w_q[0]).astype(BF16)
    wk_t = (_block_diag_tiles(w_k[0]) * (float(HEAD_DIM) ** -0.5)).astype(BF16)
    wv_t = _block_diag_tiles(w_v[0]).astype(BF16)
    wg, bg = _gate_weights(w_gates[0], b_gates[0])
    conv_w8 = jnp.zeros((8, D_MLSTM), F32).at[:CONV_WIDTH].set(conv_w[0])

    mx_prev, mx_next = _halo_specs(PREP_TILE, n_halo, 2)
    prep_seq = pl.BlockSpec((1, PREP_TILE, D), lambda b, i: (b, i, 0))
    prep_cols = pl.BlockSpec((1, PREP_TILE, DIR_LANES), lambda b, i: (b, i, 0))
    cpt = PREP_TILE // CHUNK
    prep_rows = pl.BlockSpec((1, cpt, SCAN_ROWS, CHUNK), lambda b, i: (b, i, 0, 0))
    seq_bf = jax.ShapeDtypeStruct((B, S, D), BF16)
    cols_shape = jax.ShapeDtypeStruct((B, S, DIR_LANES), F32)
    rows_shape = jax.ShapeDtypeStruct((B, nc, SCAN_ROWS, CHUNK), F32)
    prep_kt = pl.BlockSpec((1, cpt, D, CHUNK), lambda b, i: (b, i, 0, 0))
    kt_shape = jax.ShapeDtypeStruct((B, nc, D, CHUNK), BF16)
    q, k, kt, v, c, cols_f, cols_b, rows_f, rows_b = pl.pallas_call(
        _prep_kernel,
        grid=(B, S // PREP_TILE),
        in_specs=[mx_prev, pl.BlockSpec((1, PREP_TILE, D), lambda b, i: (b, i, 2)), mx_next,
                  _full((8, D)), _full((1, D)),
                  _full((N_HEADS, HEAD_DIM, HEAD_DIM)), _full((N_HEADS, HEAD_DIM, HEAD_DIM)),
                  _full((N_HEADS, HEAD_DIM, HEAD_DIM)),
                  _full((3, D, N_DIRS * DIR_LANES)), _full((1, N_DIRS * DIR_LANES))],
        out_specs=[prep_seq, prep_seq, prep_kt, prep_seq, prep_seq,
                   prep_cols, prep_cols, prep_rows, prep_rows],
        out_shape=[seq_bf, seq_bf, kt_shape, seq_bf, seq_bf,
                   cols_shape, cols_shape, rows_shape, rows_shape],
        scratch_shapes=[pltpu.VMEM((PREP_TILE + 2 * HALO, D), F32)],
        compiler_params=arb2,
        name="prep",
    )(proj, proj, proj, conv_w8, conv_b[0][None, :], wq_t, wk_t, wv_t, wg.astype(BF16), bg)

    def last(rows, lane):
        return jnp.concatenate([rows[:, :, F_LANE:F_LANE + N_HEADS, lane],
                                rows[:, :, CM_LANE:CM_LANE + N_HEADS, lane]], axis=-1)

    chunk_scalars = jnp.concatenate([last(rows_f, CHUNK - 1), last(rows_b, 0)], axis=-1).reshape(-1)

    def fwd_idx(b, j):
        return j

    def bwd_idx(b, j):
        return nc - 1 - j

    def sweep_specs(idx):
        seq = pl.BlockSpec((1, CHUNK, D), lambda b, j: (b, idx(b, j), 0))
        cols = pl.BlockSpec((1, CHUNK, DIR_LANES), lambda b, j: (b, idx(b, j), 0))
        rows = pl.BlockSpec((1, 1, 8, CHUNK), lambda b, j: (b, idx(b, j), 0, 0))
        kt_spec = pl.BlockSpec((1, 1, D, CHUNK), lambda b, j: (b, idx(b, j), 0, 0))
        return seq, [seq, seq, kt_spec, seq, cols, rows]

    seq_f, in_f = sweep_specs(fwd_idx)
    seq_b, in_b = sweep_specs(bwd_idx)
    n_state = N_DIRS * N_HEADS
    h_fwd, h_bwd = pl.pallas_call(
        _sweep_kernel,
        grid=(B, nc),
        in_specs=[pl.BlockSpec(memory_space=pltpu.SMEM)] + in_f + in_b,
        out_specs=[seq_f, seq_b],
        out_shape=[jax.ShapeDtypeStruct((B, S, D), F32)] * 2,
        scratch_shapes=[pltpu.VMEM((n_state, HEAD_DIM, HEAD_DIM), F32),
                        pltpu.VMEM((n_state, 8, HEAD_DIM), F32),
                        pltpu.SMEM((n_state,), F32)],
        compiler_params=arb2,
        name="sweep",
    )(chunk_scalars, q, k, kt, v, cols_f, rows_f, q, k, kt, v, cols_b, rows_b)

    px_prev, px_next = _halo_specs(OUT_TILE, n_halo, 0)

    def out_col(col):
        return pl.BlockSpec((1, OUT_TILE, D), lambda b, i: (b, i, col))

    out_seq = out_col(0)
    out = pl.pallas_call(
        _combine_kernel,
        grid=(B, S // OUT_TILE),
        in_specs=[out_seq, out_seq, out_seq, out_col(3),
                  px_prev, out_col(0), px_next, out_col(1), out_seq,
                  _full((len(POOL_WINDOWS), CHUNK, 2 * CHUNK)),
                  _full((len(POOL_WINDOWS), POOL_GROUP_DIM, POOL_GROUP_DIM)),
                  _full((1, D)), _full((1, D)), _full((1, D)),
                  _full((2 * D, D)), _full((1, D))],
        out_specs=out_seq,
        out_shape=jax.ShapeDtypeStruct((B, S, D), F32),
        compiler_params=arb2,
        name="combine",
    )(h_fwd, h_bwd, c, proj, proj, proj, proj, proj, x,
      _pool_band_matrices(), pool_w[0].astype(BF16), pool_scale[0][None, :],
      mh_norm_w[0][None, :], skip_w[0][None, :], w_out[0].astype(BF16), norm_out_g[None, :])
    return out
```

```python
import numpy as np
import jax
import jax.numpy as jnp
from jax import lax
from jax.experimental import pallas as pl
from jax.experimental.pallas import tpu as pltpu

D_MODEL = 1024
D_POOL = 1024
D_MLSTM = 1024
POOL_WINDOWS = (2, 4, 8, 16)
POOL_GROUP_DIM = D_POOL // len(POOL_WINDOWS)
N_HEADS = 4
HEAD_DIM = 256
QKV_BLOCK = 4
CONV_WIDTH = 5
CHUNK = 128
N_DIRS = 2
EPS = 1e-6

X_GROUPS, Z_GROUPS = (0, 2), (1, 3)
HALO = 16
DIR_LANES = 128
I_LANE, F_LANE, CM_LANE = 0, 8, 16
SCAN_ROWS = 24
IN_TILE = 512
PREP_TILE = 256
OUT_TILE = 256
VMEM_LIMIT = 48 * 1024 * 1024

F32 = jnp.float32
BF16 = jnp.bfloat16


def _silu(z):
    return z * (1.0 / (1.0 + jnp.exp(-z)))


def _log_sigmoid(g):
    return jnp.minimum(g, 0.0) - jnp.log1p(jnp.exp(-jnp.abs(g)))


def _inproj_kernel(x_ref, g_ref, w_ref, o_ref):
    x = x_ref[...]
    ms = jnp.mean(x * x, axis=-1, keepdims=True)
    u = (x * lax.rsqrt(ms + EPS) * g_ref[...]).astype(BF16)
    for n in Z_GROUPS + X_GROUPS:
        cols = slice(n * D_MODEL, (n + 1) * D_MODEL)
        acc = jnp.dot(u, w_ref[:, cols], preferred_element_type=F32)
        if n in Z_GROUPS:
            acc = _silu(acc)
        o_ref[:, cols] = acc.astype(BF16)


def _token_scan(x, op, reverse):
    t = lax.broadcasted_iota(jnp.int32, x.shape, 0)
    k = 1
    while k < CHUNK:
        if reverse:
            shifted = pltpu.roll(x, CHUNK - k, 0)
            valid = t < CHUNK - k
        else:
            shifted = pltpu.roll(x, k, 0)
            valid = t >= k
        x = jnp.where(valid, op(x, shifted), x)
        k *= 2
    return x


def _prep_kernel(mxp_ref, mx_ref, mxn_ref, convw_ref, convb_ref, wq_ref, wk_ref, wv_ref,
                 wg_ref, bg_ref,
                 q_ref, k_ref, kt_ref, v_ref, c_ref, colsf_ref, colsb_ref, rowsf_ref, rowsb_ref):
    i = pl.program_id(1)
    n_tiles = pl.num_programs(1)
    tile = PREP_TILE

    mx_bf = mx_ref[0]
    zero_halo = jnp.zeros((HALO, D_MLSTM), BF16)
    ext = jnp.concatenate([jnp.where(i == 0, zero_halo, mxp_ref[0]), mx_bf,
                           jnp.where(i == n_tiles - 1, zero_halo, mxn_ref[0])], axis=0).astype(F32)
    rows = tile + 2 * HALO

    pad = CONV_WIDTH // 2
    conv = jnp.broadcast_to(convb_ref[...], (tile, D_MLSTM))
    for tap in range(CONV_WIDTH):
        d = tap - pad
        shifted = ext if d == 0 else pltpu.roll(ext, (rows - d) % rows, 0)
        conv = conv + shifted[HALO:HALO + tile, :] * convw_ref[tap:tap + 1, :]
    c_bf = _silu(conv).astype(BF16)
    c_ref[0] = c_bf

    gates = jnp.broadcast_to(bg_ref[...], (tile, N_DIRS * DIR_LANES))
    for h in range(N_HEADS):
        hs = slice(h * HEAD_DIM, (h + 1) * HEAD_DIM)
        qh = jnp.dot(c_bf[:, hs], wq_ref[h], preferred_element_type=F32).astype(BF16)
        kh32 = jnp.dot(c_bf[:, hs], wk_ref[h], preferred_element_type=F32)
        kh = kh32.astype(BF16)
        for ch in range(tile // CHUNK):
            kt_ref[0, ch, hs, :] = kh32[ch * CHUNK:(ch + 1) * CHUNK, :].T.astype(BF16)
        vh = jnp.dot(mx_bf[:, hs], wv_ref[h], preferred_element_type=F32).astype(BF16)
        q_ref[0, :, hs], k_ref[0, :, hs], v_ref[0, :, hs] = qh, kh, vh
        gates = gates + jnp.dot(qh, wg_ref[0, hs, :], preferred_element_type=F32)
        gates = gates + jnp.dot(kh, wg_ref[1, hs, :], preferred_element_type=F32)
        gates = gates + jnp.dot(vh, wg_ref[2, hs, :], preferred_element_type=F32)

    lane = lax.broadcasted_iota(jnp.int32, (CHUNK, DIR_LANES), 1)
    for ch in range(tile // CHUNK):
        rs = slice(ch * CHUNK, (ch + 1) * CHUNK)
        for d, (cols_ref, rows_ref) in enumerate(((colsf_ref, rowsf_ref), (colsb_ref, rowsb_ref))):
            g = gates[rs, d * DIR_LANES:(d + 1) * DIR_LANES]
            b = _token_scan(_log_sigmoid(g), jnp.add, reverse=(d == 1))
            a = g - pltpu.roll(b, DIR_LANES - (F_LANE - I_LANE), 1)
            cm = _token_scan(a, jnp.maximum, reverse=(d == 1))
            scan = jnp.where(lane < F_LANE, a,
                             jnp.where(lane < CM_LANE, b, pltpu.roll(cm, CM_LANE - I_LANE, 1)))
            cols_ref[0, rs, :] = scan
            rows_ref[0, ch] = scan.T[0:SCAN_ROWS, :]


def _lane_bcast(tile, lane):
    return jnp.broadcast_to(tile[:, lane:lane + 1], tile.shape)


def _sweep_direction(d, q_ref, k_ref, kt_ref, v_ref, cols_ref, rows_ref, h_ref, sc_ref, sc_base,
                     c_sc, n_sc, m_sc):
    reverse = d == 1
    t_idx = lax.broadcasted_iota(jnp.int32, (CHUNK, CHUNK), 0)
    s_idx = lax.broadcasted_iota(jnp.int32, (CHUNK, CHUNK), 1)
    causal = (s_idx >= t_idx) if reverse else (s_idx <= t_idx)
    cols = cols_ref[0]
    for h in range(N_HEADS):
        hs = slice(h * HEAD_DIM, (h + 1) * HEAD_DIM)
        st = d * N_HEADS + h
        m = m_sc[st]
        b_last = sc_ref[sc_base + h]
        cm_last = sc_ref[sc_base + N_HEADS + h]
        m_last = jnp.maximum(m, cm_last)
        m_sc[st] = b_last + m_last

        cm_b = _lane_bcast(cols, CM_LANE + h)
        b_b = _lane_bcast(cols, F_LANE + h)
        a_row = rows_ref[0, 0, I_LANE + h:I_LANE + h + 1, :]
        big_m = jnp.maximum(cm_b, m)
        dmat = jnp.where(causal, jnp.exp(a_row - big_m), 0.0)
        inter_w = jnp.exp(m - big_m)
        exp_neg_mt = jnp.exp(-(b_b + big_m))
        ws_row = jnp.exp(a_row - m_last)
        decay = jnp.exp(jnp.full((1, HEAD_DIM), m - m_last, F32))

        qh, kh, vh = q_ref[0, :, hs], k_ref[0, :, hs], v_ref[0, :, hs]
        kt = kt_ref[0, 0, hs, :]
        s = jnp.dot(qh, kt, preferred_element_type=F32) * dmat
        c_old = c_sc[st]
        n_old = n_sc[st]
        q_c = jnp.dot(qh, c_old.astype(BF16), preferred_element_type=F32)
        q_n = qh.astype(F32) * n_old[0:1, :]
        den = jnp.sum(s + inter_w * (q_n[:, :CHUNK] + q_n[:, CHUNK:]), axis=-1, keepdims=True)
        inv = 1.0 / jnp.maximum(jnp.abs(den), exp_neg_mt[:, 0:1])
        inv_b = jnp.broadcast_to(inv, (CHUNK, CHUNK))
        num = jnp.dot(s.astype(BF16), vh, preferred_element_type=F32)
        h_ref[0, :, hs] = (num + jnp.concatenate([inter_w, inter_w], axis=1) * q_c) \
            * jnp.concatenate([inv_b, inv_b], axis=1)

        kts = (kt.astype(F32) * ws_row).astype(BF16)
        kv = jnp.dot(kts, vh, preferred_element_type=F32)
        c_sc[st] = decay * c_old + kv
        ws8 = jnp.broadcast_to(ws_row, (8, CHUNK)).astype(BF16)
        n_sc[st] = decay * n_old + jnp.dot(ws8, kh, preferred_element_type=F32)


def _sweep_kernel(sc_ref,
                  qf_ref, kf_ref, ktf_ref, vf_ref, colsf_ref, rowsf_ref,
                  qb_ref, kb_ref, ktb_ref, vb_ref, colsb_ref, rowsb_ref,
                  hf_ref, hb_ref,
                  c_sc, n_sc, m_sc):
    b = pl.program_id(0)
    j = pl.program_id(1)
    nc = pl.num_programs(1)

    @pl.when(j == 0)
    def _():
        c_sc[...] = jnp.zeros_like(c_sc)
        n_sc[...] = jnp.zeros_like(n_sc)
        for st in range(N_DIRS * N_HEADS):
            m_sc[st] = jnp.float32(0.0)

    per_chunk = N_DIRS * 2 * N_HEADS
    base_f = (b * nc + j) * per_chunk
    base_b = (b * nc + (nc - 1 - j)) * per_chunk + 2 * N_HEADS
    _sweep_direction(0, qf_ref, kf_ref, ktf_ref, vf_ref, colsf_ref, rowsf_ref, hf_ref, sc_ref,
                     base_f, c_sc, n_sc, m_sc)
    _sweep_direction(1, qb_ref, kb_ref, ktb_ref, vb_ref, colsb_ref, rowsb_ref, hb_ref, sc_ref,
                     base_b, c_sc, n_sc, m_sc)


def _combine_kernel(hf_ref, hb_ref, c_ref, szm_ref, pxp_ref, px_ref, pxn_ref, szp_ref, x_ref,
                    pmat_ref, poolw_ref, mhw_ref, skipw_ref, wout_ref, gout_ref,
                    o_ref):
    i = pl.program_id(1)
    n_tiles = pl.num_programs(1)
    tile = OUT_TILE

    y_m_parts = []
    for h in range(N_HEADS):
        hs = slice(h * HEAD_DIM, (h + 1) * HEAD_DIM)
        ht = hf_ref[0, :, hs] + hb_ref[0, :, hs]
        mu = jnp.mean(ht, axis=-1, keepdims=True)
        dlt = ht - mu
        var = jnp.mean(dlt * dlt, axis=-1, keepdims=True)
        hn = (dlt * lax.rsqrt(var + EPS) * mhw_ref[:, hs]).astype(BF16)
        y_m_parts.append((hn + skipw_ref[:, hs] * c_ref[0, :, hs]) * szm_ref[0, :, hs])
    y_m = jnp.concatenate(y_m_parts, axis=1)

    zero_halo = jnp.zeros((HALO, D_POOL), BF16)
    px_main = px_ref[0]
    ext = jnp.concatenate(
        [jnp.where(i == 0, zero_halo, pxp_ref[0]), px_main,
         jnp.where(i == n_tiles - 1, zero_halo, pxn_ref[0]),
         jnp.zeros((CHUNK - 2 * HALO, D_POOL), BF16)], axis=0)
    t_glob = i * tile + lax.broadcasted_iota(jnp.int32, (tile, 1), 0)
    seq_last = n_tiles * tile - 1
    y_p_parts = []
    for g, w in enumerate(POOL_WINDOWS):
        gs = slice(g * POOL_GROUP_DIM, (g + 1) * POOL_GROUP_DIM)
        left = (w - 1) // 2
        right = w - 1 - left
        count = jnp.minimum(t_glob + right, seq_last) - jnp.maximum(t_glob - left, 0) + 1
        total = jnp.concatenate(
            [jnp.dot(pmat_ref[g], ext[ch * CHUNK:(ch + 2) * CHUNK, gs], preferred_element_type=F32)
             for ch in range(tile // CHUNK)], axis=0)
        pooled = total / count.astype(F32) - px_main[:, gs].astype(F32)
        mixed = jnp.dot(pooled.astype(BF16), poolw_ref[g], preferred_element_type=F32)
        y_p_parts.append(mixed.astype(BF16) * szp_ref[0, :, gs])
    y_p = jnp.concatenate(y_p_parts, axis=1)

    hres = x_ref[0] + jnp.dot(y_p, wout_ref[0:D_POOL, :], preferred_element_type=F32) \
        + jnp.dot(y_m, wout_ref[D_POOL:, :], preferred_element_type=F32)
    ms = jnp.mean(hres * hres, axis=-1, keepdims=True)
    o_ref[0] = hres * lax.rsqrt(ms + EPS) * gout_ref[...]


def _block_diag_tiles(w):
    rows = w.reshape(N_HEADS, HEAD_DIM, QKV_BLOCK)
    col = np.arange(HEAD_DIM)
    spread = jnp.asarray((col[None, :] % QKV_BLOCK == np.arange(QKV_BLOCK)[:, None]), w.dtype)
    tiled = jnp.einsum('tro,oc->trc', rows, spread, precision=lax.Precision.HIGHEST)
    same_block = jnp.asarray(col[:, None] // QKV_BLOCK == col[None, :] // QKV_BLOCK)
    return jnp.where(same_block[None], tiled, 0.0)


def _gate_weights(w_gates, b_gates):
    def to_lanes(g):
        rows = g.shape[0]
        gap = jnp.zeros((rows, N_DIRS, F_LANE - I_LANE - N_HEADS), F32)
        tail = jnp.zeros((rows, N_DIRS, DIR_LANES - F_LANE - N_HEADS), F32)
        lanes = jnp.concatenate([g[..., :N_HEADS], gap, g[..., N_HEADS:], tail], axis=-1)
        return lanes.reshape(rows, N_DIRS * DIR_LANES)

    row_scale = jnp.repeat(jnp.asarray([1.0, float(HEAD_DIM) ** 0.5, 1.0], F32), D_MLSTM)
    wg = to_lanes(jnp.transpose(w_gates, (1, 0, 2)) * row_scale[:, None, None])
    return wg.reshape(3, D_MLSTM, N_DIRS * DIR_LANES), to_lanes(b_gates[None])


def _pool_band_matrices():
    t = np.arange(CHUNK)[:, None]
    r = np.arange(2 * CHUNK)[None, :] - HALO
    mats = []
    for w in POOL_WINDOWS:
        left = (w - 1) // 2
        right = w - 1 - left
        mats.append(((r >= t - left) & (r <= t + right)).astype(np.float32))
    return jnp.asarray(np.stack(mats), dtype=BF16)


def _full(shape):
    return pl.BlockSpec(shape, lambda b, j: (0,) * len(shape))


def _halo_specs(tile, n_halo, col):
    per = tile // HALO
    prev = pl.BlockSpec((1, HALO, D_MODEL), lambda b, i: (b, jnp.maximum(i * per - 1, 0), col))
    nxt = pl.BlockSpec((1, HALO, D_MODEL),
                       lambda b, i: (b, jnp.minimum((i + 1) * per, n_halo - 1), col))
    return prev, nxt


def kernel(x, norm_in_g, w_in, pool_w, pool_scale, conv_w, conv_b, w_q, w_k, w_v, w_gates,
           b_gates, mh_norm_w, skip_w, w_out, norm_out_g):
    B, S, D = x.shape
    assert D == D_MODEL and (B * S) % IN_TILE == 0
    assert S % PREP_TILE == 0 and S % OUT_TILE == 0 and PREP_TILE % CHUNK == 0
    assert norm_in_g.shape[0] == 1, "single-layer block"
    nc = S // CHUNK
    tokens = B * S
    n_halo = S // HALO
    arb2 = pltpu.CompilerParams(dimension_semantics=("arbitrary", "arbitrary"),
                                vmem_limit_bytes=VMEM_LIMIT)

    proj = pl.pallas_call(
        _inproj_kernel,
        grid=(tokens // IN_TILE,),
        in_specs=[pl.BlockSpec((IN_TILE, D), lambda i: (i, 0)),
                  pl.BlockSpec((1, D), lambda i: (0, 0)),
                  pl.BlockSpec((D, 4 * D), lambda i: (0, 0))],
        out_specs=pl.BlockSpec((IN_TILE, 4 * D), lambda i: (i, 0)),
        out_shape=jax.ShapeDtypeStruct((tokens, 4 * D), BF16),
        compiler_params=pltpu.CompilerParams(dimension_semantics=("arbitrary",),
                                             vmem_limit_bytes=VMEM_LIMIT),
        name="inproj",
    )(x.reshape(tokens, D), norm_in_g[0][None, :], w_in[0].astype(BF16))
    proj = proj.reshape(B, S, 4 * D)

    wq_t = _block_diag_tiles(w_q[0]).astype(BF16)
    wk_t = (_block_diag_tiles(w_k[0]) * (float(HEAD_DIM) ** -0.5)).astype(BF16)
    wv_t = _block_diag_tiles(w_v[0]).astype(BF16)
    wg, bg = _gate_weights(w_gates[0], b_gates[0])
    conv_w8 = jnp.zeros((8, D_MLSTM), F32).at[:CONV_WIDTH].set(conv_w[0])

    mx_prev, mx_next = _halo_specs(PREP_TILE, n_halo, 2)
    prep_seq = pl.BlockSpec((1, PREP_TILE, D), lambda b, i: (b, i, 0))
    prep_cols = pl.BlockSpec((1, PREP_TILE, DIR_LANES), lambda b, i: (b, i, 0))
    cpt = PREP_TILE // CHUNK
    prep_rows = pl.BlockSpec((1, cpt, SCAN_ROWS, CHUNK), lambda b, i: (b, i, 0, 0))
    seq_bf = jax.ShapeDtypeStruct((B, S, D), BF16)
    cols_shape = jax.ShapeDtypeStruct((B, S, DIR_LANES), F32)
    rows_shape = jax.ShapeDtypeStruct((B, nc, SCAN_ROWS, CHUNK), F32)
    prep_kt = pl.BlockSpec((1, cpt, D, CHUNK), lambda b, i: (b, i, 0, 0))
    kt_shape = jax.ShapeDtypeStruct((B, nc, D, CHUNK), BF16)
    q, k, kt, v, c, cols_f, cols_b, rows_f, rows_b = pl.pallas_call(
        _prep_kernel,
        grid=(B, S // PREP_TILE),
        in_specs=[mx_prev, pl.BlockSpec((1, PREP_TILE, D), lambda b, i: (b, i, 2)), mx_next,
                  _full((8, D)), _full((1, D)),
                  _full((N_HEADS, HEAD_DIM, HEAD_DIM)), _full((N_HEADS, HEAD_DIM, HEAD_DIM)),
                  _full((N_HEADS, HEAD_DIM, HEAD_DIM)),
                  _full((3, D, N_DIRS * DIR_LANES)), _full((1, N_DIRS * DIR_LANES))],
        out_specs=[prep_seq, prep_seq, prep_kt, prep_seq, prep_seq,
                   prep_cols, prep_cols, prep_rows, prep_rows],
        out_shape=[seq_bf, seq_bf, kt_shape, seq_bf, seq_bf,
                   cols_shape, cols_shape, rows_shape, rows_shape],
        compiler_params=arb2,
        name="prep",
    )(proj, proj, proj, conv_w8, conv_b[0][None, :], wq_t, wk_t, wv_t, wg.astype(BF16), bg)

    def last(rows, lane):
        return jnp.concatenate([rows[:, :, F_LANE:F_LANE + N_HEADS, lane],
                                rows[:, :, CM_LANE:CM_LANE + N_HEADS, lane]], axis=-1)

    chunk_scalars = jnp.concatenate([last(rows_f, CHUNK - 1), last(rows_b, 0)], axis=-1).reshape(-1)

    def fwd_idx(b, j):
        return j

    def bwd_idx(b, j):
        return nc - 1 - j

    def sweep_specs(idx):
        seq = pl.BlockSpec((1, CHUNK, D), lambda b, j: (b, idx(b, j), 0))
        cols = pl.BlockSpec((1, CHUNK, DIR_LANES), lambda b, j: (b, idx(b, j), 0))
        rows = pl.BlockSpec((1, 1, 8, CHUNK), lambda b, j: (b, idx(b, j), 0, 0))
        kt_spec = pl.BlockSpec((1, 1, D, CHUNK), lambda b, j: (b, idx(b, j), 0, 0))
        return seq, [seq, seq, kt_spec, seq, cols, rows]

    seq_f, in_f = sweep_specs(fwd_idx)
    seq_b, in_b = sweep_specs(bwd_idx)
    n_state = N_DIRS * N_HEADS
    h_fwd, h_bwd = pl.pallas_call(
        _sweep_kernel,
        grid=(B, nc),
        in_specs=[pl.BlockSpec(memory_space=pltpu.SMEM)] + in_f + in_b,
        out_specs=[seq_f, seq_b],
        out_shape=[jax.ShapeDtypeStruct((B, S, D), F32)] * 2,
        scratch_shapes=[pltpu.VMEM((n_state, HEAD_DIM, HEAD_DIM), F32),
                        pltpu.VMEM((n_state, 8, HEAD_DIM), F32),
                        pltpu.SMEM((n_state,), F32)],
        compiler_params=arb2,
        name="sweep",
    )(chunk_scalars, q, k, kt, v, cols_f, rows_f, q, k, kt, v, cols_b, rows_b)

    px_prev, px_next = _halo_specs(OUT_TILE, n_halo, 0)
    pool_w_scaled = pool_w[0] * pool_scale[0].reshape(len(POOL_WINDOWS), 1, POOL_GROUP_DIM)

    def out_col(col):
        return pl.BlockSpec((1, OUT_TILE, D), lambda b, i: (b, i, col))

    out_seq = out_col(0)
    out = pl.pallas_call(
        _combine_kernel,
        grid=(B, S // OUT_TILE),
        in_specs=[out_seq, out_seq, out_seq, out_col(3),
                  px_prev, out_col(0), px_next, out_col(1), out_seq,
                  _full((len(POOL_WINDOWS), CHUNK, 2 * CHUNK)),
                  _full((len(POOL_WINDOWS), POOL_GROUP_DIM, POOL_GROUP_DIM)),
                  _full((1, D)), _full((1, D)),
                  _full((2 * D, D)), _full((1, D))],
        out_specs=out_seq,
        out_shape=jax.ShapeDtypeStruct((B, S, D), F32),
        compiler_params=arb2,
        name="combine",
    )(h_fwd, h_bwd, c, proj, proj, proj, proj, proj, x,
      _pool_band_matrices(), pool_w_scaled.astype(BF16),
      mh_norm_w[0][None, :], skip_w[0][None, :].astype(BF16), w_out[0].astype(BF16),
      norm_out_g[None, :])
    return out
```

```python
import numpy as np
import jax
import jax.numpy as jnp
from jax import lax
from jax.experimental import pallas as pl
from jax.experimental.pallas import tpu as pltpu

D_MODEL = 1024
D_POOL = 1024
D_MLSTM = 1024
POOL_WINDOWS = (2, 4, 8, 16)
POOL_GROUP_DIM = D_POOL // len(POOL_WINDOWS)
N_HEADS = 4
HEAD_DIM = 256
QKV_BLOCK = 4
CONV_WIDTH = 5
CHUNK = 128
N_DIRS = 2
EPS = 1e-6

X_GROUPS, Z_GROUPS = (0, 2), (1, 3)
HALO = 16
DIR_LANES = 128
I_LANE, F_LANE, CM_LANE = 0, 8, 16
SCAN_ROWS = 24
IN_TILE = 512
PREP_TILE = 256
OUT_TILE = 256
VMEM_LIMIT = 48 * 1024 * 1024

F32 = jnp.float32
BF16 = jnp.bfloat16


def _silu(z):
    return z * (1.0 / (1.0 + jnp.exp(-z)))


def _log_sigmoid(g):
    return jnp.minimum(g, 0.0) - jnp.log1p(jnp.exp(-jnp.abs(g)))


def _inproj_kernel(x_ref, g_ref, w_ref, o_ref):
    x = x_ref[...]
    ms = jnp.mean(x * x, axis=-1, keepdims=True)
    u = (x * lax.rsqrt(ms + EPS) * g_ref[...]).astype(BF16)
    for n in Z_GROUPS + X_GROUPS:
        cols = slice(n * D_MODEL, (n + 1) * D_MODEL)
        acc = jnp.dot(u, w_ref[:, cols], preferred_element_type=F32)
        if n in Z_GROUPS:
            acc = _silu(acc)
        o_ref[:, cols] = acc.astype(BF16)


def _token_scan(x, op, reverse):
    t = lax.broadcasted_iota(jnp.int32, x.shape, 0)
    k = 1
    while k < CHUNK:
        if reverse:
            shifted = pltpu.roll(x, CHUNK - k, 0)
            valid = t < CHUNK - k
        else:
            shifted = pltpu.roll(x, k, 0)
            valid = t >= k
        x = jnp.where(valid, op(x, shifted), x)
        k *= 2
    return x


def _prep_kernel(mxp_ref, mx_ref, mxn_ref, convw_ref, convb_ref, wq_ref, wk_ref, wv_ref,
                 wg_ref, bg_ref,
                 q_ref, kt_ref, v_ref, c_ref, colsf_ref, colsb_ref, rowsf_ref, rowsb_ref):
    i = pl.program_id(1)
    n_tiles = pl.num_programs(1)
    tile = PREP_TILE

    mx_bf = mx_ref[0]
    zero_halo = jnp.zeros((HALO, D_MLSTM), BF16)
    ext = jnp.concatenate([jnp.where(i == 0, zero_halo, mxp_ref[0]), mx_bf,
                           jnp.where(i == n_tiles - 1, zero_halo, mxn_ref[0])], axis=0).astype(F32)
    rows = tile + 2 * HALO

    pad = CONV_WIDTH // 2
    conv = jnp.broadcast_to(convb_ref[...], (tile, D_MLSTM))
    for tap in range(CONV_WIDTH):
        d = tap - pad
        shifted = ext if d == 0 else pltpu.roll(ext, (rows - d) % rows, 0)
        conv = conv + shifted[HALO:HALO + tile, :] * convw_ref[tap:tap + 1, :]
    c_bf = _silu(conv).astype(BF16)
    c_ref[0] = c_bf

    gates = jnp.broadcast_to(bg_ref[...], (tile, N_DIRS * DIR_LANES))
    for h in range(N_HEADS):
        hs = slice(h * HEAD_DIM, (h + 1) * HEAD_DIM)
        qh = jnp.dot(c_bf[:, hs], wq_ref[h], preferred_element_type=F32).astype(BF16)
        kh32 = jnp.dot(c_bf[:, hs], wk_ref[h], preferred_element_type=F32)
        kh = kh32.astype(BF16)
        for ch in range(tile // CHUNK):
            kt_ref[0, ch, hs, :] = kh32[ch * CHUNK:(ch + 1) * CHUNK, :].T.astype(BF16)
        vh = jnp.dot(mx_bf[:, hs], wv_ref[h], preferred_element_type=F32).astype(BF16)
        q_ref[0, :, hs], v_ref[0, :, hs] = qh, vh
        gates = gates + jnp.dot(qh, wg_ref[0, hs, :], preferred_element_type=F32)
        gates = gates + jnp.dot(kh, wg_ref[1, hs, :], preferred_element_type=F32)
        gates = gates + jnp.dot(vh, wg_ref[2, hs, :], preferred_element_type=F32)

    lane = lax.broadcasted_iota(jnp.int32, (CHUNK, DIR_LANES), 1)
    for ch in range(tile // CHUNK):
        rs = slice(ch * CHUNK, (ch + 1) * CHUNK)
        for d, (cols_ref, rows_ref) in enumerate(((colsf_ref, rowsf_ref), (colsb_ref, rowsb_ref))):
            g = gates[rs, d * DIR_LANES:(d + 1) * DIR_LANES]
            b = _token_scan(_log_sigmoid(g), jnp.add, reverse=(d == 1))
            a = g - pltpu.roll(b, DIR_LANES - (F_LANE - I_LANE), 1)
            cm = _token_scan(a, jnp.maximum, reverse=(d == 1))
            scan = jnp.where(lane < F_LANE, a,
                             jnp.where(lane < CM_LANE, b, pltpu.roll(cm, CM_LANE - I_LANE, 1)))
            cols_ref[0, rs, :] = scan
            rows_ref[0, ch] = scan.T[0:SCAN_ROWS, :]


def _lane_bcast(tile, lane):
    return jnp.broadcast_to(tile[:, lane:lane + 1], tile.shape)


def _sweep_state_phase(d, h, q_ref, kt_ref, v_ref, rows_ref, sc_ref, sc_base, c_sc, n_sc, m_sc):
    hs = slice(h * HEAD_DIM, (h + 1) * HEAD_DIM)
    st = d * N_HEADS + h
    m = m_sc[st]
    b_last = sc_ref[sc_base + h]
    cm_last = sc_ref[sc_base + N_HEADS + h]
    m_last = jnp.maximum(m, cm_last)
    m_sc[st] = b_last + m_last

    a_row = rows_ref[0, 0, I_LANE + h:I_LANE + h + 1, :]
    ws_row = jnp.exp(a_row - m_last)
    decay = jnp.exp(jnp.full((1, HEAD_DIM), m - m_last, F32))

    qh, vh = q_ref[0, :, hs], v_ref[0, :, hs]
    kt = kt_ref[0, 0, hs, :]
    c_old = c_sc[st]
    n_old = n_sc[st]
    q_c = jnp.dot(qh, c_old.astype(BF16), preferred_element_type=F32)
    qk = jnp.dot(qh, kt, preferred_element_type=F32)
    q_n = qh.astype(F32) * n_old[0:1, :]
    q_n = q_n[:, :CHUNK] + q_n[:, CHUNK:]

    ws_bf = ws_row.astype(BF16)
    kv = jnp.dot(kt * ws_bf, vh, preferred_element_type=F32)
    c_sc[st] = decay * c_old + kv
    ws8 = jnp.broadcast_to(ws_bf, (8, CHUNK))
    n_sc[st] = decay * n_old + lax.dot_general(ws8, kt, (((1,), (1,)), ((), ())),
                                               preferred_element_type=F32)
    return m, a_row, q_c, qk, q_n


def _sweep_output_phase(d, h, state, v_ref, cols, causal, h_ref):
    hs = slice(h * HEAD_DIM, (h + 1) * HEAD_DIM)
    m, a_row, q_c, qk, q_n = state
    big_m = jnp.maximum(_lane_bcast(cols, CM_LANE + h), m)
    dmat = jnp.where(causal, jnp.exp(a_row - big_m), 0.0)
    inter_w = jnp.exp(m - big_m)
    exp_neg_mt = jnp.exp(-(_lane_bcast(cols, F_LANE + h) + big_m))
    s = qk * dmat
    den = jnp.sum(s + inter_w * q_n, axis=-1, keepdims=True)
    inv = 1.0 / jnp.maximum(jnp.abs(den), exp_neg_mt[:, 0:1])
    inv_b = jnp.broadcast_to(inv, (CHUNK, CHUNK))
    num = jnp.dot(s.astype(BF16), v_ref[0, :, hs], preferred_element_type=F32)
    out = (num + jnp.concatenate([inter_w, inter_w], axis=1) * q_c) \
        * jnp.concatenate([inv_b, inv_b], axis=1)
    h_ref[0, :, hs] = out.astype(h_ref.dtype)


def _sweep_kernel(sc_ref,
                  qf_ref, ktf_ref, vf_ref, colsf_ref, rowsf_ref,
                  qb_ref, ktb_ref, vb_ref, colsb_ref, rowsb_ref,
                  hf_ref, hb_ref,
                  c_sc, n_sc, m_sc):
    b = pl.program_id(0)
    j = pl.program_id(1)
    nc = pl.num_programs(1)

    @pl.when(j == 0)
    def _():
        c_sc[...] = jnp.zeros_like(c_sc)
        n_sc[...] = jnp.zeros_like(n_sc)
        for st in range(N_DIRS * N_HEADS):
            m_sc[st] = jnp.float32(0.0)

    per_chunk = N_DIRS * 2 * N_HEADS
    base_f = (b * nc + j) * per_chunk
    base_b = (b * nc + (nc - 1 - j)) * per_chunk + 2 * N_HEADS
    dirs = ((qf_ref, ktf_ref, vf_ref, colsf_ref, rowsf_ref, hf_ref, base_f),
            (qb_ref, ktb_ref, vb_ref, colsb_ref, rowsb_ref, hb_ref, base_b))
    units = [(d, h) for h in range(N_HEADS) for d in range(N_DIRS)]
    states = {}
    for d, h in units:
        q_ref, kt_ref, v_ref, _, rows_ref, _, base = dirs[d]
        states[d, h] = _sweep_state_phase(d, h, q_ref, kt_ref, v_ref, rows_ref, sc_ref, base,
                                          c_sc, n_sc, m_sc)
    t_idx = lax.broadcasted_iota(jnp.int32, (CHUNK, CHUNK), 0)
    s_idx = lax.broadcasted_iota(jnp.int32, (CHUNK, CHUNK), 1)
    causal = (s_idx <= t_idx, s_idx >= t_idx)
    cols = (colsf_ref[0], colsb_ref[0])
    for d, h in units:
        _sweep_output_phase(d, h, states[d, h], dirs[d][2], cols[d], causal[d], dirs[d][5])


def _combine_kernel(hf_ref, hb_ref, c_ref, szm_ref, pxp_ref, px_ref, pxn_ref, szp_ref, x_ref,
                    pmat_ref, poolw_ref, mhw_ref, skipw_ref, wout_ref, gout_ref,
                    o_ref):
    i = pl.program_id(1)
    n_tiles = pl.num_programs(1)
    tile = OUT_TILE

    y_m_parts = []
    for h in range(N_HEADS):
        hs = slice(h * HEAD_DIM, (h + 1) * HEAD_DIM)
        ht = hf_ref[0, :, hs].astype(F32) + hb_ref[0, :, hs].astype(F32)
        mu = jnp.mean(ht, axis=-1, keepdims=True)
        dlt = ht - mu
        var = jnp.mean(dlt * dlt, axis=-1, keepdims=True)
        hn = (dlt * lax.rsqrt(var + EPS) * mhw_ref[:, hs]).astype(BF16)
        y_m_parts.append((hn + skipw_ref[:, hs] * c_ref[0, :, hs]) * szm_ref[0, :, hs])
    y_m = jnp.concatenate(y_m_parts, axis=1)

    zero_halo = jnp.zeros((HALO, D_POOL), BF16)
    px_main = px_ref[0]
    ext = jnp.concatenate(
        [jnp.where(i == 0, zero_halo, pxp_ref[0]), px_main,
         jnp.where(i == n_tiles - 1, zero_halo, pxn_ref[0]),
         jnp.zeros((CHUNK - 2 * HALO, D_POOL), BF16)], axis=0)
    t_glob = i * tile + lax.broadcasted_iota(jnp.int32, (tile, 1), 0)
    seq_last = n_tiles * tile - 1
    y_p_parts = []
    for g, w in enumerate(POOL_WINDOWS):
        gs = slice(g * POOL_GROUP_DIM, (g + 1) * POOL_GROUP_DIM)
        left = (w - 1) // 2
        right = w - 1 - left
        count = jnp.minimum(t_glob + right, seq_last) - jnp.maximum(t_glob - left, 0) + 1
        total = jnp.concatenate(
            [jnp.dot(pmat_ref[g], ext[ch * CHUNK:(ch + 2) * CHUNK, gs], preferred_element_type=F32)
             for ch in range(tile // CHUNK)], axis=0)
        pooled = total / count.astype(F32) - px_main[:, gs].astype(F32)
        mixed = jnp.dot(pooled.astype(BF16), poolw_ref[g], preferred_element_type=F32)
        y_p_parts.append(mixed.astype(BF16) * szp_ref[0, :, gs])
    y_p = jnp.concatenate(y_p_parts, axis=1)

    hres = x_ref[0] + jnp.dot(y_p, wout_ref[0:D_POOL, :], preferred_element_type=F32) \
        + jnp.dot(y_m, wout_ref[D_POOL:, :], preferred_element_type=F32)
    ms = jnp.mean(hres * hres, axis=-1, keepdims=True)
    o_ref[0] = hres * lax.rsqrt(ms + EPS) * gout_ref[...]


def _block_diag_tiles(w):
    rows = w.reshape(N_HEADS, HEAD_DIM, QKV_BLOCK)
    col = np.arange(HEAD_DIM)
    spread = jnp.asarray((col[None, :] % QKV_BLOCK == np.arange(QKV_BLOCK)[:, None]), w.dtype)
    tiled = jnp.einsum('tro,oc->trc', rows, spread, precision=lax.Precision.HIGHEST)
    same_block = jnp.asarray(col[:, None] // QKV_BLOCK == col[None, :] // QKV_BLOCK)
    return jnp.where(same_block[None], tiled, 0.0)


def _gate_weights(w_gates, b_gates):
    def to_lanes(g):
        rows = g.shape[0]
        gap = jnp.zeros((rows, N_DIRS, F_LANE - I_LANE - N_HEADS), F32)
        tail = jnp.zeros((rows, N_DIRS, DIR_LANES - F_LANE - N_HEADS), F32)
        lanes = jnp.concatenate([g[..., :N_HEADS], gap, g[..., N_HEADS:], tail], axis=-1)
        return lanes.reshape(rows, N_DIRS * DIR_LANES)

    row_scale = jnp.repeat(jnp.asarray([1.0, float(HEAD_DIM) ** 0.5, 1.0], F32), D_MLSTM)
    wg = to_lanes(jnp.transpose(w_gates, (1, 0, 2)) * row_scale[:, None, None])
    return wg.reshape(3, D_MLSTM, N_DIRS * DIR_LANES), to_lanes(b_gates[None])


def _pool_band_matrices():
    t = np.arange(CHUNK)[:, None]
    r = np.arange(2 * CHUNK)[None, :] - HALO
    mats = []
    for w in POOL_WINDOWS:
        left = (w - 1) // 2
        right = w - 1 - left
        mats.append(((r >= t - left) & (r <= t + right)).astype(np.float32))
    return jnp.asarray(np.stack(mats), dtype=BF16)


def _full(shape):
    return pl.BlockSpec(shape, lambda b, j: (0,) * len(shape))


def _halo_specs(tile, n_halo, col):
    per = tile // HALO
    prev = pl.BlockSpec((1, HALO, D_MODEL), lambda b, i: (b, jnp.maximum(i * per - 1, 0), col))
    nxt = pl.BlockSpec((1, HALO, D_MODEL),
                       lambda b, i: (b, jnp.minimum((i + 1) * per, n_halo - 1), col))
    return prev, nxt


def kernel(x, norm_in_g, w_in, pool_w, pool_scale, conv_w, conv_b, w_q, w_k, w_v, w_gates,
           b_gates, mh_norm_w, skip_w, w_out, norm_out_g):
    B, S, D = x.shape
    assert D == D_MODEL and (B * S) % IN_TILE == 0
    assert S % PREP_TILE == 0 and S % OUT_TILE == 0 and PREP_TILE % CHUNK == 0
    assert norm_in_g.shape[0] == 1, "single-layer block"
    nc = S // CHUNK
    tokens = B * S
    n_halo = S // HALO
    arb2 = pltpu.CompilerParams(dimension_semantics=("arbitrary", "arbitrary"),
                                vmem_limit_bytes=VMEM_LIMIT)

    proj = pl.pallas_call(
        _inproj_kernel,
        grid=(tokens // IN_TILE,),
        in_specs=[pl.BlockSpec((IN_TILE, D), lambda i: (i, 0)),
                  pl.BlockSpec((1, D), lambda i: (0, 0)),
                  pl.BlockSpec((D, 4 * D), lambda i: (0, 0))],
        out_specs=pl.BlockSpec((IN_TILE, 4 * D), lambda i: (i, 0)),
        out_shape=jax.ShapeDtypeStruct((tokens, 4 * D), BF16),
        compiler_params=pltpu.CompilerParams(dimension_semantics=("arbitrary",),
                                             vmem_limit_bytes=VMEM_LIMIT),
        name="inproj",
    )(x.reshape(tokens, D), norm_in_g[0][None, :], w_in[0].astype(BF16))
    proj = proj.reshape(B, S, 4 * D)

    wq_t = _block_diag_tiles(w_q[0]).astype(BF16)
    wk_t = (_block_diag_tiles(w_k[0]) * (float(HEAD_DIM) ** -0.5)).astype(BF16)
    wv_t = _block_diag_tiles(w_v[0]).astype(BF16)
    wg, bg = _gate_weights(w_gates[0], b_gates[0])
    conv_w8 = jnp.zeros((8, D_MLSTM), F32).at[:CONV_WIDTH].set(conv_w[0])

    mx_prev, mx_next = _halo_specs(PREP_TILE, n_halo, 2)
    prep_seq = pl.BlockSpec((1, PREP_TILE, D), lambda b, i: (b, i, 0))
    prep_cols = pl.BlockSpec((1, PREP_TILE, DIR_LANES), lambda b, i: (b, i, 0))
    cpt = PREP_TILE // CHUNK
    prep_rows = pl.BlockSpec((1, cpt, SCAN_ROWS, CHUNK), lambda b, i: (b, i, 0, 0))
    seq_bf = jax.ShapeDtypeStruct((B, S, D), BF16)
    cols_shape = jax.ShapeDtypeStruct((B, S, DIR_LANES), F32)
    rows_shape = jax.ShapeDtypeStruct((B, nc, SCAN_ROWS, CHUNK), F32)
    prep_kt = pl.BlockSpec((1, cpt, D, CHUNK), lambda b, i: (b, i, 0, 0))
    kt_shape = jax.ShapeDtypeStruct((B, nc, D, CHUNK), BF16)
    q, kt, v, c, cols_f, cols_b, rows_f, rows_b = pl.pallas_call(
        _prep_kernel,
        grid=(B, S // PREP_TILE),
        in_specs=[mx_prev, pl.BlockSpec((1, PREP_TILE, D), lambda b, i: (b, i, 2)), mx_next,
                  _full((8, D)), _full((1, D)),
                  _full((N_HEADS, HEAD_DIM, HEAD_DIM)), _full((N_HEADS, HEAD_DIM, HEAD_DIM)),
                  _full((N_HEADS, HEAD_DIM, HEAD_DIM)),
                  _full((3, D, N_DIRS * DIR_LANES)), _full((1, N_DIRS * DIR_LANES))],
        out_specs=[prep_seq, prep_kt, prep_seq, prep_seq,
                   prep_cols, prep_cols, prep_rows, prep_rows],
        out_shape=[seq_bf, kt_shape, seq_bf, seq_bf,
                   cols_shape, cols_shape, rows_shape, rows_shape],
        compiler_params=arb2,
        name="prep",
    )(proj, proj, proj, conv_w8, conv_b[0][None, :], wq_t, wk_t, wv_t, wg.astype(BF16), bg)

    def last(rows, lane):
        return jnp.concatenate([rows[:, :, F_LANE:F_LANE + N_HEADS, lane],
                                rows[:, :, CM_LANE:CM_LANE + N_HEADS, lane]], axis=-1)

    chunk_scalars = jnp.concatenate([last(rows_f, CHUNK - 1), last(rows_b, 0)], axis=-1).reshape(-1)

    def fwd_idx(b, j):
        return j

    def bwd_idx(b, j):
        return nc - 1 - j

    def sweep_specs(idx):
        seq = pl.BlockSpec((1, CHUNK, D), lambda b, j: (b, idx(b, j), 0))
        cols = pl.BlockSpec((1, CHUNK, DIR_LANES), lambda b, j: (b, idx(b, j), 0))
        rows = pl.BlockSpec((1, 1, 8, CHUNK), lambda b, j: (b, idx(b, j), 0, 0))
        kt_spec = pl.BlockSpec((1, 1, D, CHUNK), lambda b, j: (b, idx(b, j), 0, 0))
        return seq, [seq, kt_spec, seq, cols, rows]

    seq_f, in_f = sweep_specs(fwd_idx)
    seq_b, in_b = sweep_specs(bwd_idx)
    n_state = N_DIRS * N_HEADS
    h_fwd, h_bwd = pl.pallas_call(
        _sweep_kernel,
        grid=(B, nc),
        in_specs=[pl.BlockSpec(memory_space=pltpu.SMEM)] + in_f + in_b,
        out_specs=[seq_f, seq_b],
        out_shape=[seq_bf, seq_bf],
        scratch_shapes=[pltpu.VMEM((n_state, HEAD_DIM, HEAD_DIM), F32),
                        pltpu.VMEM((n_state, 8, HEAD_DIM), F32),
                        pltpu.SMEM((n_state,), F32)],
        compiler_params=arb2,
        name="sweep",
    )(chunk_scalars, q, kt, v, cols_f, rows_f, q, kt, v, cols_b, rows_b)

    px_prev, px_next = _halo_specs(OUT_TILE, n_halo, 0)
    pool_w_scaled = pool_w[0] * pool_scale[0].reshape(len(POOL_WINDOWS), 1, POOL_GROUP_DIM)

    def out_col(col):
        return pl.BlockSpec((1, OUT_TILE, D), lambda b, i: (b, i, col))

    out_seq = out_col(0)
    out = pl.pallas_call(
        _combine_kernel,
        grid=(B, S // OUT_TILE),
        in_specs=[out_seq, out_seq, out_seq, out_col(3),
                  px_prev, out_col(0), px_next, out_col(1), out_seq,
                  _full((len(POOL_WINDOWS), CHUNK, 2 * CHUNK)),
                  _full((len(POOL_WINDOWS), POOL_GROUP_DIM, POOL_GROUP_DIM)),
                  _full((1, D)), _full((1, D)),
                  _full((2 * D, D)), _full((1, D))],
        out_specs=out_seq,
        out_shape=jax.ShapeDtypeStruct((B, S, D), F32),
        compiler_params=arb2,
        name="combine",
    )(h_fwd, h_bwd, c, proj, proj, proj, proj, proj, x,
      _pool_band_matrices(), pool_w_scaled.astype(BF16),
      mh_norm_w[0][None, :], skip_w[0][None, :].astype(BF16), w_out[0].astype(BF16),
      norm_out_g[None, :])
    return out
```

```python
import numpy as np
import jax
import jax.numpy as jnp
from jax import lax
from jax.experimental import pallas as pl
from jax.experimental.pallas import tpu as pltpu

D_MODEL = 1024
D_POOL = 1024
D_MLSTM = 1024
POOL_WINDOWS = (2, 4, 8, 16)
POOL_GROUP_DIM = D_POOL // len(POOL_WINDOWS)
N_HEADS = 4
HEAD_DIM = 256
QKV_BLOCK = 4
CONV_WIDTH = 5
CHUNK = 128
N_DIRS = 2
EPS = 1e-6

X_GROUPS, Z_GROUPS = (0, 2), (1, 3)
HALO = 16
DIR_LANES = 128
GROUP_LANES = 16
I_LANE, CM_LANE, F_LANE = 0, 4, 8
SCAN_ROWS = 16
CONV_ROLL_TAPS = ()
CONV_MXU_TAPS = (0, 1, 3, 4)
IN_TILE = 512
PREP_TILE = 512
OUT_TILE = 256
VMEM_LIMIT = 48 * 1024 * 1024

F32 = jnp.float32
BF16 = jnp.bfloat16


def _silu(z):
    return z * (1.0 / (1.0 + jnp.exp(-z)))


def _log_sigmoid(g):
    return jnp.minimum(g, 0.0) - jnp.log1p(jnp.exp(-jnp.abs(g)))


def _inproj_kernel(x_ref, g_ref, w_ref, o_ref):
    x = x_ref[...]
    ms = jnp.mean(x * x, axis=-1, keepdims=True)
    u = (x * lax.rsqrt(ms + EPS) * g_ref[...]).astype(BF16)
    for n in Z_GROUPS + X_GROUPS:
        cols = slice(n * D_MODEL, (n + 1) * D_MODEL)
        acc = jnp.dot(u, w_ref[:, cols], preferred_element_type=F32)
        if n in Z_GROUPS:
            acc = _silu(acc)
        o_ref[:, cols] = acc.astype(BF16)


def _token_scan(x, op, reverse):
    t = lax.broadcasted_iota(jnp.int32, x.shape, 0)
    k = 1
    while k < CHUNK:
        if reverse:
            shifted = pltpu.roll(x, CHUNK - k, 0)
            valid = t < CHUNK - k
        else:
            shifted = pltpu.roll(x, k, 0)
            valid = t >= k
        x = jnp.where(valid, op(x, shifted), x)
        k *= 2
    return x


def _prep_kernel(mxp_ref, mx_ref, mxn_ref, shift_ref, convw_ref, convb_ref, wq_ref, wk_ref, wv_ref,
                 wg_ref, bg_ref,
                 q_ref, kt_ref, v_ref, c_ref, colsf_ref, colsb_ref, rowsf_ref, rowsb_ref):
    i = pl.program_id(1)
    n_tiles = pl.num_programs(1)
    tile = PREP_TILE
    n_chunks = tile // CHUNK

    mx_bf = mx_ref[0]
    zero_halo = jnp.zeros((HALO, D_MLSTM), BF16)
    ext = jnp.concatenate(
        [jnp.where(i == 0, zero_halo, mxp_ref[0]), mx_bf,
         jnp.where(i == n_tiles - 1, zero_halo, mxn_ref[0]),
         jnp.zeros((CHUNK - 2 * HALO, D_MLSTM), BF16)], axis=0)

    pad = CONV_WIDTH // 2
    rows = tile + 2 * HALO
    ext32 = ext[0:rows, :].astype(F32)
    conv = convb_ref[...] + ext32[HALO:HALO + tile, :] * convw_ref[pad:pad + 1, :]
    for tap in CONV_ROLL_TAPS:
        rolled = pltpu.roll(ext32, (rows - (tap - pad)) % rows, 0)
        conv = conv + rolled[HALO:HALO + tile, :] * convw_ref[tap:tap + 1, :]
    conv_parts = []
    for ch in range(n_chunks):
        shifted = jnp.dot(shift_ref[...], ext[ch * CHUNK:(ch + 2) * CHUNK, :],
                          preferred_element_type=F32)
        part = conv[ch * CHUNK:(ch + 1) * CHUNK, :]
        for n, tap in enumerate(CONV_MXU_TAPS):
            part = part + shifted[n * CHUNK:(n + 1) * CHUNK, :] * convw_ref[tap:tap + 1, :]
        conv_parts.append(part)
    c_bf = _silu(jnp.concatenate(conv_parts, axis=0)).astype(BF16)
    c_ref[0] = c_bf

    gates = bg_ref[...] + jnp.dot(c_bf, wg_ref[0], preferred_element_type=F32) \
        + jnp.dot(mx_bf, wg_ref[1], preferred_element_type=F32)

    sub = lax.broadcasted_iota(jnp.int32, (CHUNK, DIR_LANES), 1) % GROUP_LANES
    for d, (cols_ref, rows_ref) in enumerate(((colsf_ref, rowsf_ref), (colsb_ref, rowsb_ref))):
        ds = slice(d * DIR_LANES, (d + 1) * DIR_LANES)
        packed = gates[0:CHUNK, ds]
        for ch in range(1, n_chunks):
            packed = packed + pltpu.roll(gates[ch * CHUNK:(ch + 1) * CHUNK, ds], ch * GROUP_LANES, 1)
        b = _token_scan(_log_sigmoid(packed), jnp.add, reverse=(d == 1))
        a = packed - pltpu.roll(b, DIR_LANES - (F_LANE - I_LANE), 1)
        cm = _token_scan(a, jnp.maximum, reverse=(d == 1))
        scan = jnp.where(sub < CM_LANE, a,
                         jnp.where(sub < F_LANE, pltpu.roll(cm, CM_LANE - I_LANE, 1), b))
        for ch in range(n_chunks):
            cols = scan if ch == 0 else pltpu.roll(scan, DIR_LANES - ch * GROUP_LANES, 1)
            cols_ref[0, ch * CHUNK:(ch + 1) * CHUNK, :] = cols
            rows_ref[0, ch] = cols.T[0:SCAN_ROWS, :]

    for h in range(N_HEADS):
        hs = slice(h * HEAD_DIM, (h + 1) * HEAD_DIM)
        q_ref[0, :, hs] = jnp.dot(c_bf[:, hs], wq_ref[h], preferred_element_type=F32).astype(BF16)
        v_ref[0, :, hs] = jnp.dot(mx_bf[:, hs], wv_ref[h], preferred_element_type=F32).astype(BF16)
        kh = jnp.dot(c_bf[:, hs], wk_ref[h], preferred_element_type=F32)
        for ch in range(n_chunks):
            kt_ref[0, ch, hs, :] = kh[ch * CHUNK:(ch + 1) * CHUNK, :].T.astype(BF16)


def _lane_bcast(tile, lane):
    return jnp.broadcast_to(tile[:, lane:lane + 1], tile.shape)


def _sweep_state_phase(d, h, q_ref, kt_ref, v_ref, rows_ref, sc_ref, sc_base, c_sc, n_sc, m_sc):
    hs = slice(h * HEAD_DIM, (h + 1) * HEAD_DIM)
    st = d * N_HEADS + h
    m = m_sc[st]
    b_last = sc_ref[sc_base + h]
    cm_last = sc_ref[sc_base + N_HEADS + h]
    m_last = jnp.maximum(m, cm_last)
    m_sc[st] = b_last + m_last

    a_row = rows_ref[0, 0, I_LANE + h:I_LANE + h + 1, :]
    ws_row = jnp.exp(a_row - m_last)
    decay = jnp.exp(jnp.full((1, HEAD_DIM), m - m_last, F32))

    qh, vh = q_ref[0, :, hs], v_ref[0, :, hs]
    kt = kt_ref[0, 0, hs, :]
    c_old = c_sc[st]
    n_old = n_sc[st]
    q_c = jnp.dot(qh, c_old.astype(BF16), preferred_element_type=F32)
    qk = jnp.dot(qh, kt, preferred_element_type=F32)
    q_n = qh.astype(F32) * n_old[0:1, :]
    q_n = q_n[:, :CHUNK] + q_n[:, CHUNK:]

    ws_bf = ws_row.astype(BF16)
    kv = jnp.dot(kt * ws_bf, vh, preferred_element_type=F32)
    c_sc[st] = decay * c_old + kv
    ws8 = jnp.broadcast_to(ws_bf, (8, CHUNK))
    n_sc[st] = decay * n_old + lax.dot_general(ws8, kt, (((1,), (1,)), ((), ())),
                                               preferred_element_type=F32)
    return m, a_row, q_c, qk, q_n


def _sweep_output_phase(d, h, state, v_ref, cols, causal, h_ref):
    hs = slice(h * HEAD_DIM, (h + 1) * HEAD_DIM)
    m, a_row, q_c, qk, q_n = state
    big_m = jnp.maximum(_lane_bcast(cols, CM_LANE + h), m)
    dmat = jnp.where(causal, jnp.exp(a_row - big_m), 0.0)
    inter_w = jnp.exp(m - big_m)
    exp_neg_mt = jnp.exp(-(_lane_bcast(cols, F_LANE + h) + big_m))
    s = qk * dmat
    den = jnp.sum(s + inter_w * q_n, axis=-1, keepdims=True)
    inv = 1.0 / jnp.maximum(jnp.abs(den), exp_neg_mt[:, 0:1])
    inv_b = jnp.broadcast_to(inv, (CHUNK, CHUNK))
    num = jnp.dot(s.astype(BF16), v_ref[0, :, hs], preferred_element_type=F32)
    out = (num + jnp.concatenate([inter_w, inter_w], axis=1) * q_c) \
        * jnp.concatenate([inv_b, inv_b], axis=1)
    h_ref[0, :, hs] = out.astype(h_ref.dtype)


def _sweep_kernel(sc_ref,
                  qf_ref, ktf_ref, vf_ref, colsf_ref, rowsf_ref,
                  qb_ref, ktb_ref, vb_ref, colsb_ref, rowsb_ref,
                  hf_ref, hb_ref,
                  c_sc, n_sc, m_sc):
    b = pl.program_id(0)
    j = pl.program_id(1)
    nc = pl.num_programs(1)

    @pl.when(j == 0)
    def _():
        c_sc[...] = jnp.zeros_like(c_sc)
        n_sc[...] = jnp.zeros_like(n_sc)
        for st in range(N_DIRS * N_HEADS):
            m_sc[st] = jnp.float32(0.0)

    per_chunk = N_DIRS * 2 * N_HEADS
    base_f = (b * nc + j) * per_chunk
    base_b = (b * nc + (nc - 1 - j)) * per_chunk + 2 * N_HEADS
    dirs = ((qf_ref, ktf_ref, vf_ref, colsf_ref, rowsf_ref, hf_ref, base_f),
            (qb_ref, ktb_ref, vb_ref, colsb_ref, rowsb_ref, hb_ref, base_b))
    units = [(d, h) for h in range(N_HEADS) for d in range(N_DIRS)]
    states = {}
    for d, h in units:
        q_ref, kt_ref, v_ref, _, rows_ref, _, base = dirs[d]
        states[d, h] = _sweep_state_phase(d, h, q_ref, kt_ref, v_ref, rows_ref, sc_ref, base,
                                          c_sc, n_sc, m_sc)
    t_idx = lax.broadcasted_iota(jnp.int32, (CHUNK, CHUNK), 0)
    s_idx = lax.broadcasted_iota(jnp.int32, (CHUNK, CHUNK), 1)
    causal = (s_idx <= t_idx, s_idx >= t_idx)
    cols = (colsf_ref[0], colsb_ref[0])
    for d, h in units:
        _sweep_output_phase(d, h, states[d, h], dirs[d][2], cols[d], causal[d], dirs[d][5])


def _combine_kernel(hf_ref, hb_ref, c_ref, szm_ref, pxp_ref, px_ref, pxn_ref, szp_ref, x_ref,
                    pmat_ref, poolw_ref, mhw_ref, skipw_ref, wout_ref, gout_ref,
                    o_ref):
    i = pl.program_id(1)
    n_tiles = pl.num_programs(1)
    tile = OUT_TILE

    y_m_parts = []
    for h in range(N_HEADS):
        hs = slice(h * HEAD_DIM, (h + 1) * HEAD_DIM)
        ht = hf_ref[0, :, hs].astype(F32) + hb_ref[0, :, hs].astype(F32)
        mu = jnp.mean(ht, axis=-1, keepdims=True)
        dlt = ht - mu
        var = jnp.mean(dlt * dlt, axis=-1, keepdims=True)
        hn = (dlt * lax.rsqrt(var + EPS) * mhw_ref[:, hs]).astype(BF16)
        y_m_parts.append((hn + skipw_ref[:, hs] * c_ref[0, :, hs]) * szm_ref[0, :, hs])
    y_m = jnp.concatenate(y_m_parts, axis=1)

    zero_halo = jnp.zeros((HALO, D_POOL), BF16)
    px_main = px_ref[0]
    ext = jnp.concatenate(
        [jnp.where(i == 0, zero_halo, pxp_ref[0]), px_main,
         jnp.where(i == n_tiles - 1, zero_halo, pxn_ref[0]),
         jnp.zeros((CHUNK - 2 * HALO, D_POOL), BF16)], axis=0)
    t_glob = i * tile + lax.broadcasted_iota(jnp.int32, (tile, 1), 0)
    seq_last = n_tiles * tile - 1
    y_p_parts = []
    for g, w in enumerate(POOL_WINDOWS):
        gs = slice(g * POOL_GROUP_DIM, (g + 1) * POOL_GROUP_DIM)
        left = (w - 1) // 2
        right = w - 1 - left
        count = jnp.minimum(t_glob + right, seq_last) - jnp.maximum(t_glob - left, 0) + 1
        total = jnp.concatenate(
            [jnp.dot(pmat_ref[g], ext[ch * CHUNK:(ch + 2) * CHUNK, gs], preferred_element_type=F32)
             for ch in range(tile // CHUNK)], axis=0)
        pooled = total / count.astype(F32) - px_main[:, gs].astype(F32)
        mixed = jnp.dot(pooled.astype(BF16), poolw_ref[g], preferred_element_type=F32)
        y_p_parts.append(mixed.astype(BF16) * szp_ref[0, :, gs])
    y_p = jnp.concatenate(y_p_parts, axis=1)

    hres = x_ref[0] + jnp.dot(y_p, wout_ref[0:D_POOL, :], preferred_element_type=F32) \
        + jnp.dot(y_m, wout_ref[D_POOL:, :], preferred_element_type=F32)
    ms = jnp.mean(hres * hres, axis=-1, keepdims=True)
    o_ref[0] = hres * lax.rsqrt(ms + EPS) * gout_ref[...]


def _block_diag_tiles(w):
    rows = w.reshape(N_HEADS, HEAD_DIM, QKV_BLOCK)
    col = np.arange(HEAD_DIM)
    spread = jnp.asarray((col[None, :] % QKV_BLOCK == np.arange(QKV_BLOCK)[:, None]), w.dtype)
    tiled = jnp.einsum('tro,oc->trc', rows, spread, precision=lax.Precision.HIGHEST)
    same_block = jnp.asarray(col[:, None] // QKV_BLOCK == col[None, :] // QKV_BLOCK)
    return jnp.where(same_block[None], tiled, 0.0)


def _gate_weights(w_gates, b_gates, w_q, w_k, w_v):
    def to_lanes(g):
        rows = g.shape[0]
        gap = jnp.zeros((rows, N_DIRS, F_LANE - I_LANE - N_HEADS), F32)
        tail = jnp.zeros((rows, N_DIRS, DIR_LANES - F_LANE - N_HEADS), F32)
        lanes = jnp.concatenate([g[..., :N_HEADS], gap, g[..., N_HEADS:], tail], axis=-1)
        return lanes.reshape(rows, N_DIRS * DIR_LANES)

    lanes = N_DIRS * DIR_LANES
    n_blocks = D_MLSTM // QKV_BLOCK
    wg = to_lanes(jnp.transpose(w_gates, (1, 0, 2))).reshape(3, n_blocks, QKV_BLOCK, lanes)

    def fold(w_head, rows):
        return jnp.einsum('nio,nol->nil', w_head, rows,
                          precision=lax.Precision.HIGHEST).reshape(D_MLSTM, lanes)

    w_c = fold(w_q, wg[0]) + fold(w_k, wg[1])
    return jnp.stack([w_c, fold(w_v, wg[2])]), to_lanes(b_gates[None])


def _pool_band_matrices():
    t = np.arange(CHUNK)[:, None]
    r = np.arange(2 * CHUNK)[None, :] - HALO
    mats = []
    for w in POOL_WINDOWS:
        left = (w - 1) // 2
        right = w - 1 - left
        mats.append(((r >= t - left) & (r <= t + right)).astype(np.float32))
    return jnp.asarray(np.stack(mats), dtype=BF16)


def _conv_shift_matrix():
    t = np.arange(CHUNK)[:, None]
    r = np.arange(2 * CHUNK)[None, :] - HALO
    blocks = [r == t + (tap - CONV_WIDTH // 2) for tap in CONV_MXU_TAPS]
    return jnp.asarray(np.concatenate(blocks, axis=0).astype(np.float32), dtype=BF16)


def _full(shape):
    return pl.BlockSpec(shape, lambda b, j: (0,) * len(shape))


def _halo_specs(tile, n_halo, col):
    per = tile // HALO
    prev = pl.BlockSpec((1, HALO, D_MODEL), lambda b, i: (b, jnp.maximum(i * per - 1, 0), col))
    nxt = pl.BlockSpec((1, HALO, D_MODEL),
                       lambda b, i: (b, jnp.minimum((i + 1) * per, n_halo - 1), col))
    return prev, nxt


def kernel(x, norm_in_g, w_in, pool_w, pool_scale, conv_w, conv_b, w_q, w_k, w_v, w_gates,
           b_gates, mh_norm_w, skip_w, w_out, norm_out_g):
    B, S, D = x.shape
    assert D == D_MODEL and (B * S) % IN_TILE == 0
    assert S % PREP_TILE == 0 and S % OUT_TILE == 0 and PREP_TILE % CHUNK == 0
    assert (PREP_TILE // CHUNK) * GROUP_LANES <= DIR_LANES and CONV_WIDTH // 2 <= HALO
    assert norm_in_g.shape[0] == 1, "single-layer block"
    nc = S // CHUNK
    tokens = B * S
    n_halo = S // HALO
    arb2 = pltpu.CompilerParams(dimension_semantics=("arbitrary", "arbitrary"),
                                vmem_limit_bytes=VMEM_LIMIT)

    proj = pl.pallas_call(
        _inproj_kernel,
        grid=(tokens // IN_TILE,),
        in_specs=[pl.BlockSpec((IN_TILE, D), lambda i: (i, 0)),
                  pl.BlockSpec((1, D), lambda i: (0, 0)),
                  pl.BlockSpec((D, 4 * D), lambda i: (0, 0))],
        out_specs=pl.BlockSpec((IN_TILE, 4 * D), lambda i: (i, 0)),
        out_shape=jax.ShapeDtypeStruct((tokens, 4 * D), BF16),
        compiler_params=pltpu.CompilerParams(dimension_semantics=("arbitrary",),
                                             vmem_limit_bytes=VMEM_LIMIT),
        name="inproj",
    )(x.reshape(tokens, D), norm_in_g[0][None, :], w_in[0].astype(BF16))
    proj = proj.reshape(B, S, 4 * D)

    wq_t = _block_diag_tiles(w_q[0]).astype(BF16)
    wk_t = (_block_diag_tiles(w_k[0]) * (float(HEAD_DIM) ** -0.5)).astype(BF16)
    wv_t = _block_diag_tiles(w_v[0]).astype(BF16)
    wg, bg = _gate_weights(w_gates[0], b_gates[0], w_q[0], w_k[0], w_v[0])
    conv_w8 = jnp.zeros((8, D_MLSTM), F32).at[:CONV_WIDTH].set(conv_w[0])

    mx_prev, mx_next = _halo_specs(PREP_TILE, n_halo, 2)
    prep_seq = pl.BlockSpec((1, PREP_TILE, D), lambda b, i: (b, i, 0))
    prep_cols = pl.BlockSpec((1, PREP_TILE, DIR_LANES), lambda b, i: (b, i, 0))
    cpt = PREP_TILE // CHUNK
    prep_rows = pl.BlockSpec((1, cpt, SCAN_ROWS, CHUNK), lambda b, i: (b, i, 0, 0))
    seq_bf = jax.ShapeDtypeStruct((B, S, D), BF16)
    cols_shape = jax.ShapeDtypeStruct((B, S, DIR_LANES), F32)
    rows_shape = jax.ShapeDtypeStruct((B, nc, SCAN_ROWS, CHUNK), F32)
    prep_kt = pl.BlockSpec((1, cpt, D, CHUNK), lambda b, i: (b, i, 0, 0))
    kt_shape = jax.ShapeDtypeStruct((B, nc, D, CHUNK), BF16)
    q, kt, v, c, cols_f, cols_b, rows_f, rows_b = pl.pallas_call(
        _prep_kernel,
        grid=(B, S // PREP_TILE),
        in_specs=[mx_prev, pl.BlockSpec((1, PREP_TILE, D), lambda b, i: (b, i, 2)), mx_next,
                  _full((len(CONV_MXU_TAPS) * CHUNK, 2 * CHUNK)),
                  _full((8, D)), _full((1, D)),
                  _full((N_HEADS, HEAD_DIM, HEAD_DIM)), _full((N_HEADS, HEAD_DIM, HEAD_DIM)),
                  _full((N_HEADS, HEAD_DIM, HEAD_DIM)),
                  _full((2, D, N_DIRS * DIR_LANES)), _full((1, N_DIRS * DIR_LANES))],
        out_specs=[prep_seq, prep_kt, prep_seq, prep_seq,
                   prep_cols, prep_cols, prep_rows, prep_rows],
        out_shape=[seq_bf, kt_shape, seq_bf, seq_bf,
                   cols_shape, cols_shape, rows_shape, rows_shape],
        compiler_params=arb2,
        name="prep",
    )(proj, proj, proj, _conv_shift_matrix(), conv_w8, conv_b[0][None, :], wq_t, wk_t, wv_t,
      wg.astype(BF16), bg)

    def last(rows, lane):
        return jnp.concatenate([rows[:, :, F_LANE:F_LANE + N_HEADS, lane],
                                rows[:, :, CM_LANE:CM_LANE + N_HEADS, lane]], axis=-1)

    chunk_scalars = jnp.concatenate([last(rows_f, CHUNK - 1), last(rows_b, 0)], axis=-1).reshape(-1)

    def fwd_idx(b, j):
        return j

    def bwd_idx(b, j):
        return nc - 1 - j

    def sweep_specs(idx):
        seq = pl.BlockSpec((1, CHUNK, D), lambda b, j: (b, idx(b, j), 0))
        cols = pl.BlockSpec((1, CHUNK, DIR_LANES), lambda b, j: (b, idx(b, j), 0))
        rows = pl.BlockSpec((1, 1, 8, CHUNK), lambda b, j: (b, idx(b, j), 0, 0))
        kt_spec = pl.BlockSpec((1, 1, D, CHUNK), lambda b, j: (b, idx(b, j), 0, 0))
        return seq, [seq, kt_spec, seq, cols, rows]

    seq_f, in_f = sweep_specs(fwd_idx)
    seq_b, in_b = sweep_specs(bwd_idx)
    n_state = N_DIRS * N_HEADS
    h_fwd, h_bwd = pl.pallas_call(
        _sweep_kernel,
        grid=(B, nc),
        in_specs=[pl.BlockSpec(memory_space=pltpu.SMEM)] + in_f + in_b,
        out_specs=[seq_f, seq_b],
        out_shape=[seq_bf, seq_bf],
        scratch_shapes=[pltpu.VMEM((n_state, HEAD_DIM, HEAD_DIM), F32),
                        pltpu.VMEM((n_state, 8, HEAD_DIM), F32),
                        pltpu.SMEM((n_state,), F32)],
        compiler_params=arb2,
        name="sweep",
    )(chunk_scalars, q, kt, v, cols_f, rows_f, q, kt, v, cols_b, rows_b)

    px_prev, px_next = _halo_specs(OUT_TILE, n_halo, 0)
    pool_w_scaled = pool_w[0] * pool_scale[0].reshape(len(POOL_WINDOWS), 1, POOL_GROUP_DIM)

    def out_col(col):
        return pl.BlockSpec((1, OUT_TILE, D), lambda b, i: (b, i, col))

    out_seq = out_col(0)
    out = pl.pallas_call(
        _combine_kernel,
        grid=(B, S // OUT_TILE),
        in_specs=[out_seq, out_seq, out_seq, out_col(3),
                  px_prev, out_col(0), px_next, out_col(1), out_seq,
                  _full((len(POOL_WINDOWS), CHUNK, 2 * CHUNK)),
                  _full((len(POOL_WINDOWS), POOL_GROUP_DIM, POOL_GROUP_DIM)),
                  _full((1, D)), _full((1, D)),
                  _full((2 * D, D)), _full((1, D))],
        out_specs=out_seq,
        out_shape=jax.ShapeDtypeStruct((B, S, D), F32),
        compiler_params=arb2,
        name="combine",
    )(h_fwd, h_bwd, c, proj, proj, proj, proj, proj, x,
      _pool_band_matrices(), pool_w_scaled.astype(BF16),
      mh_norm_w[0][None, :], skip_w[0][None, :].astype(BF16), w_out[0].astype(BF16),
      norm_out_g[None, :])
    return out
```

```python
import numpy as np
import jax
import jax.numpy as jnp
from jax import lax
from jax.experimental import pallas as pl
from jax.experimental.pallas import tpu as pltpu

D_MODEL = 1024
D_POOL = 1024
D_MLSTM = 1024
POOL_WINDOWS = (2, 4, 8, 16)
POOL_GROUP_DIM = D_POOL // len(POOL_WINDOWS)
N_HEADS = 4
HEAD_DIM = 256
QKV_BLOCK = 4
CONV_WIDTH = 5
CHUNK = 128
N_DIRS = 2
EPS = 1e-6

X_GROUPS, Z_GROUPS = (0, 2), (1, 3)
HALO = 16
DIR_LANES = 128
GROUP_LANES = 16
I_LANE, CM_LANE, F_LANE = 0, 4, 8
SCAN_ROWS = 16
CONV_ROLL_TAPS = ()
CONV_MXU_TAPS = (0, 1, 3, 4)
IN_TILE = 1024
PREP_TILE = 512
OUT_TILE = 512
SWEEP_CHUNKS = 1
VMEM_LIMIT = 48 * 1024 * 1024

F32 = jnp.float32
BF16 = jnp.bfloat16


def _silu(z):
    return z * (1.0 / (1.0 + jnp.exp(-z)))


def _log_sigmoid(g):
    return jnp.minimum(g, 0.0) - jnp.log1p(jnp.exp(-jnp.abs(g)))


def _inproj_kernel(x_ref, g_ref, w_ref, o_ref):
    x = x_ref[...]
    ms = jnp.mean(x * x, axis=-1, keepdims=True)
    u = (x * lax.rsqrt(ms + EPS) * g_ref[...]).astype(BF16)
    for n in Z_GROUPS + X_GROUPS:
        cols = slice(n * D_MODEL, (n + 1) * D_MODEL)
        acc = jnp.dot(u, w_ref[:, cols], preferred_element_type=F32)
        if n in Z_GROUPS:
            acc = _silu(acc)
        o_ref[:, cols] = acc.astype(BF16)


def _token_scan(x, op, reverse):
    t = lax.broadcasted_iota(jnp.int32, x.shape, 0)
    k = 1
    while k < CHUNK:
        if reverse:
            shifted = pltpu.roll(x, CHUNK - k, 0)
            valid = t < CHUNK - k
        else:
            shifted = pltpu.roll(x, k, 0)
            valid = t >= k
        x = jnp.where(valid, op(x, shifted), x)
        k *= 2
    return x


def _prep_kernel(mxp_ref, mx_ref, mxn_ref, shift_ref, convw_ref, convb_ref, wq_ref, wk_ref, wv_ref,
                 wg_ref, bg_ref,
                 q_ref, kt_ref, v_ref, c_ref, colsf_ref, colsb_ref, rowsf_ref, rowsb_ref):
    i = pl.program_id(1)
    n_tiles = pl.num_programs(1)
    tile = PREP_TILE
    n_chunks = tile // CHUNK

    mx_bf = mx_ref[0]
    zero_halo = jnp.zeros((HALO, D_MLSTM), BF16)
    ext = jnp.concatenate(
        [jnp.where(i == 0, zero_halo, mxp_ref[0]), mx_bf,
         jnp.where(i == n_tiles - 1, zero_halo, mxn_ref[0]),
         jnp.zeros((CHUNK - 2 * HALO, D_MLSTM), BF16)], axis=0)

    pad = CONV_WIDTH // 2
    rows = tile + 2 * HALO
    ext32 = ext[0:rows, :].astype(F32)
    conv = convb_ref[...] + ext32[HALO:HALO + tile, :] * convw_ref[pad:pad + 1, :]
    for tap in CONV_ROLL_TAPS:
        rolled = pltpu.roll(ext32, (rows - (tap - pad)) % rows, 0)
        conv = conv + rolled[HALO:HALO + tile, :] * convw_ref[tap:tap + 1, :]
    conv_parts = []
    for ch in range(n_chunks):
        shifted = jnp.dot(shift_ref[...], ext[ch * CHUNK:(ch + 2) * CHUNK, :],
                          preferred_element_type=F32)
        part = conv[ch * CHUNK:(ch + 1) * CHUNK, :]
        for n, tap in enumerate(CONV_MXU_TAPS):
            part = part + shifted[n * CHUNK:(n + 1) * CHUNK, :] * convw_ref[tap:tap + 1, :]
        conv_parts.append(part)
    c_bf = _silu(jnp.concatenate(conv_parts, axis=0)).astype(BF16)
    c_ref[0] = c_bf

    gates = bg_ref[...] + jnp.dot(c_bf, wg_ref[0], preferred_element_type=F32) \
        + jnp.dot(mx_bf, wg_ref[1], preferred_element_type=F32)

    sub = lax.broadcasted_iota(jnp.int32, (CHUNK, DIR_LANES), 1) % GROUP_LANES
    for d, (cols_ref, rows_ref) in enumerate(((colsf_ref, rowsf_ref), (colsb_ref, rowsb_ref))):
        ds = slice(d * DIR_LANES, (d + 1) * DIR_LANES)
        packed = gates[0:CHUNK, ds]
        for ch in range(1, n_chunks):
            packed = packed + pltpu.roll(gates[ch * CHUNK:(ch + 1) * CHUNK, ds], ch * GROUP_LANES, 1)
        b = _token_scan(_log_sigmoid(packed), jnp.add, reverse=(d == 1))
        a = packed - pltpu.roll(b, DIR_LANES - (F_LANE - I_LANE), 1)
        cm = _token_scan(a, jnp.maximum, reverse=(d == 1))
        scan = jnp.where(sub < CM_LANE, a,
                         jnp.where(sub < F_LANE, pltpu.roll(cm, CM_LANE - I_LANE, 1), b))
        for ch in range(n_chunks):
            cols = scan if ch == 0 else pltpu.roll(scan, DIR_LANES - ch * GROUP_LANES, 1)
            cols_ref[0, ch * CHUNK:(ch + 1) * CHUNK, :] = cols
            rows_ref[0, ch] = cols.T[0:SCAN_ROWS, :]

    for h in range(N_HEADS):
        hs = slice(h * HEAD_DIM, (h + 1) * HEAD_DIM)
        q_ref[0, :, hs] = jnp.dot(c_bf[:, hs], wq_ref[h], preferred_element_type=F32).astype(BF16)
        v_ref[0, :, hs] = jnp.dot(mx_bf[:, hs], wv_ref[h], preferred_element_type=F32).astype(BF16)
        kh = jnp.dot(c_bf[:, hs], wk_ref[h], preferred_element_type=F32)
        for ch in range(n_chunks):
            kt_ref[0, ch, hs, :] = kh[ch * CHUNK:(ch + 1) * CHUNK, :].T.astype(BF16)


def _lane_bcast(tile, lane):
    return jnp.broadcast_to(tile[:, lane:lane + 1], tile.shape)


def _sweep_state_phase(h, off, carry, q_ref, kt_ref, v_ref, rows_ref, sc_ref, sc_base):
    hs = slice(h * HEAD_DIM, (h + 1) * HEAD_DIM)
    rs = slice(off * CHUNK, (off + 1) * CHUNK)
    c_old, n_old, m = carry
    b_last = sc_ref[sc_base + h]
    cm_last = sc_ref[sc_base + N_HEADS + h]
    m_last = jnp.maximum(m, cm_last)

    a_row = rows_ref[0, off, I_LANE + h:I_LANE + h + 1, :]
    ws_row = jnp.exp(a_row - m_last)
    decay = jnp.exp(jnp.full((1, HEAD_DIM), m - m_last, F32))

    qh, vh = q_ref[0, rs, hs], v_ref[0, rs, hs]
    kt = kt_ref[0, off, hs, :]
    q_c = jnp.dot(qh, c_old.astype(BF16), preferred_element_type=F32)
    qk = jnp.dot(qh, kt, preferred_element_type=F32)
    q_n = qh.astype(F32) * n_old[0:1, :]
    q_n = q_n[:, :CHUNK] + q_n[:, CHUNK:]

    ws_bf = ws_row.astype(BF16)
    kv = jnp.dot(kt * ws_bf, vh, preferred_element_type=F32)
    ws8 = jnp.broadcast_to(ws_bf, (8, CHUNK))
    kn = lax.dot_general(ws8, kt, (((1,), (1,)), ((), ())), preferred_element_type=F32)
    new_carry = (decay * c_old + kv, decay * n_old + kn, b_last + m_last)
    return new_carry, (m, a_row, q_c, qk, q_n)


def _sweep_output_phase(h, off, state, v_ref, cols_ref, causal, h_ref):
    hs = slice(h * HEAD_DIM, (h + 1) * HEAD_DIM)
    rs = slice(off * CHUNK, (off + 1) * CHUNK)
    cols = cols_ref[0, rs, :]
    m, a_row, q_c, qk, q_n = state
    big_m = jnp.maximum(_lane_bcast(cols, CM_LANE + h), m)
    dmat = jnp.where(causal, jnp.exp(a_row - big_m), 0.0)
    inter_w = jnp.exp(m - big_m)
    exp_neg_mt = jnp.exp(-(_lane_bcast(cols, F_LANE + h) + big_m))
    s = qk * dmat
    den = jnp.sum(s + inter_w * q_n, axis=-1, keepdims=True)
    inv = 1.0 / jnp.maximum(jnp.abs(den), exp_neg_mt[:, 0:1])
    inv_b = jnp.broadcast_to(inv, (CHUNK, CHUNK))
    num = jnp.dot(s.astype(BF16), v_ref[0, rs, hs], preferred_element_type=F32)
    out = (num + jnp.concatenate([inter_w, inter_w], axis=1) * q_c) \
        * jnp.concatenate([inv_b, inv_b], axis=1)
    h_ref[0, rs, hs] = out.astype(h_ref.dtype)


def _sweep_kernel(sc_ref,
                  qf_ref, ktf_ref, vf_ref, colsf_ref, rowsf_ref,
                  qb_ref, ktb_ref, vb_ref, colsb_ref, rowsb_ref,
                  hf_ref, hb_ref,
                  c_sc, n_sc, m_sc):
    b = pl.program_id(0)
    j = pl.program_id(1)
    nc = pl.num_programs(1) * SWEEP_CHUNKS

    @pl.when(j == 0)
    def _():
        c_sc[...] = jnp.zeros_like(c_sc)
        n_sc[...] = jnp.zeros_like(n_sc)
        for st in range(N_DIRS * N_HEADS):
            m_sc[st] = jnp.float32(0.0)

    per_chunk = N_DIRS * 2 * N_HEADS
    dirs = ((qf_ref, ktf_ref, vf_ref, colsf_ref, rowsf_ref, hf_ref),
            (qb_ref, ktb_ref, vb_ref, colsb_ref, rowsb_ref, hb_ref))
    units = [(d, h) for h in range(N_HEADS) for d in range(N_DIRS)]
    t_idx = lax.broadcasted_iota(jnp.int32, (CHUNK, CHUNK), 0)
    s_idx = lax.broadcasted_iota(jnp.int32, (CHUNK, CHUNK), 1)
    causal = (s_idx <= t_idx, s_idx >= t_idx)

    carry = {(d, h): (c_sc[d * N_HEADS + h], n_sc[d * N_HEADS + h], m_sc[d * N_HEADS + h])
             for d, h in units}
    for sub in range(SWEEP_CHUNKS):
        off = (sub, SWEEP_CHUNKS - 1 - sub)
        chunk_f = j * SWEEP_CHUNKS + sub
        base = ((b * nc + chunk_f) * per_chunk,
                (b * nc + (nc - 1 - chunk_f)) * per_chunk + 2 * N_HEADS)
        states = {}
        for d, h in units:
            q_ref, kt_ref, v_ref, _, rows_ref, _ = dirs[d]
            carry[d, h], states[d, h] = _sweep_state_phase(
                h, off[d], carry[d, h], q_ref, kt_ref, v_ref, rows_ref, sc_ref, base[d])
        for d, h in units:
            _sweep_output_phase(h, off[d], states[d, h], dirs[d][2], dirs[d][3], causal[d],
                                dirs[d][5])
    for d, h in units:
        st = d * N_HEADS + h
        c_sc[st], n_sc[st], m_sc[st] = carry[d, h]


def _combine_kernel(hf_ref, hb_ref, c_ref, szm_ref, pxp_ref, px_ref, pxn_ref, szp_ref, x_ref,
                    pmat_ref, poolw_ref, mhw_ref, skipw_ref, wout_ref, gout_ref,
                    o_ref):
    i = pl.program_id(1)
    n_tiles = pl.num_programs(1)
    tile = OUT_TILE

    y_m_parts = []
    for h in range(N_HEADS):
        hs = slice(h * HEAD_DIM, (h + 1) * HEAD_DIM)
        ht = hf_ref[0, :, hs].astype(F32) + hb_ref[0, :, hs].astype(F32)
        mu = jnp.mean(ht, axis=-1, keepdims=True)
        dlt = ht - mu
        var = jnp.mean(dlt * dlt, axis=-1, keepdims=True)
        hn = (dlt * lax.rsqrt(var + EPS) * mhw_ref[:, hs]).astype(BF16)
        y_m_parts.append((hn + skipw_ref[:, hs] * c_ref[0, :, hs]) * szm_ref[0, :, hs])
    y_m = jnp.concatenate(y_m_parts, axis=1)

    zero_halo = jnp.zeros((HALO, D_POOL), BF16)
    px_main = px_ref[0]
    ext = jnp.concatenate(
        [jnp.where(i == 0, zero_halo, pxp_ref[0]), px_main,
         jnp.where(i == n_tiles - 1, zero_halo, pxn_ref[0]),
         jnp.zeros((CHUNK - 2 * HALO, D_POOL), BF16)], axis=0)
    t_glob = i * tile + lax.broadcasted_iota(jnp.int32, (tile, 1), 0)
    seq_last = n_tiles * tile - 1
    y_p_parts = []
    for g, w in enumerate(POOL_WINDOWS):
        gs = slice(g * POOL_GROUP_DIM, (g + 1) * POOL_GROUP_DIM)
        left = (w - 1) // 2
        right = w - 1 - left
        count = jnp.minimum(t_glob + right, seq_last) - jnp.maximum(t_glob - left, 0) + 1
        total = jnp.concatenate(
            [jnp.dot(pmat_ref[g], ext[ch * CHUNK:(ch + 2) * CHUNK, gs], preferred_element_type=F32)
             for ch in range(tile // CHUNK)], axis=0)
        pooled = total / count.astype(F32) - px_main[:, gs].astype(F32)
        mixed = jnp.dot(pooled.astype(BF16), poolw_ref[g], preferred_element_type=F32)
        y_p_parts.append(mixed.astype(BF16) * szp_ref[0, :, gs])
    y_p = jnp.concatenate(y_p_parts, axis=1)

    hres = x_ref[0] + jnp.dot(y_p, wout_ref[0:D_POOL, :], preferred_element_type=F32) \
        + jnp.dot(y_m, wout_ref[D_POOL:, :], preferred_element_type=F32)
    ms = jnp.mean(hres * hres, axis=-1, keepdims=True)
    o_ref[0] = hres * lax.rsqrt(ms + EPS) * gout_ref[...]


def _block_diag_tiles(w):
    rows = w.reshape(N_HEADS, HEAD_DIM, QKV_BLOCK)
    col = np.arange(HEAD_DIM)
    spread = jnp.asarray((col[None, :] % QKV_BLOCK == np.arange(QKV_BLOCK)[:, None]), w.dtype)
    tiled = jnp.einsum('tro,oc->trc', rows, spread, precision=lax.Precision.HIGHEST)
    same_block = jnp.asarray(col[:, None] // QKV_BLOCK == col[None, :] // QKV_BLOCK)
    return jnp.where(same_block[None], tiled, 0.0)


def _gate_weights(w_gates, b_gates, w_q, w_k, w_v):
    def to_lanes(g):
        rows = g.shape[0]
        gap = jnp.zeros((rows, N_DIRS, F_LANE - I_LANE - N_HEADS), F32)
        tail = jnp.zeros((rows, N_DIRS, DIR_LANES - F_LANE - N_HEADS), F32)
        lanes = jnp.concatenate([g[..., :N_HEADS], gap, g[..., N_HEADS:], tail], axis=-1)
        return lanes.reshape(rows, N_DIRS * DIR_LANES)

    lanes = N_DIRS * DIR_LANES
    n_blocks = D_MLSTM // QKV_BLOCK
    wg = to_lanes(jnp.transpose(w_gates, (1, 0, 2))).reshape(3, n_blocks, QKV_BLOCK, lanes)

    def fold(w_head, rows):
        return jnp.einsum('nio,nol->nil', w_head, rows,
                          precision=lax.Precision.HIGHEST).reshape(D_MLSTM, lanes)

    w_c = fold(w_q, wg[0]) + fold(w_k, wg[1])
    return jnp.stack([w_c, fold(w_v, wg[2])]), to_lanes(b_gates[None])


def _pool_band_matrices():
    t = np.arange(CHUNK)[:, None]
    r = np.arange(2 * CHUNK)[None, :] - HALO
    mats = []
    for w in POOL_WINDOWS:
        left = (w - 1) // 2
        right = w - 1 - left
        mats.append(((r >= t - left) & (r <= t + right)).astype(np.float32))
    return jnp.asarray(np.stack(mats), dtype=BF16)


def _conv_shift_matrix():
    t = np.arange(CHUNK)[:, None]
    r = np.arange(2 * CHUNK)[None, :] - HALO
    blocks = [r == t + (tap - CONV_WIDTH // 2) for tap in CONV_MXU_TAPS]
    return jnp.asarray(np.concatenate(blocks, axis=0).astype(np.float32), dtype=BF16)


def _full(shape):
    return pl.BlockSpec(shape, lambda b, j: (0,) * len(shape))


def _halo_specs(tile, n_halo, col):
    per = tile // HALO
    prev = pl.BlockSpec((1, HALO, D_MODEL), lambda b, i: (b, jnp.maximum(i * per - 1, 0), col))
    nxt = pl.BlockSpec((1, HALO, D_MODEL),
                       lambda b, i: (b, jnp.minimum((i + 1) * per, n_halo - 1), col))
    return prev, nxt


def kernel(x, norm_in_g, w_in, pool_w, pool_scale, conv_w, conv_b, w_q, w_k, w_v, w_gates,
           b_gates, mh_norm_w, skip_w, w_out, norm_out_g):
    B, S, D = x.shape
    assert D == D_MODEL and (B * S) % IN_TILE == 0
    assert S % PREP_TILE == 0 and S % OUT_TILE == 0 and PREP_TILE % CHUNK == 0
    assert (PREP_TILE // CHUNK) * GROUP_LANES <= DIR_LANES and CONV_WIDTH // 2 <= HALO
    assert norm_in_g.shape[0] == 1, "single-layer block"
    nc = S // CHUNK
    tokens = B * S
    n_halo = S // HALO
    arb2 = pltpu.CompilerParams(dimension_semantics=("arbitrary", "arbitrary"),
                                vmem_limit_bytes=VMEM_LIMIT)

    proj = pl.pallas_call(
        _inproj_kernel,
        grid=(tokens // IN_TILE,),
        in_specs=[pl.BlockSpec((IN_TILE, D), lambda i: (i, 0)),
                  pl.BlockSpec((1, D), lambda i: (0, 0)),
                  pl.BlockSpec((D, 4 * D), lambda i: (0, 0), pipeline_mode=pl.Buffered(1))],
        out_specs=pl.BlockSpec((IN_TILE, 4 * D), lambda i: (i, 0)),
        out_shape=jax.ShapeDtypeStruct((tokens, 4 * D), BF16),
        compiler_params=pltpu.CompilerParams(dimension_semantics=("arbitrary",),
                                             vmem_limit_bytes=VMEM_LIMIT),
        name="inproj",
    )(x.reshape(tokens, D), norm_in_g[0][None, :], w_in[0].astype(BF16))
    proj = proj.reshape(B, S, 4 * D)

    wq_t = _block_diag_tiles(w_q[0]).astype(BF16)
    wk_t = (_block_diag_tiles(w_k[0]) * (float(HEAD_DIM) ** -0.5)).astype(BF16)
    wv_t = _block_diag_tiles(w_v[0]).astype(BF16)
    wg, bg = _gate_weights(w_gates[0], b_gates[0], w_q[0], w_k[0], w_v[0])
    conv_w8 = jnp.zeros((8, D_MLSTM), F32).at[:CONV_WIDTH].set(conv_w[0])

    mx_prev, mx_next = _halo_specs(PREP_TILE, n_halo, 2)
    prep_seq = pl.BlockSpec((1, PREP_TILE, D), lambda b, i: (b, i, 0))
    prep_cols = pl.BlockSpec((1, PREP_TILE, DIR_LANES), lambda b, i: (b, i, 0))
    cpt = PREP_TILE // CHUNK
    prep_rows = pl.BlockSpec((1, cpt, SCAN_ROWS, CHUNK), lambda b, i: (b, i, 0, 0))
    seq_bf = jax.ShapeDtypeStruct((B, S, D), BF16)
    cols_shape = jax.ShapeDtypeStruct((B, S, DIR_LANES), F32)
    rows_shape = jax.ShapeDtypeStruct((B, nc, SCAN_ROWS, CHUNK), F32)
    prep_kt = pl.BlockSpec((1, cpt, D, CHUNK), lambda b, i: (b, i, 0, 0))
    kt_shape = jax.ShapeDtypeStruct((B, nc, D, CHUNK), BF16)
    q, kt, v, c, cols_f, cols_b, rows_f, rows_b = pl.pallas_call(
        _prep_kernel,
        grid=(B, S // PREP_TILE),
        in_specs=[mx_prev, pl.BlockSpec((1, PREP_TILE, D), lambda b, i: (b, i, 2)), mx_next,
                  _full((len(CONV_MXU_TAPS) * CHUNK, 2 * CHUNK)),
                  _full((8, D)), _full((1, D)),
                  _full((N_HEADS, HEAD_DIM, HEAD_DIM)), _full((N_HEADS, HEAD_DIM, HEAD_DIM)),
                  _full((N_HEADS, HEAD_DIM, HEAD_DIM)),
                  _full((2, D, N_DIRS * DIR_LANES)), _full((1, N_DIRS * DIR_LANES))],
        out_specs=[prep_seq, prep_kt, prep_seq, prep_seq,
                   prep_cols, prep_cols, prep_rows, prep_rows],
        out_shape=[seq_bf, kt_shape, seq_bf, seq_bf,
                   cols_shape, cols_shape, rows_shape, rows_shape],
        compiler_params=arb2,
        name="prep",
    )(proj, proj, proj, _conv_shift_matrix(), conv_w8, conv_b[0][None, :], wq_t, wk_t, wv_t,
      wg.astype(BF16), bg)

    def last(rows, lane):
        return jnp.concatenate([rows[:, :, F_LANE:F_LANE + N_HEADS, lane],
                                rows[:, :, CM_LANE:CM_LANE + N_HEADS, lane]], axis=-1)

    chunk_scalars = jnp.concatenate([last(rows_f, CHUNK - 1), last(rows_b, 0)], axis=-1).reshape(-1)

    sweep_steps = nc // SWEEP_CHUNKS
    sweep_rows = SWEEP_CHUNKS * CHUNK

    def fwd_idx(b, j):
        return j

    def bwd_idx(b, j):
        return sweep_steps - 1 - j

    def sweep_specs(idx):
        seq = pl.BlockSpec((1, sweep_rows, D), lambda b, j: (b, idx(b, j), 0))
        cols = pl.BlockSpec((1, sweep_rows, DIR_LANES), lambda b, j: (b, idx(b, j), 0))
        rows = pl.BlockSpec((1, SWEEP_CHUNKS, 8, CHUNK), lambda b, j: (b, idx(b, j), 0, 0))
        kt_spec = pl.BlockSpec((1, SWEEP_CHUNKS, D, CHUNK), lambda b, j: (b, idx(b, j), 0, 0))
        return seq, [seq, kt_spec, seq, cols, rows]

    seq_f, in_f = sweep_specs(fwd_idx)
    seq_b, in_b = sweep_specs(bwd_idx)
    n_state = N_DIRS * N_HEADS
    h_fwd, h_bwd = pl.pallas_call(
        _sweep_kernel,
        grid=(B, sweep_steps),
        in_specs=[pl.BlockSpec(memory_space=pltpu.SMEM)] + in_f + in_b,
        out_specs=[seq_f, seq_b],
        out_shape=[seq_bf, seq_bf],
        scratch_shapes=[pltpu.VMEM((n_state, HEAD_DIM, HEAD_DIM), F32),
                        pltpu.VMEM((n_state, 8, HEAD_DIM), F32),
                        pltpu.SMEM((n_state,), F32)],
        compiler_params=arb2,
        name="sweep",
    )(chunk_scalars, q, kt, v, cols_f, rows_f, q, kt, v, cols_b, rows_b)

    px_prev, px_next = _halo_specs(OUT_TILE, n_halo, 0)
    pool_w_scaled = pool_w[0] * pool_scale[0].reshape(len(POOL_WINDOWS), 1, POOL_GROUP_DIM)

    def out_col(col):
        return pl.BlockSpec((1, OUT_TILE, D), lambda b, i: (b, i, col))

    out_seq = out_col(0)
    out = pl.pallas_call(
        _combine_kernel,
        grid=(B, S // OUT_TILE),
        in_specs=[out_seq, out_seq, out_seq, out_col(3),
                  px_prev, out_col(0), px_next, out_col(1), out_seq,
                  _full((len(POOL_WINDOWS), CHUNK, 2 * CHUNK)),
                  _full((len(POOL_WINDOWS), POOL_GROUP_DIM, POOL_GROUP_DIM)),
                  _full((1, D)), _full((1, D)),
                  _full((2 * D, D)), _full((1, D))],
        out_specs=out_seq,
        out_shape=jax.ShapeDtypeStruct((B, S, D), F32),
        compiler_params=arb2,
        name="combine",
    )(h_fwd, h_bwd, c, proj, proj, proj, proj, proj, x,
      _pool_band_matrices(), pool_w_scaled.astype(BF16),
      mh_norm_w[0][None, :], skip_w[0][None, :].astype(BF16), w_out[0].astype(BF16),
      norm_out_g[None, :])
    return out
```

```python
import numpy as np
import jax
import jax.numpy as jnp
from jax import lax
from jax.experimental import pallas as pl
from jax.experimental.pallas import tpu as pltpu

D_MODEL = 1024
D_POOL = 1024
D_MLSTM = 1024
POOL_WINDOWS = (2, 4, 8, 16)
POOL_GROUP_DIM = D_POOL // len(POOL_WINDOWS)
N_HEADS = 4
HEAD_DIM = 256
QKV_BLOCK = 4
CONV_WIDTH = 5
CHUNK = 128
N_DIRS = 2
EPS = 1e-6

X_GROUPS, Z_GROUPS = (0, 2), (1, 3)
HALO = 16
DIR_LANES = 128
GROUP_LANES = 16
I_LANE, CM_LANE, F_LANE = 0, 4, 8
SCAN_ROWS = 16
CONV_SIDE_TAPS = (0, 1, 3, 4)
IN_TILE = 1024
PREP_TILE = 512
OUT_TILE = 512
SWEEP_CHUNKS = 1
VMEM_LIMIT = 48 * 1024 * 1024

F32 = jnp.float32
BF16 = jnp.bfloat16


def _silu(z):
    return z * (1.0 / (1.0 + jnp.exp(-z)))


def _log_sigmoid(g):
    return jnp.minimum(g, 0.0) - jnp.log1p(jnp.exp(-jnp.abs(g)))


def _inproj_kernel(x_ref, g_ref, w_ref, o_ref):
    x = x_ref[...]
    ms = jnp.mean(x * x, axis=-1, keepdims=True)
    u = (x * lax.rsqrt(ms + EPS) * g_ref[...]).astype(BF16)
    for n in Z_GROUPS + X_GROUPS:
        cols = slice(n * D_MODEL, (n + 1) * D_MODEL)
        acc = jnp.dot(u, w_ref[:, cols], preferred_element_type=F32)
        if n in Z_GROUPS:
            acc = _silu(acc)
        o_ref[:, cols] = acc.astype(BF16)


def _token_scan(x, op, reverse):
    t = lax.broadcasted_iota(jnp.int32, x.shape, 0)
    k = 1
    while k < CHUNK:
        if reverse:
            shifted = pltpu.roll(x, CHUNK - k, 0)
            valid = t < CHUNK - k
        else:
            shifted = pltpu.roll(x, k, 0)
            valid = t >= k
        x = jnp.where(valid, op(x, shifted), x)
        k *= 2
    return x


def _prep_kernel(mxp_ref, mx_ref, mxn_ref, shift_ref, convw_ref, convb_ref, wq_ref, wk_ref, wv_ref,
                 wg_ref, bg_ref,
                 q_ref, kt_ref, v_ref, c_ref, colsf_ref, colsb_ref, rowsf_ref, rowsb_ref):
    i = pl.program_id(1)
    n_tiles = pl.num_programs(1)
    tile = PREP_TILE
    n_chunks = tile // CHUNK

    mx_bf = mx_ref[0]
    zero_halo = jnp.zeros((HALO, D_MLSTM), BF16)
    ext = jnp.concatenate(
        [jnp.where(i == 0, zero_halo, mxp_ref[0]), mx_bf,
         jnp.where(i == n_tiles - 1, zero_halo, mxn_ref[0]),
         jnp.zeros((CHUNK - 2 * HALO, D_MLSTM), BF16)], axis=0)

    pad = CONV_WIDTH // 2
    conv_parts = []
    for ch in range(n_chunks):
        rs = slice(ch * CHUNK, (ch + 1) * CHUNK)
        shifted = jnp.dot(shift_ref[...], ext[ch * CHUNK:(ch + 2) * CHUNK, :],
                          preferred_element_type=F32)
        part = convb_ref[...] + mx_bf[rs, :].astype(F32) * convw_ref[pad:pad + 1, :]
        for n, tap in enumerate(CONV_SIDE_TAPS):
            part = part + shifted[n * CHUNK:(n + 1) * CHUNK, :] * convw_ref[tap:tap + 1, :]
        conv_parts.append(part)
    c_bf = _silu(jnp.concatenate(conv_parts, axis=0)).astype(BF16)
    c_ref[0] = c_bf

    gates = bg_ref[...] + jnp.dot(c_bf, wg_ref[0], preferred_element_type=F32) \
        + jnp.dot(mx_bf, wg_ref[1], preferred_element_type=F32)

    sub = lax.broadcasted_iota(jnp.int32, (CHUNK, DIR_LANES), 1) % GROUP_LANES
    for d, (cols_ref, rows_ref) in enumerate(((colsf_ref, rowsf_ref), (colsb_ref, rowsb_ref))):
        ds = slice(d * DIR_LANES, (d + 1) * DIR_LANES)
        packed = gates[0:CHUNK, ds]
        for ch in range(1, n_chunks):
            packed = packed + pltpu.roll(gates[ch * CHUNK:(ch + 1) * CHUNK, ds], ch * GROUP_LANES, 1)
        b = _token_scan(_log_sigmoid(packed), jnp.add, reverse=(d == 1))
        a = packed - pltpu.roll(b, DIR_LANES - (F_LANE - I_LANE), 1)
        cm = _token_scan(a, jnp.maximum, reverse=(d == 1))
        scan = jnp.where(sub < CM_LANE, a,
                         jnp.where(sub < F_LANE, pltpu.roll(cm, CM_LANE - I_LANE, 1), b))
        for ch in range(n_chunks):
            cols = scan if ch == 0 else pltpu.roll(scan, DIR_LANES - ch * GROUP_LANES, 1)
            cols_ref[0, ch * CHUNK:(ch + 1) * CHUNK, :] = cols
            rows_ref[0, ch] = cols.T[0:SCAN_ROWS, :]

    for h in range(N_HEADS):
        hs = slice(h * HEAD_DIM, (h + 1) * HEAD_DIM)
        q_ref[0, :, hs] = jnp.dot(c_bf[:, hs], wq_ref[h], preferred_element_type=F32).astype(BF16)
        v_ref[0, :, hs] = jnp.dot(mx_bf[:, hs], wv_ref[h], preferred_element_type=F32).astype(BF16)
        kh = jnp.dot(c_bf[:, hs], wk_ref[h], preferred_element_type=F32)
        for ch in range(n_chunks):
            kt_ref[0, ch, hs, :] = kh[ch * CHUNK:(ch + 1) * CHUNK, :].T.astype(BF16)


def _lane_bcast(tile, lane):
    return jnp.broadcast_to(tile[:, lane:lane + 1], tile.shape)


def _sweep_state_phase(h, off, carry, q_ref, kt_ref, v_ref, rows_ref, sc_ref, sc_base):
    hs = slice(h * HEAD_DIM, (h + 1) * HEAD_DIM)
    rs = slice(off * CHUNK, (off + 1) * CHUNK)
    c_old, n_old, m = carry
    b_last = sc_ref[sc_base + h]
    cm_last = sc_ref[sc_base + N_HEADS + h]
    m_last = jnp.maximum(m, cm_last)

    a_row = rows_ref[0, off, I_LANE + h:I_LANE + h + 1, :]
    ws_row = jnp.exp(a_row - m_last)
    decay = jnp.exp(jnp.full((1, HEAD_DIM), m - m_last, F32))

    qh, vh = q_ref[0, rs, hs], v_ref[0, rs, hs]
    kt = kt_ref[0, off, hs, :]
    q_c = jnp.dot(qh, c_old.astype(BF16), preferred_element_type=F32)
    qk = jnp.dot(qh, kt, preferred_element_type=F32)
    q_n = qh.astype(F32) * n_old[0:1, :]
    q_n = q_n[:, :CHUNK] + q_n[:, CHUNK:]

    ws_bf = ws_row.astype(BF16)
    kv = jnp.dot(kt * ws_bf, vh, preferred_element_type=F32)
    ws8 = jnp.broadcast_to(ws_bf, (8, CHUNK))
    kn = lax.dot_general(ws8, kt, (((1,), (1,)), ((), ())), preferred_element_type=F32)
    new_carry = (decay * c_old + kv, decay * n_old + kn, b_last + m_last)
    return new_carry, (m, a_row, q_c, qk, q_n)


def _sweep_output_phase(h, off, state, v_ref, cols_ref, causal, h_ref):
    hs = slice(h * HEAD_DIM, (h + 1) * HEAD_DIM)
    rs = slice(off * CHUNK, (off + 1) * CHUNK)
    cols = cols_ref[0, rs, :]
    m, a_row, q_c, qk, q_n = state
    big_m = jnp.maximum(_lane_bcast(cols, CM_LANE + h), m)
    dmat = jnp.where(causal, jnp.exp(a_row - big_m), 0.0)
    inter_w = jnp.exp(m - big_m)
    exp_neg_mt = jnp.exp(-(_lane_bcast(cols, F_LANE + h) + big_m))
    s = qk * dmat
    den = jnp.sum(s + inter_w * q_n, axis=-1, keepdims=True)
    inv = 1.0 / jnp.maximum(jnp.abs(den), exp_neg_mt[:, 0:1])
    inv_b = jnp.broadcast_to(inv, (CHUNK, CHUNK))
    num = jnp.dot(s.astype(BF16), v_ref[0, rs, hs], preferred_element_type=F32)
    out = (num + jnp.concatenate([inter_w, inter_w], axis=1) * q_c) \
        * jnp.concatenate([inv_b, inv_b], axis=1)
    h_ref[0, rs, hs] = out.astype(h_ref.dtype)


def _sweep_kernel(sc_ref,
                  qf_ref, ktf_ref, vf_ref, colsf_ref, rowsf_ref,
                  qb_ref, ktb_ref, vb_ref, colsb_ref, rowsb_ref,
                  hf_ref, hb_ref,
                  c_sc, n_sc, m_sc):
    b = pl.program_id(0)
    j = pl.program_id(1)
    nc = pl.num_programs(1) * SWEEP_CHUNKS

    @pl.when(j == 0)
    def _():
        c_sc[...] = jnp.zeros_like(c_sc)
        n_sc[...] = jnp.zeros_like(n_sc)
        for st in range(N_DIRS * N_HEADS):
            m_sc[st] = jnp.float32(0.0)

    per_chunk = N_DIRS * 2 * N_HEADS
    dirs = ((qf_ref, ktf_ref, vf_ref, colsf_ref, rowsf_ref, hf_ref),
            (qb_ref, ktb_ref, vb_ref, colsb_ref, rowsb_ref, hb_ref))
    units = [(d, h) for h in range(N_HEADS) for d in range(N_DIRS)]
    t_idx = lax.broadcasted_iota(jnp.int32, (CHUNK, CHUNK), 0)
    s_idx = lax.broadcasted_iota(jnp.int32, (CHUNK, CHUNK), 1)
    causal = (s_idx <= t_idx, s_idx >= t_idx)

    carry = {(d, h): (c_sc[d * N_HEADS + h], n_sc[d * N_HEADS + h], m_sc[d * N_HEADS + h])
             for d, h in units}
    for sub in range(SWEEP_CHUNKS):
        off = (sub, SWEEP_CHUNKS - 1 - sub)
        chunk_f = j * SWEEP_CHUNKS + sub
        base = ((b * nc + chunk_f) * per_chunk,
                (b * nc + (nc - 1 - chunk_f)) * per_chunk + 2 * N_HEADS)
        states = {}
        for d, h in units:
            q_ref, kt_ref, v_ref, _, rows_ref, _ = dirs[d]
            carry[d, h], states[d, h] = _sweep_state_phase(
                h, off[d], carry[d, h], q_ref, kt_ref, v_ref, rows_ref, sc_ref, base[d])
        for d, h in units:
            _sweep_output_phase(h, off[d], states[d, h], dirs[d][2], dirs[d][3], causal[d],
                                dirs[d][5])
    for d, h in units:
        st = d * N_HEADS + h
        c_sc[st], n_sc[st], m_sc[st] = carry[d, h]


def _combine_kernel(hf_ref, hb_ref, c_ref, szm_ref, pxp_ref, px_ref, pxn_ref, szp_ref, x_ref,
                    pmat_ref, poolw_ref, mhw_ref, skipw_ref, wout_ref, gout_ref,
                    o_ref):
    i = pl.program_id(1)
    n_tiles = pl.num_programs(1)
    tile = OUT_TILE

    y_m_parts = []
    for h in range(N_HEADS):
        hs = slice(h * HEAD_DIM, (h + 1) * HEAD_DIM)
        ht = hf_ref[0, :, hs].astype(F32) + hb_ref[0, :, hs].astype(F32)
        mu = jnp.mean(ht, axis=-1, keepdims=True)
        dlt = ht - mu
        var = jnp.mean(dlt * dlt, axis=-1, keepdims=True)
        hn = (dlt * lax.rsqrt(var + EPS) * mhw_ref[:, hs]).astype(BF16)
        y_m_parts.append((hn + skipw_ref[:, hs] * c_ref[0, :, hs]) * szm_ref[0, :, hs])
    y_m = jnp.concatenate(y_m_parts, axis=1)

    zero_halo = jnp.zeros((HALO, D_POOL), BF16)
    px_main = px_ref[0]
    ext = jnp.concatenate(
        [jnp.where(i == 0, zero_halo, pxp_ref[0]), px_main,
         jnp.where(i == n_tiles - 1, zero_halo, pxn_ref[0]),
         jnp.zeros((CHUNK - 2 * HALO, D_POOL), BF16)], axis=0)
    t_glob = i * tile + lax.broadcasted_iota(jnp.int32, (tile, 1), 0)
    seq_last = n_tiles * tile - 1
    y_p_parts = []
    for g, w in enumerate(POOL_WINDOWS):
        gs = slice(g * POOL_GROUP_DIM, (g + 1) * POOL_GROUP_DIM)
        left = (w - 1) // 2
        right = w - 1 - left
        count = jnp.minimum(t_glob + right, seq_last) - jnp.maximum(t_glob - left, 0) + 1
        total = jnp.concatenate(
            [jnp.dot(pmat_ref[g], ext[ch * CHUNK:(ch + 2) * CHUNK, gs], preferred_element_type=F32)
             for ch in range(tile // CHUNK)], axis=0)
        pooled = total / count.astype(F32) - px_main[:, gs].astype(F32)
        mixed = jnp.dot(pooled.astype(BF16), poolw_ref[g], preferred_element_type=F32)
        y_p_parts.append(mixed.astype(BF16) * szp_ref[0, :, gs])
    y_p = jnp.concatenate(y_p_parts, axis=1)

    hres = x_ref[0] + jnp.dot(y_p, wout_ref[0:D_POOL, :], preferred_element_type=F32) \
        + jnp.dot(y_m, wout_ref[D_POOL:, :], preferred_element_type=F32)
    ms = jnp.mean(hres * hres, axis=-1, keepdims=True)
    o_ref[0] = hres * lax.rsqrt(ms + EPS) * gout_ref[...]


def _block_diag_tiles(w):
    rows = w.reshape(N_HEADS, HEAD_DIM, QKV_BLOCK)
    col = np.arange(HEAD_DIM)
    spread = jnp.asarray((col[None, :] % QKV_BLOCK == np.arange(QKV_BLOCK)[:, None]), w.dtype)
    tiled = jnp.einsum('tro,oc->trc', rows, spread, precision=lax.Precision.HIGHEST)
    same_block = jnp.asarray(col[:, None] // QKV_BLOCK == col[None, :] // QKV_BLOCK)
    return jnp.where(same_block[None], tiled, 0.0)


def _gate_weights(w_gates, b_gates, bq, bk, bv):
    n_gates = N_DIRS * 2 * N_HEADS
    place = np.zeros((n_gates, N_DIRS * DIR_LANES), np.float32)
    for d in range(N_DIRS):
        for g in range(2 * N_HEADS):
            lane = (I_LANE + g) if g < N_HEADS else (F_LANE + g - N_HEADS)
            place[d * 2 * N_HEADS + g, d * DIR_LANES + lane] = 1.0
    place = jnp.asarray(place)
    hi = lax.Precision.HIGHEST
    rows = jnp.transpose(w_gates, (1, 0, 2)).reshape(3, N_HEADS, HEAD_DIM, n_gates)
    w_c = jnp.einsum('trc,tcg->trg', bq, rows[0], precision=hi) \
        + jnp.einsum('trc,tcg->trg', bk, rows[1], precision=hi)
    w_v = jnp.einsum('trc,tcg->trg', bv, rows[2], precision=hi)
    folded = jnp.stack([w_c, w_v]).reshape(2, D_MLSTM, n_gates)
    return (jnp.einsum('krg,gl->krl', folded, place, precision=hi),
            jnp.dot(b_gates.reshape(1, n_gates), place, precision=hi))


def _pool_band_matrices():
    t = np.arange(CHUNK)[:, None]
    r = np.arange(2 * CHUNK)[None, :] - HALO
    mats = []
    for w in POOL_WINDOWS:
        left = (w - 1) // 2
        right = w - 1 - left
        mats.append(((r >= t - left) & (r <= t + right)).astype(np.float32))
    return jnp.asarray(np.stack(mats), dtype=BF16)


def _conv_shift_matrix():
    t = np.arange(CHUNK)[:, None]
    r = np.arange(2 * CHUNK)[None, :] - HALO
    blocks = [r == t + (tap - CONV_WIDTH // 2) for tap in CONV_SIDE_TAPS]
    return jnp.asarray(np.concatenate(blocks, axis=0).astype(np.float32), dtype=BF16)


def _full(shape):
    return pl.BlockSpec(shape, lambda b, j: (0,) * len(shape))


def _halo_specs(tile, n_halo, col):
    per = tile // HALO
    prev = pl.BlockSpec((1, HALO, D_MODEL), lambda b, i: (b, jnp.maximum(i * per - 1, 0), col))
    nxt = pl.BlockSpec((1, HALO, D_MODEL),
                       lambda b, i: (b, jnp.minimum((i + 1) * per, n_halo - 1), col))
    return prev, nxt


def kernel(x, norm_in_g, w_in, pool_w, pool_scale, conv_w, conv_b, w_q, w_k, w_v, w_gates,
           b_gates, mh_norm_w, skip_w, w_out, norm_out_g):
    B, S, D = x.shape
    assert D == D_MODEL and (B * S) % IN_TILE == 0
    assert S % PREP_TILE == 0 and S % OUT_TILE == 0 and PREP_TILE % CHUNK == 0
    assert (PREP_TILE // CHUNK) * GROUP_LANES <= DIR_LANES and CONV_WIDTH // 2 <= HALO
    assert norm_in_g.shape[0] == 1, "single-layer block"
    nc = S // CHUNK
    tokens = B * S
    n_halo = S // HALO
    arb2 = pltpu.CompilerParams(dimension_semantics=("arbitrary", "arbitrary"),
                                vmem_limit_bytes=VMEM_LIMIT)

    proj = pl.pallas_call(
        _inproj_kernel,
        grid=(tokens // IN_TILE,),
        in_specs=[pl.BlockSpec((IN_TILE, D), lambda i: (i, 0)),
                  pl.BlockSpec((1, D), lambda i: (0, 0)),
                  pl.BlockSpec((D, 4 * D), lambda i: (0, 0), pipeline_mode=pl.Buffered(1))],
        out_specs=pl.BlockSpec((IN_TILE, 4 * D), lambda i: (i, 0)),
        out_shape=jax.ShapeDtypeStruct((tokens, 4 * D), BF16),
        compiler_params=pltpu.CompilerParams(dimension_semantics=("arbitrary",),
                                             vmem_limit_bytes=VMEM_LIMIT),
        name="inproj",
    )(x.reshape(tokens, D), norm_in_g[0][None, :], w_in[0].astype(BF16))
    proj = proj.reshape(B, S, 4 * D)

    bq, bk, bv = (_block_diag_tiles(w[0]) for w in (w_q, w_k, w_v))
    wg, bg = _gate_weights(w_gates[0], b_gates[0], bq, bk, bv)
    wq_t, wk_t, wv_t = bq.astype(BF16), (bk * float(HEAD_DIM) ** -0.5).astype(BF16), bv.astype(BF16)
    conv_w8 = jnp.pad(conv_w[0], ((0, 8 - CONV_WIDTH), (0, 0)))

    mx_prev, mx_next = _halo_specs(PREP_TILE, n_halo, 2)
    prep_seq = pl.BlockSpec((1, PREP_TILE, D), lambda b, i: (b, i, 0))
    prep_cols = pl.BlockSpec((1, PREP_TILE, DIR_LANES), lambda b, i: (b, i, 0))
    cpt = PREP_TILE // CHUNK
    prep_rows = pl.BlockSpec((1, cpt, SCAN_ROWS, CHUNK), lambda b, i: (b, i, 0, 0))
    seq_bf = jax.ShapeDtypeStruct((B, S, D), BF16)
    cols_shape = jax.ShapeDtypeStruct((B, S, DIR_LANES), F32)
    rows_shape = jax.ShapeDtypeStruct((B, nc, SCAN_ROWS, CHUNK), F32)
    prep_kt = pl.BlockSpec((1, cpt, D, CHUNK), lambda b, i: (b, i, 0, 0))
    kt_shape = jax.ShapeDtypeStruct((B, nc, D, CHUNK), BF16)
    q, kt, v, c, cols_f, cols_b, rows_f, rows_b = pl.pallas_call(
        _prep_kernel,
        grid=(B, S // PREP_TILE),
        in_specs=[mx_prev, pl.BlockSpec((1, PREP_TILE, D), lambda b, i: (b, i, 2)), mx_next,
                  _full((len(CONV_SIDE_TAPS) * CHUNK, 2 * CHUNK)),
                  _full((8, D)), _full((1, D)),
                  _full((N_HEADS, HEAD_DIM, HEAD_DIM)), _full((N_HEADS, HEAD_DIM, HEAD_DIM)),
                  _full((N_HEADS, HEAD_DIM, HEAD_DIM)),
                  _full((2, D, N_DIRS * DIR_LANES)), _full((1, N_DIRS * DIR_LANES))],
        out_specs=[prep_seq, prep_kt, prep_seq, prep_seq,
                   prep_cols, prep_cols, prep_rows, prep_rows],
        out_shape=[seq_bf, kt_shape, seq_bf, seq_bf,
                   cols_shape, cols_shape, rows_shape, rows_shape],
        compiler_params=arb2,
        name="prep",
    )(proj, proj, proj, _conv_shift_matrix(), conv_w8, conv_b[0][None, :], wq_t, wk_t, wv_t,
      wg.astype(BF16), bg)

    def last(rows, lane):
        return jnp.concatenate([rows[:, :, F_LANE:F_LANE + N_HEADS, lane],
                                rows[:, :, CM_LANE:CM_LANE + N_HEADS, lane]], axis=-1)

    chunk_scalars = jnp.concatenate([last(rows_f, CHUNK - 1), last(rows_b, 0)], axis=-1).reshape(-1)

    sweep_steps = nc // SWEEP_CHUNKS
    sweep_rows = SWEEP_CHUNKS * CHUNK

    def fwd_idx(b, j):
        return j

    def bwd_idx(b, j):
        return sweep_steps - 1 - j

    def sweep_specs(idx):
        seq = pl.BlockSpec((1, sweep_rows, D), lambda b, j: (b, idx(b, j), 0))
        cols = pl.BlockSpec((1, sweep_rows, DIR_LANES), lambda b, j: (b, idx(b, j), 0))
        rows = pl.BlockSpec((1, SWEEP_CHUNKS, 8, CHUNK), lambda b, j: (b, idx(b, j), 0, 0))
        kt_spec = pl.BlockSpec((1, SWEEP_CHUNKS, D, CHUNK), lambda b, j: (b, idx(b, j), 0, 0))
        return seq, [seq, kt_spec, seq, cols, rows]

    seq_f, in_f = sweep_specs(fwd_idx)
    seq_b, in_b = sweep_specs(bwd_idx)
    n_state = N_DIRS * N_HEADS
    h_fwd, h_bwd = pl.pallas_call(
        _sweep_kernel,
        grid=(B, sweep_steps),
        in_specs=[pl.BlockSpec(memory_space=pltpu.SMEM)] + in_f + in_b,
        out_specs=[seq_f, seq_b],
        out_shape=[seq_bf, seq_bf],
        scratch_shapes=[pltpu.VMEM((n_state, HEAD_DIM, HEAD_DIM), F32),
                        pltpu.VMEM((n_state, 8, HEAD_DIM), F32),
                        pltpu.SMEM((n_state,), F32)],
        compiler_params=arb2,
        name="sweep",
    )(chunk_scalars, q, kt, v, cols_f, rows_f, q, kt, v, cols_b, rows_b)

    px_prev, px_next = _halo_specs(OUT_TILE, n_halo, 0)
    pool_w_scaled = pool_w[0] * pool_scale[0].reshape(len(POOL_WINDOWS), 1, POOL_GROUP_DIM)

    def out_col(col):
        return pl.BlockSpec((1, OUT_TILE, D), lambda b, i: (b, i, col))

    out_seq = out_col(0)
    out = pl.pallas_call(
        _combine_kernel,
        grid=(B, S // OUT_TILE),
        in_specs=[out_seq, out_seq, out_seq, out_col(3),
                  px_prev, out_col(0), px_next, out_col(1), out_seq,
                  _full((len(POOL_WINDOWS), CHUNK, 2 * CHUNK)),
                  _full((len(POOL_WINDOWS), POOL_GROUP_DIM, POOL_GROUP_DIM)),
                  _full((1, D)), _full((1, D)),
                  _full((2 * D, D)), _full((1, D))],
        out_specs=out_seq,
        out_shape=jax.ShapeDtypeStruct((B, S, D), F32),
        compiler_params=arb2,
        name="combine",
    )(h_fwd, h_bwd, c, proj, proj, proj, proj, proj, x,
      _pool_band_matrices(), pool_w_scaled.astype(BF16),
      mh_norm_w[0][None, :], skip_w[0][None, :].astype(BF16), w_out[0].astype(BF16),
      norm_out_g[None, :])
    return out
```

```python
import numpy as np
import jax
import jax.numpy as jnp
from jax import lax
from jax.experimental import pallas as pl
from jax.experimental.pallas import tpu as pltpu

D_MODEL = 1024
D_POOL = 1024
D_MLSTM = 1024
POOL_WINDOWS = (2, 4, 8, 16)
POOL_GROUP_DIM = D_POOL // len(POOL_WINDOWS)
N_HEADS = 4
HEAD_DIM = 256
QKV_BLOCK = 4
CONV_WIDTH = 5
CHUNK = 128
N_DIRS = 2
EPS = 1e-6

X_GROUPS, Z_GROUPS = (0, 2), (1, 3)
HALO = 16
DIR_LANES = 128
GROUP_LANES = 16
I_LANE, CM_LANE, F_LANE = 0, 4, 8
SCAN_ROWS = 16
CONV_SIDE_TAPS = (0, 1, 3, 4)
IN_TILE = 1024
PREP_TILE = 512
OUT_TILE = 512
SWEEP_GROUP = 4
VMEM_LIMIT = 48 * 1024 * 1024

F32 = jnp.float32
BF16 = jnp.bfloat16


def _silu(z):
    return z * (1.0 / (1.0 + jnp.exp(-z)))


def _log_sigmoid(g):
    return jnp.minimum(g, 0.0) - jnp.log1p(jnp.exp(-jnp.abs(g)))


def _inproj_kernel(x_ref, g_ref, w_ref, o_ref):
    x = x_ref[...]
    ms = jnp.mean(x * x, axis=-1, keepdims=True)
    u = (x * lax.rsqrt(ms + EPS) * g_ref[...]).astype(BF16)
    for n in Z_GROUPS + X_GROUPS:
        cols = slice(n * D_MODEL, (n + 1) * D_MODEL)
        acc = jnp.dot(u, w_ref[:, cols], preferred_element_type=F32)
        if n in Z_GROUPS:
            acc = _silu(acc)
        o_ref[:, cols] = acc.astype(BF16)


def _token_scan(x, op, reverse):
    t = lax.broadcasted_iota(jnp.int32, x.shape, 0)
    k = 1
    while k < CHUNK:
        if reverse:
            shifted = pltpu.roll(x, CHUNK - k, 0)
            valid = t < CHUNK - k
        else:
            shifted = pltpu.roll(x, k, 0)
            valid = t >= k
        x = jnp.where(valid, op(x, shifted), x)
        k *= 2
    return x


def _prep_kernel(mxp_ref, mx_ref, mxn_ref, shift_ref, convw_ref, convb_ref, wq_ref, wk_ref, wv_ref,
                 wg_ref, bg_ref,
                 q_ref, kt_ref, v_ref, c_ref, colsf_ref, colsb_ref, rowsf_ref, rowsb_ref):
    i = pl.program_id(1)
    n_tiles = pl.num_programs(1)
    tile = PREP_TILE
    n_chunks = tile // CHUNK

    mx_bf = mx_ref[0]
    zero_halo = jnp.zeros((HALO, D_MLSTM), BF16)
    ext = jnp.concatenate(
        [jnp.where(i == 0, zero_halo, mxp_ref[0]), mx_bf,
         jnp.where(i == n_tiles - 1, zero_halo, mxn_ref[0]),
         jnp.zeros((CHUNK - 2 * HALO, D_MLSTM), BF16)], axis=0)

    pad = CONV_WIDTH // 2
    conv_parts = []
    for ch in range(n_chunks):
        rs = slice(ch * CHUNK, (ch + 1) * CHUNK)
        shifted = jnp.dot(shift_ref[...], ext[ch * CHUNK:(ch + 2) * CHUNK, :],
                          preferred_element_type=F32)
        part = convb_ref[...] + mx_bf[rs, :].astype(F32) * convw_ref[pad:pad + 1, :]
        for n, tap in enumerate(CONV_SIDE_TAPS):
            part = part + shifted[n * CHUNK:(n + 1) * CHUNK, :] * convw_ref[tap:tap + 1, :]
        conv_parts.append(part)
    c_bf = _silu(jnp.concatenate(conv_parts, axis=0)).astype(BF16)
    c_ref[0] = c_bf

    gates = bg_ref[...] + jnp.dot(c_bf, wg_ref[0], preferred_element_type=F32) \
        + jnp.dot(mx_bf, wg_ref[1], preferred_element_type=F32)

    sub = lax.broadcasted_iota(jnp.int32, (CHUNK, DIR_LANES), 1) % GROUP_LANES
    for d, (cols_ref, rows_ref) in enumerate(((colsf_ref, rowsf_ref), (colsb_ref, rowsb_ref))):
        ds = slice(d * DIR_LANES, (d + 1) * DIR_LANES)
        packed = gates[0:CHUNK, ds]
        for ch in range(1, n_chunks):
            packed = packed + pltpu.roll(gates[ch * CHUNK:(ch + 1) * CHUNK, ds], ch * GROUP_LANES, 1)
        b = _token_scan(_log_sigmoid(packed), jnp.add, reverse=(d == 1))
        a = packed - pltpu.roll(b, DIR_LANES - (F_LANE - I_LANE), 1)
        cm = _token_scan(a, jnp.maximum, reverse=(d == 1))
        scan = jnp.where(sub < CM_LANE, a,
                         jnp.where(sub < F_LANE, pltpu.roll(cm, CM_LANE - I_LANE, 1), b))
        for ch in range(n_chunks):
            cols = scan if ch == 0 else pltpu.roll(scan, DIR_LANES - ch * GROUP_LANES, 1)
            cols_ref[0, ch * CHUNK:(ch + 1) * CHUNK, :] = cols
            rows_ref[0, ch] = cols.T[0:SCAN_ROWS, :]

    for h in range(N_HEADS):
        hs = slice(h * HEAD_DIM, (h + 1) * HEAD_DIM)
        q_ref[0, :, hs] = jnp.dot(c_bf[:, hs], wq_ref[h], preferred_element_type=F32).astype(BF16)
        v_ref[0, :, hs] = jnp.dot(mx_bf[:, hs], wv_ref[h], preferred_element_type=F32).astype(BF16)
        kh = jnp.dot(c_bf[:, hs], wk_ref[h], preferred_element_type=F32)
        for ch in range(n_chunks):
            kt_ref[0, ch, hs, :] = kh[ch * CHUNK:(ch + 1) * CHUNK, :].T.astype(BF16)


def _lane_bcast(tile, lane):
    return jnp.broadcast_to(tile[:, lane:lane + 1], tile.shape)


def _sweep_state_phase(h, carry, q_ref, kt_ref, v_ref, rows_ref, sc_ref, sc_base):
    hs = slice(h * HEAD_DIM, (h + 1) * HEAD_DIM)
    c_old, n_old, m = carry
    b_last = sc_ref[sc_base + h]
    cm_last = sc_ref[sc_base + N_HEADS + h]
    m_last = jnp.maximum(m, cm_last)

    a_row = rows_ref[0, 0, I_LANE + h:I_LANE + h + 1, :]
    ws_row = jnp.exp(a_row - m_last)
    decay = jnp.exp(jnp.full((1, HEAD_DIM), m - m_last, F32))

    qh, vh = q_ref[0, :, hs], v_ref[0, :, hs]
    kt = kt_ref[0, 0, hs, :]
    q_c = jnp.dot(qh, c_old.astype(BF16), preferred_element_type=F32)
    qk = jnp.dot(qh, kt, preferred_element_type=F32)
    q_n = qh.astype(F32) * n_old[0:1, :]
    q_n = q_n[:, :CHUNK] + q_n[:, CHUNK:]

    ws_bf = ws_row.astype(BF16)
    kv = jnp.dot(kt * ws_bf, vh, preferred_element_type=F32)
    ws8 = jnp.broadcast_to(ws_bf, (8, CHUNK))
    kn = lax.dot_general(ws8, kt, (((1,), (1,)), ((), ())), preferred_element_type=F32)
    new_carry = (decay * c_old + kv, decay * n_old + kn, b_last + m_last)
    return new_carry, (m, a_row, q_c, qk, q_n)


def _sweep_output_phase(h, state, v_ref, cols_ref, causal, h_ref):
    hs = slice(h * HEAD_DIM, (h + 1) * HEAD_DIM)
    cols = cols_ref[0]
    m, a_row, q_c, qk, q_n = state
    big_m = jnp.maximum(_lane_bcast(cols, CM_LANE + h), m)
    dmat = jnp.where(causal, jnp.exp(a_row - big_m), 0.0)
    inter_w = jnp.exp(m - big_m)
    exp_neg_mt = jnp.exp(-(_lane_bcast(cols, F_LANE + h) + big_m))
    s = qk * dmat
    den = jnp.sum(s + inter_w * q_n, axis=-1, keepdims=True)
    inv = 1.0 / jnp.maximum(jnp.abs(den), exp_neg_mt[:, 0:1])
    inv_b = jnp.broadcast_to(inv, (CHUNK, CHUNK))
    num = jnp.dot(s.astype(BF16), v_ref[0, :, hs], preferred_element_type=F32)
    out = (num + jnp.concatenate([inter_w, inter_w], axis=1) * q_c) \
        * jnp.concatenate([inv_b, inv_b], axis=1)
    h_ref[0, :, hs] = out.astype(h_ref.dtype)


def _sweep_kernel(sc_ref,
                  qf_ref, ktf_ref, vf_ref, colsf_ref, rowsf_ref,
                  qb_ref, ktb_ref, vb_ref, colsb_ref, rowsb_ref,
                  hf_ref, hb_ref,
                  c_sc, n_sc, m_sc):
    b = pl.program_id(0)
    j = pl.program_id(1)
    nc = pl.num_programs(1)

    @pl.when(j == 0)
    def _():
        c_sc[...] = jnp.zeros_like(c_sc)
        n_sc[...] = jnp.zeros_like(n_sc)
        for st in range(N_DIRS * N_HEADS):
            m_sc[st] = jnp.float32(0.0)

    per_chunk = N_DIRS * 2 * N_HEADS
    base = ((b * nc + j) * per_chunk, (b * nc + (nc - 1 - j)) * per_chunk + 2 * N_HEADS)
    dirs = ((qf_ref, ktf_ref, vf_ref, colsf_ref, rowsf_ref, hf_ref),
            (qb_ref, ktb_ref, vb_ref, colsb_ref, rowsb_ref, hb_ref))
    units = [(d, h) for h in range(N_HEADS) for d in range(N_DIRS)]
    t_idx = lax.broadcasted_iota(jnp.int32, (CHUNK, CHUNK), 0)
    s_idx = lax.broadcasted_iota(jnp.int32, (CHUNK, CHUNK), 1)
    causal = (s_idx <= t_idx, s_idx >= t_idx)

    for g in range(0, len(units), SWEEP_GROUP):
        states = {}
        for d, h in units[g:g + SWEEP_GROUP]:
            st = d * N_HEADS + h
            q_ref, kt_ref, v_ref, _, rows_ref, _ = dirs[d]
            carry, states[d, h] = _sweep_state_phase(
                h, (c_sc[st], n_sc[st], m_sc[st]), q_ref, kt_ref, v_ref, rows_ref, sc_ref, base[d])
            c_sc[st], n_sc[st], m_sc[st] = carry
        for d, h in units[g:g + SWEEP_GROUP]:
            _sweep_output_phase(h, states[d, h], dirs[d][2], dirs[d][3], causal[d], dirs[d][5])


def _combine_kernel(hf_ref, hb_ref, c_ref, szm_ref, pxp_ref, px_ref, pxn_ref, szp_ref, x_ref,
                    pmat_ref, poolw_ref, mhw_ref, skipw_ref, wout_ref, gout_ref,
                    o_ref):
    i = pl.program_id(1)
    n_tiles = pl.num_programs(1)
    tile = OUT_TILE

    y_m_parts = []
    for h in range(N_HEADS):
        hs = slice(h * HEAD_DIM, (h + 1) * HEAD_DIM)
        ht = hf_ref[0, :, hs].astype(F32) + hb_ref[0, :, hs].astype(F32)
        mu = jnp.mean(ht, axis=-1, keepdims=True)
        dlt = ht - mu
        var = jnp.mean(dlt * dlt, axis=-1, keepdims=True)
        hn = (dlt * lax.rsqrt(var + EPS) * mhw_ref[:, hs]).astype(BF16)
        y_m_parts.append((hn + skipw_ref[:, hs] * c_ref[0, :, hs]) * szm_ref[0, :, hs])
    y_m = jnp.concatenate(y_m_parts, axis=1)

    zero_halo = jnp.zeros((HALO, D_POOL), BF16)
    px_main = px_ref[0]
    ext = jnp.concatenate(
        [jnp.where(i == 0, zero_halo, pxp_ref[0]), px_main,
         jnp.where(i == n_tiles - 1, zero_halo, pxn_ref[0]),
         jnp.zeros((CHUNK - 2 * HALO, D_POOL), BF16)], axis=0)
    t_glob = i * tile + lax.broadcasted_iota(jnp.int32, (tile, 1), 0)
    seq_last = n_tiles * tile - 1
    y_p_parts = []
    for g, w in enumerate(POOL_WINDOWS):
        gs = slice(g * POOL_GROUP_DIM, (g + 1) * POOL_GROUP_DIM)
        left = (w - 1) // 2
        right = w - 1 - left
        count = jnp.minimum(t_glob + right, seq_last) - jnp.maximum(t_glob - left, 0) + 1
        total = jnp.concatenate(
            [jnp.dot(pmat_ref[g], ext[ch * CHUNK:(ch + 2) * CHUNK, gs], preferred_element_type=F32)
             for ch in range(tile // CHUNK)], axis=0)
        pooled = total / count.astype(F32) - px_main[:, gs].astype(F32)
        mixed = jnp.dot(pooled.astype(BF16), poolw_ref[g], preferred_element_type=F32)
        y_p_parts.append(mixed.astype(BF16) * szp_ref[0, :, gs])
    y_p = jnp.concatenate(y_p_parts, axis=1)

    hres = x_ref[0] + jnp.dot(y_p, wout_ref[0:D_POOL, :], preferred_element_type=F32) \
        + jnp.dot(y_m, wout_ref[D_POOL:, :], preferred_element_type=F32)
    ms = jnp.mean(hres * hres, axis=-1, keepdims=True)
    o_ref[0] = hres * lax.rsqrt(ms + EPS) * gout_ref[...]


def _block_diag_tiles(w):
    rows = w.reshape(N_HEADS, HEAD_DIM, QKV_BLOCK)
    col = np.arange(HEAD_DIM)
    spread = jnp.asarray((col[None, :] % QKV_BLOCK == np.arange(QKV_BLOCK)[:, None]), w.dtype)
    tiled = jnp.einsum('tro,oc->trc', rows, spread, precision=lax.Precision.HIGHEST)
    same_block = jnp.asarray(col[:, None] // QKV_BLOCK == col[None, :] // QKV_BLOCK)
    return jnp.where(same_block[None], tiled, 0.0)


def _gate_weights(w_gates, b_gates, bq, bk, bv):
    n_gates = N_DIRS * 2 * N_HEADS
    place = np.zeros((n_gates, N_DIRS * DIR_LANES), np.float32)
    for d in range(N_DIRS):
        for g in range(2 * N_HEADS):
            lane = (I_LANE + g) if g < N_HEADS else (F_LANE + g - N_HEADS)
            place[d * 2 * N_HEADS + g, d * DIR_LANES + lane] = 1.0
    place = jnp.asarray(place)
    hi = lax.Precision.HIGHEST
    rows = jnp.transpose(w_gates, (1, 0, 2)).reshape(3, N_HEADS, HEAD_DIM, n_gates)
    w_c = jnp.einsum('trc,tcg->trg', bq, rows[0], precision=hi) \
        + jnp.einsum('trc,tcg->trg', bk, rows[1], precision=hi)
    w_v = jnp.einsum('trc,tcg->trg', bv, rows[2], precision=hi)
    folded = jnp.stack([w_c, w_v]).reshape(2, D_MLSTM, n_gates)
    return (jnp.einsum('krg,gl->krl', folded, place, precision=hi),
            jnp.dot(b_gates.reshape(1, n_gates), place, precision=hi))


def _pool_band_matrices():
    t = np.arange(CHUNK)[:, None]
    r = np.arange(2 * CHUNK)[None, :] - HALO
    mats = []
    for w in POOL_WINDOWS:
        left = (w - 1) // 2
        right = w - 1 - left
        mats.append(((r >= t - left) & (r <= t + right)).astype(np.float32))
    return jnp.asarray(np.stack(mats), dtype=BF16)


def _conv_shift_matrix():
    t = np.arange(CHUNK)[:, None]
    r = np.arange(2 * CHUNK)[None, :] - HALO
    blocks = [r == t + (tap - CONV_WIDTH // 2) for tap in CONV_SIDE_TAPS]
    return jnp.asarray(np.concatenate(blocks, axis=0).astype(np.float32), dtype=BF16)


def _full(shape):
    return pl.BlockSpec(shape, lambda b, j: (0,) * len(shape))


def _halo_specs(tile, n_halo, col):
    per = tile // HALO
    prev = pl.BlockSpec((1, HALO, D_MODEL), lambda b, i: (b, jnp.maximum(i * per - 1, 0), col))
    nxt = pl.BlockSpec((1, HALO, D_MODEL),
                       lambda b, i: (b, jnp.minimum((i + 1) * per, n_halo - 1), col))
    return prev, nxt


def kernel(x, norm_in_g, w_in, pool_w, pool_scale, conv_w, conv_b, w_q, w_k, w_v, w_gates,
           b_gates, mh_norm_w, skip_w, w_out, norm_out_g):
    B, S, D = x.shape
    assert D == D_MODEL and (B * S) % IN_TILE == 0
    assert S % PREP_TILE == 0 and S % OUT_TILE == 0 and PREP_TILE % CHUNK == 0
    assert (PREP_TILE // CHUNK) * GROUP_LANES <= DIR_LANES and CONV_WIDTH // 2 <= HALO
    assert norm_in_g.shape[0] == 1, "single-layer block"
    nc = S // CHUNK
    tokens = B * S
    n_halo = S // HALO
    arb2 = pltpu.CompilerParams(dimension_semantics=("arbitrary", "arbitrary"),
                                vmem_limit_bytes=VMEM_LIMIT)

    proj = pl.pallas_call(
        _inproj_kernel,
        grid=(tokens // IN_TILE,),
        in_specs=[pl.BlockSpec((IN_TILE, D), lambda i: (i, 0)),
                  pl.BlockSpec((1, D), lambda i: (0, 0)),
                  pl.BlockSpec((D, 4 * D), lambda i: (0, 0), pipeline_mode=pl.Buffered(1))],
        out_specs=pl.BlockSpec((IN_TILE, 4 * D), lambda i: (i, 0)),
        out_shape=jax.ShapeDtypeStruct((tokens, 4 * D), BF16),
        compiler_params=pltpu.CompilerParams(dimension_semantics=("arbitrary",),
                                             vmem_limit_bytes=VMEM_LIMIT),
        name="inproj",
    )(x.reshape(tokens, D), norm_in_g[0][None, :], w_in[0].astype(BF16))
    proj = proj.reshape(B, S, 4 * D)

    bq, bk, bv = (_block_diag_tiles(w[0]) for w in (w_q, w_k, w_v))
    wg, bg = _gate_weights(w_gates[0], b_gates[0], bq, bk, bv)
    wq_t, wk_t, wv_t = bq.astype(BF16), (bk * float(HEAD_DIM) ** -0.5).astype(BF16), bv.astype(BF16)
    conv_w8 = jnp.pad(conv_w[0], ((0, 8 - CONV_WIDTH), (0, 0)))

    mx_prev, mx_next = _halo_specs(PREP_TILE, n_halo, 2)
    prep_seq = pl.BlockSpec((1, PREP_TILE, D), lambda b, i: (b, i, 0))
    prep_cols = pl.BlockSpec((1, PREP_TILE, DIR_LANES), lambda b, i: (b, i, 0))
    cpt = PREP_TILE // CHUNK
    prep_rows = pl.BlockSpec((1, cpt, SCAN_ROWS, CHUNK), lambda b, i: (b, i, 0, 0))
    seq_bf = jax.ShapeDtypeStruct((B, S, D), BF16)
    cols_shape = jax.ShapeDtypeStruct((B, S, DIR_LANES), F32)
    rows_shape = jax.ShapeDtypeStruct((B, nc, SCAN_ROWS, CHUNK), F32)
    prep_kt = pl.BlockSpec((1, cpt, D, CHUNK), lambda b, i: (b, i, 0, 0))
    kt_shape = jax.ShapeDtypeStruct((B, nc, D, CHUNK), BF16)
    q, kt, v, c, cols_f, cols_b, rows_f, rows_b = pl.pallas_call(
        _prep_kernel,
        grid=(B, S // PREP_TILE),
        in_specs=[mx_prev, pl.BlockSpec((1, PREP_TILE, D), lambda b, i: (b, i, 2)), mx_next,
                  _full((len(CONV_SIDE_TAPS) * CHUNK, 2 * CHUNK)),
                  _full((8, D)), _full((1, D)),
                  _full((N_HEADS, HEAD_DIM, HEAD_DIM)), _full((N_HEADS, HEAD_DIM, HEAD_DIM)),
                  _full((N_HEADS, HEAD_DIM, HEAD_DIM)),
                  _full((2, D, N_DIRS * DIR_LANES)), _full((1, N_DIRS * DIR_LANES))],
        out_specs=[prep_seq, prep_kt, prep_seq, prep_seq,
                   prep_cols, prep_cols, prep_rows, prep_rows],
        out_shape=[seq_bf, kt_shape, seq_bf, seq_bf,
                   cols_shape, cols_shape, rows_shape, rows_shape],
        compiler_params=arb2,
        name="prep",
    )(proj, proj, proj, _conv_shift_matrix(), conv_w8, conv_b[0][None, :], wq_t, wk_t, wv_t,
      wg.astype(BF16), bg)

    def last(rows, lane):
        return jnp.concatenate([rows[:, :, F_LANE:F_LANE + N_HEADS, lane],
                                rows[:, :, CM_LANE:CM_LANE + N_HEADS, lane]], axis=-1)

    chunk_scalars = jnp.concatenate([last(rows_f, CHUNK - 1), last(rows_b, 0)], axis=-1).reshape(-1)

    def sweep_specs(idx):
        seq = pl.BlockSpec((1, CHUNK, D), lambda b, j: (b, idx(j), 0))
        cols = pl.BlockSpec((1, CHUNK, DIR_LANES), lambda b, j: (b, idx(j), 0))
        rows = pl.BlockSpec((1, 1, 8, CHUNK), lambda b, j: (b, idx(j), 0, 0))
        kt_spec = pl.BlockSpec((1, 1, D, CHUNK), lambda b, j: (b, idx(j), 0, 0))
        return seq, [seq, kt_spec, seq, cols, rows]

    seq_f, in_f = sweep_specs(lambda j: j)
    seq_b, in_b = sweep_specs(lambda j: nc - 1 - j)
    n_state = N_DIRS * N_HEADS
    h_fwd, h_bwd = pl.pallas_call(
        _sweep_kernel,
        grid=(B, nc),
        in_specs=[pl.BlockSpec(memory_space=pltpu.SMEM)] + in_f + in_b,
        out_specs=[seq_f, seq_b],
        out_shape=[seq_bf, seq_bf],
        scratch_shapes=[pltpu.VMEM((n_state, HEAD_DIM, HEAD_DIM), F32),
                        pltpu.VMEM((n_state, 8, HEAD_DIM), F32),
                        pltpu.SMEM((n_state,), F32)],
        compiler_params=arb2,
        name="sweep",
    )(chunk_scalars, q, kt, v, cols_f, rows_f, q, kt, v, cols_b, rows_b)

    px_prev, px_next = _halo_specs(OUT_TILE, n_halo, 0)
    pool_w_scaled = pool_w[0] * pool_scale[0].reshape(len(POOL_WINDOWS), 1, POOL_GROUP_DIM)

    def out_col(col):
        return pl.BlockSpec((1, OUT_TILE, D), lambda b, i: (b, i, col))

    out_seq = out_col(0)
    out = pl.pallas_call(
        _combine_kernel,
        grid=(B, S // OUT_TILE),
        in_specs=[out_seq, out_seq, out_seq, out_col(3),
                  px_prev, out_col(0), px_next, out_col(1), out_seq,
                  _full((len(POOL_WINDOWS), CHUNK, 2 * CHUNK)),
                  _full((len(POOL_WINDOWS), POOL_GROUP_DIM, POOL_GROUP_DIM)),
                  _full((1, D)), _full((1, D)),
                  _full((2 * D, D)), _full((1, D))],
        out_specs=out_seq,
        out_shape=jax.ShapeDtypeStruct((B, S, D), F32),
        compiler_params=arb2,
        name="combine",
    )(h_fwd, h_bwd, c, proj, proj, proj, proj, proj, x,
      _pool_band_matrices(), pool_w_scaled.astype(BF16),
      mh_norm_w[0][None, :], skip_w[0][None, :].astype(BF16), w_out[0].astype(BF16),
      norm_out_g[None, :])
    return out
```

```python
import numpy as np
import jax
import jax.numpy as jnp
from jax import lax
from jax.experimental import pallas as pl
from jax.experimental.pallas import tpu as pltpu

D_MODEL = 1024
D_POOL = 1024
D_MLSTM = 1024
POOL_WINDOWS = (2, 4, 8, 16)
POOL_GROUP_DIM = D_POOL // len(POOL_WINDOWS)
N_HEADS = 4
HEAD_DIM = 256
QKV_BLOCK = 4
CONV_WIDTH = 5
CHUNK = 128
N_DIRS = 2
EPS = 1e-6

X_GROUPS, Z_GROUPS = (0, 2), (1, 3)
HALO = 16
DIR_LANES = 128
GROUP_LANES = 16
I_LANE, CM_LANE, F_LANE = 0, 4, 8
SCAN_ROWS = 16
CONV_SIDE_TAPS = (0, 1, 3, 4)
IN_TILE = 1024
PREP_TILE = 512
OUT_TILE = 512
SWEEP_GROUP = 4
SWEEP_CHUNKS = 2
VMEM_LIMIT = 48 * 1024 * 1024

F32 = jnp.float32
BF16 = jnp.bfloat16


def _silu(z):
    return z * (1.0 / (1.0 + jnp.exp(-z)))


def _log_sigmoid(g):
    return jnp.minimum(g, 0.0) - jnp.log1p(jnp.exp(-jnp.abs(g)))


def _inproj_kernel(x_ref, g_ref, w_ref, o_ref):
    x = x_ref[...]
    ms = jnp.mean(x * x, axis=-1, keepdims=True)
    u = (x * lax.rsqrt(ms + EPS) * g_ref[...]).astype(BF16)
    for n in Z_GROUPS + X_GROUPS:
        cols = slice(n * D_MODEL, (n + 1) * D_MODEL)
        acc = jnp.dot(u, w_ref[:, cols], preferred_element_type=F32)
        if n in Z_GROUPS:
            acc = _silu(acc)
        o_ref[:, cols] = acc.astype(BF16)


def _token_scan(x, op, reverse):
    t = lax.broadcasted_iota(jnp.int32, x.shape, 0)
    k = 1
    while k < CHUNK:
        if reverse:
            shifted = pltpu.roll(x, CHUNK - k, 0)
            valid = t < CHUNK - k
        else:
            shifted = pltpu.roll(x, k, 0)
            valid = t >= k
        x = jnp.where(valid, op(x, shifted), x)
        k *= 2
    return x


def _prep_kernel(mxp_ref, mx_ref, mxn_ref, shift_ref, convw_ref, convb_ref, wq_ref, wk_ref, wv_ref,
                 wg_ref, bg_ref,
                 q_ref, kt_ref, v_ref, c_ref, colsf_ref, colsb_ref, rowsf_ref, rowsb_ref):
    i = pl.program_id(1)
    n_tiles = pl.num_programs(1)
    tile = PREP_TILE
    n_chunks = tile // CHUNK

    mx_bf = mx_ref[0]
    zero_halo = jnp.zeros((HALO, D_MLSTM), BF16)
    ext = jnp.concatenate(
        [jnp.where(i == 0, zero_halo, mxp_ref[0]), mx_bf,
         jnp.where(i == n_tiles - 1, zero_halo, mxn_ref[0]),
         jnp.zeros((CHUNK - 2 * HALO, D_MLSTM), BF16)], axis=0)

    pad = CONV_WIDTH // 2
    conv_parts = []
    for ch in range(n_chunks):
        rs = slice(ch * CHUNK, (ch + 1) * CHUNK)
        shifted = jnp.dot(shift_ref[...], ext[ch * CHUNK:(ch + 2) * CHUNK, :],
                          preferred_element_type=F32)
        part = convb_ref[...] + mx_bf[rs, :].astype(F32) * convw_ref[pad:pad + 1, :]
        for n, tap in enumerate(CONV_SIDE_TAPS):
            part = part + shifted[n * CHUNK:(n + 1) * CHUNK, :] * convw_ref[tap:tap + 1, :]
        conv_parts.append(part)
    c_bf = _silu(jnp.concatenate(conv_parts, axis=0)).astype(BF16)
    c_ref[0] = c_bf

    gates = bg_ref[...] + jnp.dot(c_bf, wg_ref[0], preferred_element_type=F32) \
        + jnp.dot(mx_bf, wg_ref[1], preferred_element_type=F32)

    sub = lax.broadcasted_iota(jnp.int32, (CHUNK, DIR_LANES), 1) % GROUP_LANES
    for d, (cols_ref, rows_ref) in enumerate(((colsf_ref, rowsf_ref), (colsb_ref, rowsb_ref))):
        ds = slice(d * DIR_LANES, (d + 1) * DIR_LANES)
        packed = gates[0:CHUNK, ds]
        for ch in range(1, n_chunks):
            packed = packed + pltpu.roll(gates[ch * CHUNK:(ch + 1) * CHUNK, ds], ch * GROUP_LANES, 1)
        b = _token_scan(_log_sigmoid(packed), jnp.add, reverse=(d == 1))
        a = packed - pltpu.roll(b, DIR_LANES - (F_LANE - I_LANE), 1)
        cm = _token_scan(a, jnp.maximum, reverse=(d == 1))
        scan = jnp.where(sub < CM_LANE, a,
                         jnp.where(sub < F_LANE, pltpu.roll(cm, CM_LANE - I_LANE, 1), b))
        for ch in range(n_chunks):
            cols = scan if ch == 0 else pltpu.roll(scan, DIR_LANES - ch * GROUP_LANES, 1)
            cols_ref[0, ch * CHUNK:(ch + 1) * CHUNK, :] = cols
            rows_ref[0, ch] = cols.T[0:SCAN_ROWS, :]

    for h in range(N_HEADS):
        hs = slice(h * HEAD_DIM, (h + 1) * HEAD_DIM)
        q_ref[0, :, hs] = jnp.dot(c_bf[:, hs], wq_ref[h], preferred_element_type=F32).astype(BF16)
        v_ref[0, :, hs] = jnp.dot(mx_bf[:, hs], wv_ref[h], preferred_element_type=F32).astype(BF16)
        kh = jnp.dot(c_bf[:, hs], wk_ref[h], preferred_element_type=F32)
        for ch in range(n_chunks):
            kt_ref[0, ch, hs, :] = kh[ch * CHUNK:(ch + 1) * CHUNK, :].T.astype(BF16)


def _lane_bcast(tile, lane):
    return jnp.broadcast_to(tile[:, lane:lane + 1], tile.shape)


def _sweep_state_phase(h, off, carry, q_ref, kt_ref, v_ref, rows_ref, sc_ref, sc_base):
    hs = slice(h * HEAD_DIM, (h + 1) * HEAD_DIM)
    rs = pl.ds(pl.multiple_of(off * CHUNK, CHUNK), CHUNK)
    c_old, n_old, m = carry
    b_last = sc_ref[sc_base + h]
    cm_last = sc_ref[sc_base + N_HEADS + h]
    m_last = jnp.maximum(m, cm_last)

    a_row = rows_ref[0, off, I_LANE + h:I_LANE + h + 1, :]
    ws_row = jnp.exp(a_row - m_last)
    decay = jnp.exp(jnp.full((1, HEAD_DIM), m - m_last, F32))

    qh, vh = q_ref[0, rs, hs], v_ref[0, rs, hs]
    kt = kt_ref[0, off, hs, :]
    q_c = jnp.dot(qh, c_old.astype(BF16), preferred_element_type=F32)
    qk = jnp.dot(qh, kt, preferred_element_type=F32)
    q_n = qh.astype(F32) * n_old[0:1, :]
    q_n = q_n[:, :CHUNK] + q_n[:, CHUNK:]

    ws_bf = ws_row.astype(BF16)
    kv = jnp.dot(kt * ws_bf, vh, preferred_element_type=F32)
    ws8 = jnp.broadcast_to(ws_bf, (8, CHUNK))
    kn = lax.dot_general(ws8, kt, (((1,), (1,)), ((), ())), preferred_element_type=F32)
    new_carry = (decay * c_old + kv, decay * n_old + kn, b_last + m_last)
    return new_carry, (m, a_row, q_c, qk, q_n)


def _sweep_output_phase(h, off, state, v_ref, cols_ref, causal, h_ref):
    hs = slice(h * HEAD_DIM, (h + 1) * HEAD_DIM)
    rs = pl.ds(pl.multiple_of(off * CHUNK, CHUNK), CHUNK)
    cols = cols_ref[0, rs, :]
    m, a_row, q_c, qk, q_n = state
    big_m = jnp.maximum(_lane_bcast(cols, CM_LANE + h), m)
    dmat = jnp.where(causal, jnp.exp(a_row - big_m), 0.0)
    inter_w = jnp.exp(m - big_m)
    exp_neg_mt = jnp.exp(-(_lane_bcast(cols, F_LANE + h) + big_m))
    s = qk * dmat
    den = jnp.sum(s + inter_w * q_n, axis=-1, keepdims=True)
    inv = 1.0 / jnp.maximum(jnp.abs(den), exp_neg_mt[:, 0:1])
    inv_b = jnp.broadcast_to(inv, (CHUNK, CHUNK))
    num = jnp.dot(s.astype(BF16), v_ref[0, rs, hs], preferred_element_type=F32)
    out = (num + jnp.concatenate([inter_w, inter_w], axis=1) * q_c) \
        * jnp.concatenate([inv_b, inv_b], axis=1)
    h_ref[0, rs, hs] = out.astype(h_ref.dtype)


def _sweep_kernel(sc_ref,
                  qf_ref, ktf_ref, vf_ref, colsf_ref, rowsf_ref,
                  qb_ref, ktb_ref, vb_ref, colsb_ref, rowsb_ref,
                  hf_ref, hb_ref,
                  c_sc, n_sc, m_sc):
    b = pl.program_id(0)
    j = pl.program_id(1)
    nc = pl.num_programs(1) * SWEEP_CHUNKS

    @pl.when(j == 0)
    def _():
        c_sc[...] = jnp.zeros_like(c_sc)
        n_sc[...] = jnp.zeros_like(n_sc)
        for st in range(N_DIRS * N_HEADS):
            m_sc[st] = jnp.float32(0.0)

    per_chunk = N_DIRS * 2 * N_HEADS
    dirs = ((qf_ref, ktf_ref, vf_ref, colsf_ref, rowsf_ref, hf_ref),
            (qb_ref, ktb_ref, vb_ref, colsb_ref, rowsb_ref, hb_ref))
    units = [(d, h) for h in range(N_HEADS) for d in range(N_DIRS)]
    t_idx = lax.broadcasted_iota(jnp.int32, (CHUNK, CHUNK), 0)
    s_idx = lax.broadcasted_iota(jnp.int32, (CHUNK, CHUNK), 1)
    causal = (s_idx <= t_idx, s_idx >= t_idx)

    def one_chunk(sub, _):
        off = (sub, SWEEP_CHUNKS - 1 - sub)
        chunk_f = j * SWEEP_CHUNKS + sub
        base = ((b * nc + chunk_f) * per_chunk,
                (b * nc + (nc - 1 - chunk_f)) * per_chunk + 2 * N_HEADS)
        for g in range(0, len(units), SWEEP_GROUP):
            states = {}
            for d, h in units[g:g + SWEEP_GROUP]:
                st = d * N_HEADS + h
                q_ref, kt_ref, v_ref, _, rows_ref, _ = dirs[d]
                carry, states[d, h] = _sweep_state_phase(
                    h, off[d], (c_sc[st], n_sc[st], m_sc[st]), q_ref, kt_ref, v_ref, rows_ref,
                    sc_ref, base[d])
                c_sc[st], n_sc[st], m_sc[st] = carry
            for d, h in units[g:g + SWEEP_GROUP]:
                _sweep_output_phase(h, off[d], states[d, h], dirs[d][2], dirs[d][3], causal[d],
                                    dirs[d][5])

    lax.fori_loop(0, SWEEP_CHUNKS, one_chunk, None)


def _combine_kernel(hf_ref, hb_ref, c_ref, szm_ref, pxp_ref, px_ref, pxn_ref, szp_ref, x_ref,
                    pmat_ref, poolw_ref, mhw_ref, skipw_ref, wout_ref, gout_ref,
                    o_ref):
    i = pl.program_id(1)
    n_tiles = pl.num_programs(1)
    tile = OUT_TILE

    y_m_parts = []
    for h in range(N_HEADS):
        hs = slice(h * HEAD_DIM, (h + 1) * HEAD_DIM)
        ht = hf_ref[0, :, hs].astype(F32) + hb_ref[0, :, hs].astype(F32)
        mu = jnp.mean(ht, axis=-1, keepdims=True)
        dlt = ht - mu
        var = jnp.mean(dlt * dlt, axis=-1, keepdims=True)
        hn = (dlt * lax.rsqrt(var + EPS) * mhw_ref[:, hs]).astype(BF16)
        y_m_parts.append((hn + skipw_ref[:, hs] * c_ref[0, :, hs]) * szm_ref[0, :, hs])
    y_m = jnp.concatenate(y_m_parts, axis=1)

    zero_halo = jnp.zeros((HALO, D_POOL), BF16)
    px_main = px_ref[0]
    ext = jnp.concatenate(
        [jnp.where(i == 0, zero_halo, pxp_ref[0]), px_main,
         jnp.where(i == n_tiles - 1, zero_halo, pxn_ref[0]),
         jnp.zeros((CHUNK - 2 * HALO, D_POOL), BF16)], axis=0)
    t_glob = i * tile + lax.broadcasted_iota(jnp.int32, (tile, 1), 0)
    seq_last = n_tiles * tile - 1
    y_p_parts = []
    for g, w in enumerate(POOL_WINDOWS):
        gs = slice(g * POOL_GROUP_DIM, (g + 1) * POOL_GROUP_DIM)
        left = (w - 1) // 2
        right = w - 1 - left
        count = jnp.minimum(t_glob + right, seq_last) - jnp.maximum(t_glob - left, 0) + 1
        total = jnp.concatenate(
            [jnp.dot(pmat_ref[g], ext[ch * CHUNK:(ch + 2) * CHUNK, gs], preferred_element_type=F32)
             for ch in range(tile // CHUNK)], axis=0)
        pooled = total / count.astype(F32) - px_main[:, gs].astype(F32)
        mixed = jnp.dot(pooled.astype(BF16), poolw_ref[g], preferred_element_type=F32)
        y_p_parts.append(mixed.astype(BF16) * szp_ref[0, :, gs])
    y_p = jnp.concatenate(y_p_parts, axis=1)

    hres = x_ref[0] + jnp.dot(y_p, wout_ref[0:D_POOL, :], preferred_element_type=F32) \
        + jnp.dot(y_m, wout_ref[D_POOL:, :], preferred_element_type=F32)
    ms = jnp.mean(hres * hres, axis=-1, keepdims=True)
    o_ref[0] = hres * lax.rsqrt(ms + EPS) * gout_ref[...]


def _block_diag_tiles(w):
    rows = w.reshape(N_HEADS, HEAD_DIM, QKV_BLOCK)
    col = np.arange(HEAD_DIM)
    spread = jnp.asarray((col[None, :] % QKV_BLOCK == np.arange(QKV_BLOCK)[:, None]), w.dtype)
    tiled = jnp.einsum('tro,oc->trc', rows, spread, precision=lax.Precision.HIGHEST)
    same_block = jnp.asarray(col[:, None] // QKV_BLOCK == col[None, :] // QKV_BLOCK)
    return jnp.where(same_block[None], tiled, 0.0)


def _gate_weights(w_gates, b_gates, bq, bk, bv):
    n_gates = N_DIRS * 2 * N_HEADS
    place = np.zeros((n_gates, N_DIRS * DIR_LANES), np.float32)
    for d in range(N_DIRS):
        for g in range(2 * N_HEADS):
            lane = (I_LANE + g) if g < N_HEADS else (F_LANE + g - N_HEADS)
            place[d * 2 * N_HEADS + g, d * DIR_LANES + lane] = 1.0
    place = jnp.asarray(place)
    hi = lax.Precision.HIGHEST
    rows = jnp.transpose(w_gates, (1, 0, 2)).reshape(3, N_HEADS, HEAD_DIM, n_gates)
    w_c = jnp.einsum('trc,tcg->trg', bq, rows[0], precision=hi) \
        + jnp.einsum('trc,tcg->trg', bk, rows[1], precision=hi)
    w_v = jnp.einsum('trc,tcg->trg', bv, rows[2], precision=hi)
    folded = jnp.stack([w_c, w_v]).reshape(2, D_MLSTM, n_gates)
    return (jnp.einsum('krg,gl->krl', folded, place, precision=hi),
            jnp.dot(b_gates.reshape(1, n_gates), place, precision=hi))


def _pool_band_matrices():
    t = np.arange(CHUNK)[:, None]
    r = np.arange(2 * CHUNK)[None, :] - HALO
    mats = []
    for w in POOL_WINDOWS:
        left = (w - 1) // 2
        right = w - 1 - left
        mats.append(((r >= t - left) & (r <= t + right)).astype(np.float32))
    return jnp.asarray(np.stack(mats), dtype=BF16)


def _conv_shift_matrix():
    t = np.arange(CHUNK)[:, None]
    r = np.arange(2 * CHUNK)[None, :] - HALO
    blocks = [r == t + (tap - CONV_WIDTH // 2) for tap in CONV_SIDE_TAPS]
    return jnp.asarray(np.concatenate(blocks, axis=0).astype(np.float32), dtype=BF16)


def _full(shape):
    return pl.BlockSpec(shape, lambda b, j: (0,) * len(shape))


def _halo_specs(tile, n_halo, col):
    per = tile // HALO
    prev = pl.BlockSpec((1, HALO, D_MODEL), lambda b, i: (b, jnp.maximum(i * per - 1, 0), col))
    nxt = pl.BlockSpec((1, HALO, D_MODEL),
                       lambda b, i: (b, jnp.minimum((i + 1) * per, n_halo - 1), col))
    return prev, nxt


def kernel(x, norm_in_g, w_in, pool_w, pool_scale, conv_w, conv_b, w_q, w_k, w_v, w_gates,
           b_gates, mh_norm_w, skip_w, w_out, norm_out_g):
    B, S, D = x.shape
    assert D == D_MODEL and (B * S) % IN_TILE == 0
    assert S % PREP_TILE == 0 and S % OUT_TILE == 0 and PREP_TILE % CHUNK == 0
    assert (PREP_TILE // CHUNK) * GROUP_LANES <= DIR_LANES and CONV_WIDTH // 2 <= HALO
    assert norm_in_g.shape[0] == 1, "single-layer block"
    nc = S // CHUNK
    tokens = B * S
    n_halo = S // HALO
    arb2 = pltpu.CompilerParams(dimension_semantics=("arbitrary", "arbitrary"),
                                vmem_limit_bytes=VMEM_LIMIT)

    proj = pl.pallas_call(
        _inproj_kernel,
        grid=(tokens // IN_TILE,),
        in_specs=[pl.BlockSpec((IN_TILE, D), lambda i: (i, 0)),
                  pl.BlockSpec((1, D), lambda i: (0, 0)),
                  pl.BlockSpec((D, 4 * D), lambda i: (0, 0), pipeline_mode=pl.Buffered(1))],
        out_specs=pl.BlockSpec((IN_TILE, 4 * D), lambda i: (i, 0)),
        out_shape=jax.ShapeDtypeStruct((tokens, 4 * D), BF16),
        compiler_params=pltpu.CompilerParams(dimension_semantics=("arbitrary",),
                                             vmem_limit_bytes=VMEM_LIMIT),
        name="inproj",
    )(x.reshape(tokens, D), norm_in_g[0][None, :], w_in[0].astype(BF16))
    proj = proj.reshape(B, S, 4 * D)

    bq, bk, bv = (_block_diag_tiles(w[0]) for w in (w_q, w_k, w_v))
    wg, bg = _gate_weights(w_gates[0], b_gates[0], bq, bk, bv)
    wq_t, wk_t, wv_t = bq.astype(BF16), (bk * float(HEAD_DIM) ** -0.5).astype(BF16), bv.astype(BF16)
    conv_w8 = jnp.pad(conv_w[0], ((0, 8 - CONV_WIDTH), (0, 0)))

    mx_prev, mx_next = _halo_specs(PREP_TILE, n_halo, 2)
    prep_seq = pl.BlockSpec((1, PREP_TILE, D), lambda b, i: (b, i, 0))
    prep_cols = pl.BlockSpec((1, PREP_TILE, DIR_LANES), lambda b, i: (b, i, 0))
    cpt = PREP_TILE // CHUNK
    prep_rows = pl.BlockSpec((1, cpt, SCAN_ROWS, CHUNK), lambda b, i: (b, i, 0, 0))
    seq_bf = jax.ShapeDtypeStruct((B, S, D), BF16)
    cols_shape = jax.ShapeDtypeStruct((B, S, DIR_LANES), F32)
    rows_shape = jax.ShapeDtypeStruct((B, nc, SCAN_ROWS, CHUNK), F32)
    prep_kt = pl.BlockSpec((1, cpt, D, CHUNK), lambda b, i: (b, i, 0, 0))
    kt_shape = jax.ShapeDtypeStruct((B, nc, D, CHUNK), BF16)
    q, kt, v, c, cols_f, cols_b, rows_f, rows_b = pl.pallas_call(
        _prep_kernel,
        grid=(B, S // PREP_TILE),
        in_specs=[mx_prev, pl.BlockSpec((1, PREP_TILE, D), lambda b, i: (b, i, 2)), mx_next,
                  _full((len(CONV_SIDE_TAPS) * CHUNK, 2 * CHUNK)),
                  _full((8, D)), _full((1, D)),
                  _full((N_HEADS, HEAD_DIM, HEAD_DIM)), _full((N_HEADS, HEAD_DIM, HEAD_DIM)),
                  _full((N_HEADS, HEAD_DIM, HEAD_DIM)),
                  _full((2, D, N_DIRS * DIR_LANES)), _full((1, N_DIRS * DIR_LANES))],
        out_specs=[prep_seq, prep_kt, prep_seq, prep_seq,
                   prep_cols, prep_cols, prep_rows, prep_rows],
        out_shape=[seq_bf, kt_shape, seq_bf, seq_bf,
                   cols_shape, cols_shape, rows_shape, rows_shape],
        compiler_params=arb2,
        name="prep",
    )(proj, proj, proj, _conv_shift_matrix(), conv_w8, conv_b[0][None, :], wq_t, wk_t, wv_t,
      wg.astype(BF16), bg)

    def last(rows, lane):
        return jnp.concatenate([rows[:, :, F_LANE:F_LANE + N_HEADS, lane],
                                rows[:, :, CM_LANE:CM_LANE + N_HEADS, lane]], axis=-1)

    chunk_scalars = jnp.concatenate([last(rows_f, CHUNK - 1), last(rows_b, 0)], axis=-1).reshape(-1)

    sweep_steps = nc // SWEEP_CHUNKS
    sweep_rows = SWEEP_CHUNKS * CHUNK

    def sweep_specs(idx):
        seq = pl.BlockSpec((1, sweep_rows, D), lambda b, j: (b, idx(j), 0))
        cols = pl.BlockSpec((1, sweep_rows, DIR_LANES), lambda b, j: (b, idx(j), 0))
        rows = pl.BlockSpec((1, SWEEP_CHUNKS, 8, CHUNK), lambda b, j: (b, idx(j), 0, 0))
        kt_spec = pl.BlockSpec((1, SWEEP_CHUNKS, D, CHUNK), lambda b, j: (b, idx(j), 0, 0))
        return seq, [seq, kt_spec, seq, cols, rows]

    seq_f, in_f = sweep_specs(lambda j: j)
    seq_b, in_b = sweep_specs(lambda j: sweep_steps - 1 - j)
    n_state = N_DIRS * N_HEADS
    h_fwd, h_bwd = pl.pallas_call(
        _sweep_kernel,
        grid=(B, sweep_steps),
        in_specs=[pl.BlockSpec(memory_space=pltpu.SMEM)] + in_f + in_b,
        out_specs=[seq_f, seq_b],
        out_shape=[seq_bf, seq_bf],
        scratch_shapes=[pltpu.VMEM((n_state, HEAD_DIM, HEAD_DIM), F32),
                        pltpu.VMEM((n_state, 8, HEAD_DIM), F32),
                        pltpu.SMEM((n_state,), F32)],
        compiler_params=arb2,
        name="sweep",
    )(chunk_scalars, q, kt, v, cols_f, rows_f, q, kt, v, cols_b, rows_b)

    px_prev, px_next = _halo_specs(OUT_TILE, n_halo, 0)
    pool_w_scaled = pool_w[0] * pool_scale[0].reshape(len(POOL_WINDOWS), 1, POOL_GROUP_DIM)

    def out_col(col):
        return pl.BlockSpec((1, OUT_TILE, D), lambda b, i: (b, i, col))

    out_seq = out_col(0)
    out = pl.pallas_call(
        _combine_kernel,
        grid=(B, S // OUT_TILE),
        in_specs=[out_seq, out_seq, out_seq, out_col(3),
                  px_prev, out_col(0), px_next, out_col(1), out_seq,
                  _full((len(POOL_WINDOWS), CHUNK, 2 * CHUNK)),
                  _full((len(POOL_WINDOWS), POOL_GROUP_DIM, POOL_GROUP_DIM)),
                  _full((1, D)), _full((1, D)),
                  _full((2 * D, D)), _full((1, D))],
        out_specs=out_seq,
        out_shape=jax.ShapeDtypeStruct((B, S, D), F32),
        compiler_params=arb2,
        name="combine",
    )(h_fwd, h_bwd, c, proj, proj, proj, proj, proj, x,
      _pool_band_matrices(), pool_w_scaled.astype(BF16),
      mh_norm_w[0][None, :], skip_w[0][None, :].astype(BF16), w_out[0].astype(BF16),
      norm_out_g[None, :])
    return out
```

```python
import numpy as np
import jax
import jax.numpy as jnp
from jax import lax
from jax.experimental import pallas as pl
from jax.experimental.pallas import tpu as pltpu

D_MODEL = 1024
D_POOL = 1024
D_MLSTM = 1024
POOL_WINDOWS = (2, 4, 8, 16)
POOL_GROUP_DIM = D_POOL // len(POOL_WINDOWS)
N_HEADS = 4
HEAD_DIM = 256
QKV_BLOCK = 4
CONV_WIDTH = 5
CHUNK = 128
N_DIRS = 2
EPS = 1e-6
LOG2E = 1.4426950408889634

X_GROUPS, Z_GROUPS = (0, 2), (1, 3)
HALO = 16
DIR_LANES = 128
GROUP_LANES = 16
I_LANE, CM_LANE, F_LANE = 0, 4, 8
SCAN_ROWS = 16
CONV_SIDE_TAPS = (0, 1, 3, 4)
IN_TILE = 1024
PREP_TILE = 512
OUT_TILE = 512
OUT_ROWS = OUT_TILE
SWEEP_GROUP = 4
SWEEP_CHUNKS = 4
VMEM_LIMIT = 48 * 1024 * 1024

F32 = jnp.float32
BF16 = jnp.bfloat16


def _silu(z):
    return z * (1.0 / (1.0 + jnp.exp(-z)))


def _log_sigmoid(g):
    return jnp.minimum(g, 0.0) - jnp.log1p(jnp.exp(-jnp.abs(g)))


def _inproj_kernel(x_ref, g_ref, w_ref, o_ref):
    x = x_ref[...]
    ms = jnp.mean(x * x, axis=-1, keepdims=True)
    u = (x * lax.rsqrt(ms + EPS) * g_ref[...]).astype(BF16)
    for n in Z_GROUPS + X_GROUPS:
        cols = slice(n * D_MODEL, (n + 1) * D_MODEL)
        acc = jnp.dot(u, w_ref[:, cols], preferred_element_type=F32)
        if n in Z_GROUPS:
            acc = _silu(acc)
        o_ref[:, cols] = acc.astype(BF16)


def _token_scan(x, op, reverse):
    t = lax.broadcasted_iota(jnp.int32, x.shape, 0)
    k = 1
    while k < CHUNK:
        if reverse:
            shifted = pltpu.roll(x, CHUNK - k, 0)
            valid = t < CHUNK - k
        else:
            shifted = pltpu.roll(x, k, 0)
            valid = t >= k
        x = jnp.where(valid, op(x, shifted), x)
        k *= 2
    return x


def _prep_kernel(mxp_ref, mx_ref, mxn_ref, shift_ref, convw_ref, convb_ref, wq_ref, wk_ref, wv_ref,
                 wg_ref, bg_ref,
                 q_ref, kt_ref, v_ref, c_ref, colsf_ref, colsb_ref, rowsf_ref, rowsb_ref):
    i = pl.program_id(1)
    n_tiles = pl.num_programs(1)
    tile = PREP_TILE
    n_chunks = tile // CHUNK

    mx_bf = mx_ref[0]
    zero_halo = jnp.zeros((HALO, D_MLSTM), BF16)
    ext = jnp.concatenate(
        [jnp.where(i == 0, zero_halo, mxp_ref[0]), mx_bf,
         jnp.where(i == n_tiles - 1, zero_halo, mxn_ref[0]),
         jnp.zeros((CHUNK - 2 * HALO, D_MLSTM), BF16)], axis=0)

    pad = CONV_WIDTH // 2
    conv_parts = []
    for ch in range(n_chunks):
        rs = slice(ch * CHUNK, (ch + 1) * CHUNK)
        shifted = jnp.dot(shift_ref[...], ext[ch * CHUNK:(ch + 2) * CHUNK, :],
                          preferred_element_type=F32)
        part = convb_ref[...] + mx_bf[rs, :].astype(F32) * convw_ref[pad:pad + 1, :]
        for n, tap in enumerate(CONV_SIDE_TAPS):
            part = part + shifted[n * CHUNK:(n + 1) * CHUNK, :] * convw_ref[tap:tap + 1, :]
        conv_parts.append(part)
    c_bf = _silu(jnp.concatenate(conv_parts, axis=0)).astype(BF16)
    c_ref[0] = c_bf

    gates = bg_ref[...] + jnp.dot(c_bf, wg_ref[0], preferred_element_type=F32) \
        + jnp.dot(mx_bf, wg_ref[1], preferred_element_type=F32)

    sub = lax.broadcasted_iota(jnp.int32, (CHUNK, DIR_LANES), 1) % GROUP_LANES
    for d, (cols_ref, rows_ref) in enumerate(((colsf_ref, rowsf_ref), (colsb_ref, rowsb_ref))):
        ds = slice(d * DIR_LANES, (d + 1) * DIR_LANES)
        packed = gates[0:CHUNK, ds]
        for ch in range(1, n_chunks):
            packed = packed + pltpu.roll(gates[ch * CHUNK:(ch + 1) * CHUNK, ds], ch * GROUP_LANES, 1)
        b = _token_scan(_log_sigmoid(packed), jnp.add, reverse=(d == 1))
        a = packed - pltpu.roll(b, DIR_LANES - (F_LANE - I_LANE), 1)
        cm = _token_scan(a, jnp.maximum, reverse=(d == 1))
        scan = LOG2E * jnp.where(sub < CM_LANE, a,
                                 jnp.where(sub < F_LANE, pltpu.roll(cm, CM_LANE - I_LANE, 1), b))
        for ch in range(n_chunks):
            cols = scan if ch == 0 else pltpu.roll(scan, DIR_LANES - ch * GROUP_LANES, 1)
            cols_ref[0, ch * CHUNK:(ch + 1) * CHUNK, :] = cols
            rows_ref[0, ch] = cols.T[0:SCAN_ROWS, :]

    for h in range(N_HEADS):
        hs = slice(h * HEAD_DIM, (h + 1) * HEAD_DIM)
        q_ref[0, :, hs] = jnp.dot(c_bf[:, hs], wq_ref[h], preferred_element_type=F32).astype(BF16)
        v_ref[0, :, hs] = jnp.dot(mx_bf[:, hs], wv_ref[h], preferred_element_type=F32).astype(BF16)
        kh = jnp.dot(c_bf[:, hs], wk_ref[h], preferred_element_type=F32)
        for ch in range(n_chunks):
            kt_ref[0, ch, hs, :] = kh[ch * CHUNK:(ch + 1) * CHUNK, :].T.astype(BF16)


def _lane_bcast(tile, lane):
    return jnp.broadcast_to(tile[:, lane:lane + 1], tile.shape)


def _sweep_state_phase(h, off, carry, q_ref, kt_ref, v_ref, rows_ref, sc_ref, sc_base):
    hs = slice(h * HEAD_DIM, (h + 1) * HEAD_DIM)
    rs = pl.ds(pl.multiple_of(off * CHUNK, CHUNK), CHUNK)
    c_old, n_old, m = carry
    b_last = sc_ref[sc_base + h]
    cm_last = sc_ref[sc_base + N_HEADS + h]
    m_last = jnp.maximum(m, cm_last)

    a_row = rows_ref[0, off, I_LANE + h:I_LANE + h + 1, :]
    ws_row = jnp.exp2(a_row - m_last)
    decay = jnp.exp2(jnp.full((1, HEAD_DIM), m - m_last, F32))

    qh, vh = q_ref[0, rs, hs], v_ref[0, rs, hs]
    kt = kt_ref[0, off, hs, :]
    q_c = jnp.dot(qh, c_old.astype(BF16), preferred_element_type=F32)
    qk = jnp.dot(qh, kt, preferred_element_type=F32)
    q_n = qh.astype(F32) * n_old[0:1, :]
    q_n = q_n[:, :CHUNK] + q_n[:, CHUNK:]

    ws_bf = ws_row.astype(BF16)
    kv = jnp.dot(kt * ws_bf, vh, preferred_element_type=F32)
    ws8 = jnp.broadcast_to(ws_bf, (8, CHUNK))
    kn = lax.dot_general(ws8, kt, (((1,), (1,)), ((), ())), preferred_element_type=F32)
    new_carry = (decay * c_old + kv, decay * n_old + kn, b_last + m_last)
    return new_carry, (m, a_row, q_c, qk, q_n)


def _sweep_output_phase(h, off, state, v_ref, cols_ref, causal, h_ref):
    hs = slice(h * HEAD_DIM, (h + 1) * HEAD_DIM)
    rs = pl.ds(pl.multiple_of(off * CHUNK, CHUNK), CHUNK)
    cols = cols_ref[0, rs, :]
    m, a_row, q_c, qk, q_n = state
    big_m = jnp.maximum(_lane_bcast(cols, CM_LANE + h), m)
    dmat = jnp.where(causal, jnp.exp2(a_row - big_m), 0.0)
    inter_w = jnp.exp2(m - big_m)
    exp_neg_mt = jnp.exp2(-(_lane_bcast(cols, F_LANE + h) + big_m))
    s = qk * dmat
    den = jnp.sum(s + inter_w * q_n, axis=-1, keepdims=True)
    inv = 1.0 / jnp.maximum(jnp.abs(den), exp_neg_mt[:, 0:1])
    inv_b = jnp.broadcast_to(inv, (CHUNK, CHUNK))
    num = jnp.dot(s.astype(BF16), v_ref[0, rs, hs], preferred_element_type=F32)
    out = (num + jnp.concatenate([inter_w, inter_w], axis=1) * q_c) \
        * jnp.concatenate([inv_b, inv_b], axis=1)
    h_ref[0, rs, hs] = out.astype(h_ref.dtype)


def _sweep_kernel(sc_ref,
                  qf_ref, ktf_ref, vf_ref, colsf_ref, rowsf_ref,
                  qb_ref, ktb_ref, vb_ref, colsb_ref, rowsb_ref,
                  hf_ref, hb_ref,
                  c_sc, n_sc, m_sc):
    b = pl.program_id(0)
    j = pl.program_id(1)
    nc = pl.num_programs(1) * SWEEP_CHUNKS

    @pl.when(j == 0)
    def _():
        c_sc[...] = jnp.zeros_like(c_sc)
        n_sc[...] = jnp.zeros_like(n_sc)
        for st in range(N_DIRS * N_HEADS):
            m_sc[st] = jnp.float32(0.0)

    per_chunk = N_DIRS * 2 * N_HEADS
    dirs = ((qf_ref, ktf_ref, vf_ref, colsf_ref, rowsf_ref, hf_ref),
            (qb_ref, ktb_ref, vb_ref, colsb_ref, rowsb_ref, hb_ref))
    units = [(d, h) for h in range(N_HEADS) for d in range(N_DIRS)]
    t_idx = lax.broadcasted_iota(jnp.int32, (CHUNK, CHUNK), 0)
    s_idx = lax.broadcasted_iota(jnp.int32, (CHUNK, CHUNK), 1)
    causal = (s_idx <= t_idx, s_idx >= t_idx)

    def one_chunk(sub, _):
        off = (sub, SWEEP_CHUNKS - 1 - sub)
        chunk_f = j * SWEEP_CHUNKS + sub
        base = ((b * nc + chunk_f) * per_chunk,
                (b * nc + (nc - 1 - chunk_f)) * per_chunk + 2 * N_HEADS)
        for g in range(0, len(units), SWEEP_GROUP):
            states = {}
            for d, h in units[g:g + SWEEP_GROUP]:
                st = d * N_HEADS + h
                q_ref, kt_ref, v_ref, _, rows_ref, _ = dirs[d]
                carry, states[d, h] = _sweep_state_phase(
                    h, off[d], (c_sc[st], n_sc[st], m_sc[st]), q_ref, kt_ref, v_ref, rows_ref,
                    sc_ref, base[d])
                c_sc[st], n_sc[st], m_sc[st] = carry
            for d, h in units[g:g + SWEEP_GROUP]:
                _sweep_output_phase(h, off[d], states[d, h], dirs[d][2], dirs[d][3], causal[d],
                                    dirs[d][5])

    lax.fori_loop(0, SWEEP_CHUNKS, one_chunk, None)


def _combine_kernel(hf_ref, hb_ref, c_ref, szm_ref, pxp_ref, px_ref, pxn_ref, szp_ref, x_ref,
                    pmat_ref, poolw_ref, mhw_ref, skipw_ref, wout_ref, gout_ref,
                    o_ref):
    i = pl.program_id(1)
    n_tiles = pl.num_programs(1)
    tile = OUT_TILE
    seq_last = n_tiles * tile - 1

    zero_halo = jnp.zeros((HALO, D_POOL), BF16)
    px_main = px_ref[0]
    ext = jnp.concatenate(
        [jnp.where(i == 0, zero_halo, pxp_ref[0]), px_main,
         jnp.where(i == n_tiles - 1, zero_halo, pxn_ref[0]),
         jnp.zeros((CHUNK - 2 * HALO, D_POOL), BF16)], axis=0)

    for r0 in range(0, tile, OUT_ROWS):
        rs = slice(r0, r0 + OUT_ROWS)

        t_glob = i * tile + r0 + lax.broadcasted_iota(jnp.int32, (OUT_ROWS, 1), 0)
        y_p_parts = []
        for g, w in enumerate(POOL_WINDOWS):
            gs = slice(g * POOL_GROUP_DIM, (g + 1) * POOL_GROUP_DIM)
            left = (w - 1) // 2
            right = w - 1 - left
            count = jnp.minimum(t_glob + right, seq_last) - jnp.maximum(t_glob - left, 0) + 1
            total = jnp.concatenate(
                [jnp.dot(pmat_ref[g], ext[r0 + ch * CHUNK:r0 + (ch + 2) * CHUNK, gs],
                         preferred_element_type=F32) for ch in range(OUT_ROWS // CHUNK)], axis=0)
            pooled = total / count.astype(F32) - px_main[rs, gs].astype(F32)
            mixed = jnp.dot(pooled.astype(BF16), poolw_ref[g], preferred_element_type=F32)
            y_p_parts.append(mixed.astype(BF16) * szp_ref[0, rs, gs])
        y_p = jnp.concatenate(y_p_parts, axis=1)
        hres = x_ref[0, rs, :] + jnp.dot(y_p, wout_ref[0:D_POOL, :], preferred_element_type=F32)

        y_m_parts = []
        for h in range(N_HEADS):
            hs = slice(h * HEAD_DIM, (h + 1) * HEAD_DIM)
            ht = hf_ref[0, rs, hs].astype(F32) + hb_ref[0, rs, hs].astype(F32)
            mu = jnp.mean(ht, axis=-1, keepdims=True)
            dlt = ht - mu
            var = jnp.mean(dlt * dlt, axis=-1, keepdims=True)
            hn = (dlt * lax.rsqrt(var + EPS) * mhw_ref[:, hs]).astype(BF16)
            y_m_parts.append((hn + skipw_ref[:, hs] * c_ref[0, rs, hs]) * szm_ref[0, rs, hs])
        y_m = jnp.concatenate(y_m_parts, axis=1)
        hres = hres + jnp.dot(y_m, wout_ref[D_POOL:, :], preferred_element_type=F32)
        ms = jnp.mean(hres * hres, axis=-1, keepdims=True)
        o_ref[0, rs, :] = hres * lax.rsqrt(ms + EPS) * gout_ref[...]


def _block_diag_tiles(w):
    rows = w.reshape(N_HEADS, HEAD_DIM, QKV_BLOCK)
    col = np.arange(HEAD_DIM)
    spread = jnp.asarray((col[None, :] % QKV_BLOCK == np.arange(QKV_BLOCK)[:, None]), w.dtype)
    tiled = jnp.einsum('tro,oc->trc', rows, spread, precision=lax.Precision.HIGHEST)
    same_block = jnp.asarray(col[:, None] // QKV_BLOCK == col[None, :] // QKV_BLOCK)
    return jnp.where(same_block[None], tiled, 0.0)


def _gate_weights(w_gates, b_gates, bq, bk, bv):
    n_gates = N_DIRS * 2 * N_HEADS
    place = np.zeros((n_gates, N_DIRS * DIR_LANES), np.float32)
    for d in range(N_DIRS):
        for g in range(2 * N_HEADS):
            lane = (I_LANE + g) if g < N_HEADS else (F_LANE + g - N_HEADS)
            place[d * 2 * N_HEADS + g, d * DIR_LANES + lane] = 1.0
    place = jnp.asarray(place)
    hi = lax.Precision.HIGHEST
    rows = jnp.transpose(w_gates, (1, 0, 2)).reshape(3, N_HEADS, HEAD_DIM, n_gates)
    w_c = jnp.einsum('trc,tcg->trg', bq, rows[0], precision=hi) \
        + jnp.einsum('trc,tcg->trg', bk, rows[1], precision=hi)
    w_v = jnp.einsum('trc,tcg->trg', bv, rows[2], precision=hi)
    folded = jnp.stack([w_c, w_v]).reshape(2, D_MLSTM, n_gates)
    return (jnp.einsum('krg,gl->krl', folded, place, precision=hi),
            jnp.dot(b_gates.reshape(1, n_gates), place, precision=hi))


def _pool_band_matrices():
    t = np.arange(CHUNK)[:, None]
    r = np.arange(2 * CHUNK)[None, :] - HALO
    mats = []
    for w in POOL_WINDOWS:
        left = (w - 1) // 2
        right = w - 1 - left
        mats.append(((r >= t - left) & (r <= t + right)).astype(np.float32))
    return jnp.asarray(np.stack(mats), dtype=BF16)


def _conv_shift_matrix():
    t = np.arange(CHUNK)[:, None]
    r = np.arange(2 * CHUNK)[None, :] - HALO
    blocks = [r == t + (tap - CONV_WIDTH // 2) for tap in CONV_SIDE_TAPS]
    return jnp.asarray(np.concatenate(blocks, axis=0).astype(np.float32), dtype=BF16)


def _full(shape):
    return pl.BlockSpec(shape, lambda b, j: (0,) * len(shape))


def _halo_specs(tile, n_halo, col):
    per = tile // HALO
    prev = pl.BlockSpec((1, HALO, D_MODEL), lambda b, i: (b, jnp.maximum(i * per - 1, 0), col))
    nxt = pl.BlockSpec((1, HALO, D_MODEL),
                       lambda b, i: (b, jnp.minimum((i + 1) * per, n_halo - 1), col))
    return prev, nxt


def kernel(x, norm_in_g, w_in, pool_w, pool_scale, conv_w, conv_b, w_q, w_k, w_v, w_gates,
           b_gates, mh_norm_w, skip_w, w_out, norm_out_g):
    B, S, D = x.shape
    assert D == D_MODEL and (B * S) % IN_TILE == 0
    assert S % PREP_TILE == 0 and S % OUT_TILE == 0 and PREP_TILE % CHUNK == 0
    assert (PREP_TILE // CHUNK) * GROUP_LANES <= DIR_LANES and CONV_WIDTH // 2 <= HALO
    assert norm_in_g.shape[0] == 1, "single-layer block"
    nc = S // CHUNK
    tokens = B * S
    n_halo = S // HALO
    arb2 = pltpu.CompilerParams(dimension_semantics=("arbitrary", "arbitrary"),
                                vmem_limit_bytes=VMEM_LIMIT)

    proj = pl.pallas_call(
        _inproj_kernel,
        grid=(tokens // IN_TILE,),
        in_specs=[pl.BlockSpec((IN_TILE, D), lambda i: (i, 0)),
                  pl.BlockSpec((1, D), lambda i: (0, 0)),
                  pl.BlockSpec((D, 4 * D), lambda i: (0, 0), pipeline_mode=pl.Buffered(1))],
        out_specs=pl.BlockSpec((IN_TILE, 4 * D), lambda i: (i, 0)),
        out_shape=jax.ShapeDtypeStruct((tokens, 4 * D), BF16),
        compiler_params=pltpu.CompilerParams(dimension_semantics=("arbitrary",),
                                             vmem_limit_bytes=VMEM_LIMIT),
        name="inproj",
    )(x.reshape(tokens, D), norm_in_g[0][None, :], w_in[0].astype(BF16))
    proj = proj.reshape(B, S, 4 * D)

    bq, bk, bv = (_block_diag_tiles(w[0]) for w in (w_q, w_k, w_v))
    wg, bg = _gate_weights(w_gates[0], b_gates[0], bq, bk, bv)
    wq_t, wk_t, wv_t = bq.astype(BF16), (bk * float(HEAD_DIM) ** -0.5).astype(BF16), bv.astype(BF16)
    conv_w8 = jnp.pad(conv_w[0], ((0, 8 - CONV_WIDTH), (0, 0)))

    mx_prev, mx_next = _halo_specs(PREP_TILE, n_halo, 2)
    prep_seq = pl.BlockSpec((1, PREP_TILE, D), lambda b, i: (b, i, 0))
    prep_cols = pl.BlockSpec((1, PREP_TILE, DIR_LANES), lambda b, i: (b, i, 0))
    cpt = PREP_TILE // CHUNK
    prep_rows = pl.BlockSpec((1, cpt, SCAN_ROWS, CHUNK), lambda b, i: (b, i, 0, 0))
    seq_bf = jax.ShapeDtypeStruct((B, S, D), BF16)
    cols_shape = jax.ShapeDtypeStruct((B, S, DIR_LANES), F32)
    rows_shape = jax.ShapeDtypeStruct((B, nc, SCAN_ROWS, CHUNK), F32)
    prep_kt = pl.BlockSpec((1, cpt, D, CHUNK), lambda b, i: (b, i, 0, 0))
    kt_shape = jax.ShapeDtypeStruct((B, nc, D, CHUNK), BF16)
    q, kt, v, c, cols_f, cols_b, rows_f, rows_b = pl.pallas_call(
        _prep_kernel,
        grid=(B, S // PREP_TILE),
        in_specs=[mx_prev, pl.BlockSpec((1, PREP_TILE, D), lambda b, i: (b, i, 2)), mx_next,
                  _full((len(CONV_SIDE_TAPS) * CHUNK, 2 * CHUNK)),
                  _full((8, D)), _full((1, D)),
                  _full((N_HEADS, HEAD_DIM, HEAD_DIM)), _full((N_HEADS, HEAD_DIM, HEAD_DIM)),
                  _full((N_HEADS, HEAD_DIM, HEAD_DIM)),
                  _full((2, D, N_DIRS * DIR_LANES)), _full((1, N_DIRS * DIR_LANES))],
        out_specs=[prep_seq, prep_kt, prep_seq, prep_seq,
                   prep_cols, prep_cols, prep_rows, prep_rows],
        out_shape=[seq_bf, kt_shape, seq_bf, seq_bf,
                   cols_shape, cols_shape, rows_shape, rows_shape],
        compiler_params=arb2,
        name="prep",
    )(proj, proj, proj, _conv_shift_matrix(), conv_w8, conv_b[0][None, :], wq_t, wk_t, wv_t,
      wg.astype(BF16), bg)

    def last(rows, lane):
        return jnp.concatenate([rows[:, :, F_LANE:F_LANE + N_HEADS, lane],
                                rows[:, :, CM_LANE:CM_LANE + N_HEADS, lane]], axis=-1)

    chunk_scalars = jnp.concatenate([last(rows_f, CHUNK - 1), last(rows_b, 0)], axis=-1).reshape(-1)

    sweep_steps = nc // SWEEP_CHUNKS
    sweep_rows = SWEEP_CHUNKS * CHUNK

    def sweep_specs(idx):
        seq = pl.BlockSpec((1, sweep_rows, D), lambda b, j: (b, idx(j), 0))
        cols = pl.BlockSpec((1, sweep_rows, DIR_LANES), lambda b, j: (b, idx(j), 0))
        rows = pl.BlockSpec((1, SWEEP_CHUNKS, 8, CHUNK), lambda b, j: (b, idx(j), 0, 0))
        kt_spec = pl.BlockSpec((1, SWEEP_CHUNKS, D, CHUNK), lambda b, j: (b, idx(j), 0, 0))
        return seq, [seq, kt_spec, seq, cols, rows]

    seq_f, in_f = sweep_specs(lambda j: j)
    seq_b, in_b = sweep_specs(lambda j: sweep_steps - 1 - j)
    n_state = N_DIRS * N_HEADS
    h_fwd, h_bwd = pl.pallas_call(
        _sweep_kernel,
        grid=(B, sweep_steps),
        in_specs=[pl.BlockSpec(memory_space=pltpu.SMEM)] + in_f + in_b,
        out_specs=[seq_f, seq_b],
        out_shape=[seq_bf, seq_bf],
        scratch_shapes=[pltpu.VMEM((n_state, HEAD_DIM, HEAD_DIM), F32),
                        pltpu.VMEM((n_state, 8, HEAD_DIM), F32),
                        pltpu.SMEM((n_state,), F32)],
        compiler_params=arb2,
        name="sweep",
    )(chunk_scalars, q, kt, v, cols_f, rows_f, q, kt, v, cols_b, rows_b)

    px_prev, px_next = _halo_specs(OUT_TILE, n_halo, 0)
    pool_w_scaled = pool_w[0] * pool_scale[0].reshape(len(POOL_WINDOWS), 1, POOL_GROUP_DIM)

    def out_col(col):
        return pl.BlockSpec((1, OUT_TILE, D), lambda b, i: (b, i, col))

    out_seq = out_col(0)
    out = pl.pallas_call(
        _combine_kernel,
        grid=(B, S // OUT_TILE),
        in_specs=[out_seq, out_seq, out_seq, out_col(3),
                  px_prev, out_col(0), px_next, out_col(1), out_seq,
                  _full((len(POOL_WINDOWS), CHUNK, 2 * CHUNK)),
                  _full((len(POOL_WINDOWS), POOL_GROUP_DIM, POOL_GROUP_DIM)),
                  _full((1, D)), _full((1, D)),
                  _full((2 * D, D)), _full((1, D))],
        out_specs=out_seq,
        out_shape=jax.ShapeDtypeStruct((B, S, D), F32),
        compiler_params=arb2,
        name="combine",
    )(h_fwd, h_bwd, c, proj, proj, proj, proj, proj, x,
      _pool_band_matrices(), pool_w_scaled.astype(BF16),
      mh_norm_w[0][None, :], skip_w[0][None, :].astype(BF16), w_out[0].astype(BF16),
      norm_out_g[None, :])
    return out
```

```python
import numpy as np
import jax
import jax.numpy as jnp
from jax import lax
from jax.experimental import pallas as pl
from jax.experimental.pallas import tpu as pltpu

D_MODEL = 1024
D_POOL = 1024
D_MLSTM = 1024
POOL_WINDOWS = (2, 4, 8, 16)
POOL_GROUP_DIM = D_POOL // len(POOL_WINDOWS)
N_HEADS = 4
HEAD_DIM = 256
QKV_BLOCK = 4
CONV_WIDTH = 5
CHUNK = 128
N_DIRS = 2
EPS = 1e-6
LOG2E = 1.4426950408889634

X_GROUPS, Z_GROUPS = (0, 2), (1, 3)
HALO = 16
DIR_LANES = 128
GROUP_LANES = 16
I_LANE, CM_LANE, F_LANE = 0, 4, 8
SCAN_ROWS = 16
CONV_SIDE_TAPS = (0, 1, 3, 4)
FRONT_TILE = 512
OUT_TILE = 512
OUT_ROWS = 256
SWEEP_GROUP = 4
SWEEP_CHUNKS = 4
VMEM_LIMIT = 48 * 1024 * 1024

F32 = jnp.float32
BF16 = jnp.bfloat16


def _silu(z):
    return z * (1.0 / (1.0 + jnp.exp(-z)))


def _log_sigmoid(g):
    return jnp.minimum(g, 0.0) - jnp.log1p(jnp.exp(-jnp.abs(g)))


def _token_scan(x, op, reverse):
    t = lax.broadcasted_iota(jnp.int32, x.shape, 0)
    k = 1
    while k < CHUNK:
        if reverse:
            shifted = pltpu.roll(x, CHUNK - k, 0)
            valid = t < CHUNK - k
        else:
            shifted = pltpu.roll(x, k, 0)
            valid = t >= k
        x = jnp.where(valid, op(x, shifted), x)
        k *= 2
    return x


def _front_kernel(xp_ref, x_ref, xn_ref, gin_ref, win_ref, shift_ref, convw_ref, convb_ref,
                  wq_ref, wk_ref, wv_ref, wg_ref, bg_ref,
                  px_ref, szp_ref, szm_ref, q_ref, kt_ref, v_ref, c_ref,
                  colsf_ref, colsb_ref, rowsf_ref, rowsb_ref):
    i = pl.program_id(1)
    n_tiles = pl.num_programs(1)
    tile = FRONT_TILE
    n_chunks = tile // CHUNK

    x_ext = jnp.concatenate([xp_ref[0], x_ref[0], xn_ref[0]], axis=0)
    ms = jnp.mean(x_ext * x_ext, axis=-1, keepdims=True)
    u_ext = (x_ext * lax.rsqrt(ms + EPS) * gin_ref[...]).astype(BF16)
    u = u_ext[HALO:HALO + tile, :]

    def project(lhs, group):
        return jnp.dot(lhs, win_ref[:, group * D_MODEL:(group + 1) * D_MODEL],
                       preferred_element_type=F32)

    mx_ext = project(u_ext, 2).astype(BF16)
    mx_bf = mx_ext[HALO:HALO + tile, :]
    zero_halo = jnp.zeros((HALO, D_MLSTM), BF16)
    ext = jnp.concatenate(
        [jnp.where(i == 0, zero_halo, mx_ext[0:HALO, :]), mx_bf,
         jnp.where(i == n_tiles - 1, zero_halo, mx_ext[HALO + tile:, :]),
         jnp.zeros((CHUNK - 2 * HALO, D_MLSTM), BF16)], axis=0)

    pad = CONV_WIDTH // 2
    conv_parts = []
    for ch in range(n_chunks):
        rs = slice(ch * CHUNK, (ch + 1) * CHUNK)
        shifted = jnp.dot(shift_ref[...], ext[ch * CHUNK:(ch + 2) * CHUNK, :],
                          preferred_element_type=F32)
        part = convb_ref[...] + mx_bf[rs, :].astype(F32) * convw_ref[pad:pad + 1, :]
        for n, tap in enumerate(CONV_SIDE_TAPS):
            part = part + shifted[n * CHUNK:(n + 1) * CHUNK, :] * convw_ref[tap:tap + 1, :]
        conv_parts.append(part)
    c_bf = _silu(jnp.concatenate(conv_parts, axis=0)).astype(BF16)
    c_ref[0] = c_bf
    szp_ref[0] = _silu(project(u, 1)).astype(BF16)

    gates = bg_ref[...] + jnp.dot(c_bf, wg_ref[0], preferred_element_type=F32) \
        + jnp.dot(mx_bf, wg_ref[1], preferred_element_type=F32)

    sub = lax.broadcasted_iota(jnp.int32, (CHUNK, DIR_LANES), 1) % GROUP_LANES
    for d, (cols_ref, rows_ref) in enumerate(((colsf_ref, rowsf_ref), (colsb_ref, rowsb_ref))):
        ds = slice(d * DIR_LANES, (d + 1) * DIR_LANES)
        packed = gates[0:CHUNK, ds]
        for ch in range(1, n_chunks):
            packed = packed + pltpu.roll(gates[ch * CHUNK:(ch + 1) * CHUNK, ds], ch * GROUP_LANES, 1)
        b = _token_scan(_log_sigmoid(packed), jnp.add, reverse=(d == 1))
        a = packed - pltpu.roll(b, DIR_LANES - (F_LANE - I_LANE), 1)
        cm = _token_scan(a, jnp.maximum, reverse=(d == 1))
        scan = LOG2E * jnp.where(sub < CM_LANE, a,
                                 jnp.where(sub < F_LANE, pltpu.roll(cm, CM_LANE - I_LANE, 1), b))
        for ch in range(n_chunks):
            cols = scan if ch == 0 else pltpu.roll(scan, DIR_LANES - ch * GROUP_LANES, 1)
            cols_ref[0, ch * CHUNK:(ch + 1) * CHUNK, :] = cols
            rows_ref[0, ch] = cols.T[0:SCAN_ROWS, :]
    szm_ref[0] = _silu(project(u, 3)).astype(BF16)

    for h in range(N_HEADS):
        hs = slice(h * HEAD_DIM, (h + 1) * HEAD_DIM)
        q_ref[0, :, hs] = jnp.dot(c_bf[:, hs], wq_ref[h], preferred_element_type=F32).astype(BF16)
        v_ref[0, :, hs] = jnp.dot(mx_bf[:, hs], wv_ref[h], preferred_element_type=F32).astype(BF16)
        kh = jnp.dot(c_bf[:, hs], wk_ref[h], preferred_element_type=F32)
        for ch in range(n_chunks):
            kt_ref[0, ch, hs, :] = kh[ch * CHUNK:(ch + 1) * CHUNK, :].T.astype(BF16)

    px_ref[0] = project(u, 0).astype(BF16)


def _lane_bcast(tile, lane):
    return jnp.broadcast_to(tile[:, lane:lane + 1], tile.shape)


def _sweep_state_phase(h, off, carry, q_ref, kt_ref, v_ref, rows_ref, sc_ref, sc_base):
    hs = slice(h * HEAD_DIM, (h + 1) * HEAD_DIM)
    rs = pl.ds(pl.multiple_of(off * CHUNK, CHUNK), CHUNK)
    c_old, n_old, m = carry
    b_last = sc_ref[sc_base + h]
    cm_last = sc_ref[sc_base + N_HEADS + h]
    m_last = jnp.maximum(m, cm_last)

    a_row = rows_ref[0, off, I_LANE + h:I_LANE + h + 1, :]
    ws_row = jnp.exp2(a_row - m_last)
    decay = jnp.exp2(jnp.full((1, HEAD_DIM), m - m_last, F32))

    qh, vh = q_ref[0, rs, hs], v_ref[0, rs, hs]
    kt = kt_ref[0, off, hs, :]
    q_c = jnp.dot(qh, c_old.astype(BF16), preferred_element_type=F32)
    qk = jnp.dot(qh, kt, preferred_element_type=F32)
    q_n = qh.astype(F32) * n_old[0:1, :]
    q_n = q_n[:, :CHUNK] + q_n[:, CHUNK:]

    ws_bf = ws_row.astype(BF16)
    kv = jnp.dot(kt * ws_bf, vh, preferred_element_type=F32)
    ws8 = jnp.broadcast_to(ws_bf, (8, CHUNK))
    kn = lax.dot_general(ws8, kt, (((1,), (1,)), ((), ())), preferred_element_type=F32)
    new_carry = (decay * c_old + kv, decay * n_old + kn, b_last + m_last)
    return new_carry, (m, a_row, q_c, qk, q_n)


def _sweep_output_phase(h, off, state, v_ref, cols_ref, causal, h_ref):
    hs = slice(h * HEAD_DIM, (h + 1) * HEAD_DIM)
    rs = pl.ds(pl.multiple_of(off * CHUNK, CHUNK), CHUNK)
    cols = cols_ref[0, rs, :]
    m, a_row, q_c, qk, q_n = state
    big_m = jnp.maximum(_lane_bcast(cols, CM_LANE + h), m)
    dmat = jnp.where(causal, jnp.exp2(a_row - big_m), 0.0)
    inter_w = jnp.exp2(m - big_m)
    exp_neg_mt = jnp.exp2(-(_lane_bcast(cols, F_LANE + h) + big_m))
    s = qk * dmat
    den = jnp.sum(s + inter_w * q_n, axis=-1, keepdims=True)
    inv = 1.0 / jnp.maximum(jnp.abs(den), exp_neg_mt[:, 0:1])
    inv_b = jnp.broadcast_to(inv, (CHUNK, CHUNK))
    num = jnp.dot(s.astype(BF16), v_ref[0, rs, hs], preferred_element_type=F32)
    out = (num + jnp.concatenate([inter_w, inter_w], axis=1) * q_c) \
        * jnp.concatenate([inv_b, inv_b], axis=1)
    h_ref[0, rs, hs] = out.astype(h_ref.dtype)


def _sweep_kernel(sc_ref,
                  qf_ref, ktf_ref, vf_ref, colsf_ref, rowsf_ref,
                  qb_ref, ktb_ref, vb_ref, colsb_ref, rowsb_ref,
                  hf_ref, hb_ref,
                  c_sc, n_sc, m_sc):
    b = pl.program_id(0)
    j = pl.program_id(1)
    nc = pl.num_programs(1) * SWEEP_CHUNKS

    @pl.when(j == 0)
    def _():
        c_sc[...] = jnp.zeros_like(c_sc)
        n_sc[...] = jnp.zeros_like(n_sc)
        for st in range(N_DIRS * N_HEADS):
            m_sc[st] = jnp.float32(0.0)

    per_chunk = N_DIRS * 2 * N_HEADS
    dirs = ((qf_ref, ktf_ref, vf_ref, colsf_ref, rowsf_ref, hf_ref),
            (qb_ref, ktb_ref, vb_ref, colsb_ref, rowsb_ref, hb_ref))
    units = [(d, h) for h in range(N_HEADS) for d in range(N_DIRS)]
    t_idx = lax.broadcasted_iota(jnp.int32, (CHUNK, CHUNK), 0)
    s_idx = lax.broadcasted_iota(jnp.int32, (CHUNK, CHUNK), 1)
    causal = (s_idx <= t_idx, s_idx >= t_idx)

    def one_chunk(sub, _):
        off = (sub, SWEEP_CHUNKS - 1 - sub)
        chunk_f = j * SWEEP_CHUNKS + sub
        base = ((b * nc + chunk_f) * per_chunk,
                (b * nc + (nc - 1 - chunk_f)) * per_chunk + 2 * N_HEADS)
        for g in range(0, len(units), SWEEP_GROUP):
            states = {}
            for d, h in units[g:g + SWEEP_GROUP]:
                st = d * N_HEADS + h
                q_ref, kt_ref, v_ref, _, rows_ref, _ = dirs[d]
                carry, states[d, h] = _sweep_state_phase(
                    h, off[d], (c_sc[st], n_sc[st], m_sc[st]), q_ref, kt_ref, v_ref, rows_ref,
                    sc_ref, base[d])
                c_sc[st], n_sc[st], m_sc[st] = carry
            for d, h in units[g:g + SWEEP_GROUP]:
                _sweep_output_phase(h, off[d], states[d, h], dirs[d][2], dirs[d][3], causal[d],
                                    dirs[d][5])

    lax.fori_loop(0, SWEEP_CHUNKS, one_chunk, None)


def _combine_kernel(hf_ref, hb_ref, c_ref, szm_ref, pxp_ref, px_ref, pxn_ref, szp_ref, x_ref,
                    pmat_ref, poolw_ref, mhw_ref, skipw_ref, wout_ref, gout_ref,
                    o_ref):
    i = pl.program_id(1)
    n_tiles = pl.num_programs(1)
    tile = OUT_TILE
    seq_last = n_tiles * tile - 1

    zero_halo = jnp.zeros((HALO, D_POOL), BF16)
    px_main = px_ref[0]
    ext = jnp.concatenate(
        [jnp.where(i == 0, zero_halo, pxp_ref[0]), px_main,
         jnp.where(i == n_tiles - 1, zero_halo, pxn_ref[0]),
         jnp.zeros((CHUNK - 2 * HALO, D_POOL), BF16)], axis=0)

    def branches(r0):
        rs = slice(r0, r0 + OUT_ROWS)

        t_glob = i * tile + r0 + lax.broadcasted_iota(jnp.int32, (OUT_ROWS, 1), 0)
        y_p_parts = []
        for g, w in enumerate(POOL_WINDOWS):
            gs = slice(g * POOL_GROUP_DIM, (g + 1) * POOL_GROUP_DIM)
            left = (w - 1) // 2
            right = w - 1 - left
            count = jnp.minimum(t_glob + right, seq_last) - jnp.maximum(t_glob - left, 0) + 1
            total = jnp.concatenate(
                [jnp.dot(pmat_ref[g], ext[r0 + ch * CHUNK:r0 + (ch + 2) * CHUNK, gs],
                         preferred_element_type=F32) for ch in range(OUT_ROWS // CHUNK)], axis=0)
            pooled = total / count.astype(F32) - px_main[rs, gs].astype(F32)
            mixed = jnp.dot(pooled.astype(BF16), poolw_ref[g], preferred_element_type=F32)
            y_p_parts.append(mixed.astype(BF16) * szp_ref[0, rs, gs])
        y_p = jnp.concatenate(y_p_parts, axis=1)

        y_m_parts = []
        for h in range(N_HEADS):
            hs = slice(h * HEAD_DIM, (h + 1) * HEAD_DIM)
            ht = hf_ref[0, rs, hs].astype(F32) + hb_ref[0, rs, hs].astype(F32)
            mu = jnp.mean(ht, axis=-1, keepdims=True)
            dlt = ht - mu
            var = jnp.mean(dlt * dlt, axis=-1, keepdims=True)
            hn = (dlt * lax.rsqrt(var + EPS) * mhw_ref[:, hs]).astype(BF16)
            y_m_parts.append((hn + skipw_ref[:, hs] * c_ref[0, rs, hs]) * szm_ref[0, rs, hs])
        return y_p, jnp.concatenate(y_m_parts, axis=1)

    def project(r0, y_p, y_m):
        rs = slice(r0, r0 + OUT_ROWS)
        hres = x_ref[0, rs, :] + jnp.dot(y_p, wout_ref[0:D_POOL, :], preferred_element_type=F32) \
            + jnp.dot(y_m, wout_ref[D_POOL:, :], preferred_element_type=F32)
        ms = jnp.mean(hres * hres, axis=-1, keepdims=True)
        o_ref[0, rs, :] = hres * lax.rsqrt(ms + EPS) * gout_ref[...]

    starts = list(range(0, tile, OUT_ROWS))
    pending = branches(starts[0])
    for prev, nxt in zip(starts[:-1], starts[1:]):
        upcoming = branches(nxt)
        project(prev, *pending)
        pending = upcoming
    project(starts[-1], *pending)


def _block_diag_tiles(w):
    rows = w.reshape(N_HEADS, HEAD_DIM, QKV_BLOCK)
    col = np.arange(HEAD_DIM)
    spread = jnp.asarray((col[None, :] % QKV_BLOCK == np.arange(QKV_BLOCK)[:, None]), w.dtype)
    tiled = jnp.einsum('tro,oc->trc', rows, spread, precision=lax.Precision.HIGHEST)
    same_block = jnp.asarray(col[:, None] // QKV_BLOCK == col[None, :] // QKV_BLOCK)
    return jnp.where(same_block[None], tiled, 0.0)


def _gate_weights(w_gates, b_gates, bq, bk, bv):
    n_gates = N_DIRS * 2 * N_HEADS
    place = np.zeros((n_gates, N_DIRS * DIR_LANES), np.float32)
    for d in range(N_DIRS):
        for g in range(2 * N_HEADS):
            lane = (I_LANE + g) if g < N_HEADS else (F_LANE + g - N_HEADS)
            place[d * 2 * N_HEADS + g, d * DIR_LANES + lane] = 1.0
    place = jnp.asarray(place)
    hi = lax.Precision.HIGHEST
    rows = jnp.transpose(w_gates, (1, 0, 2)).reshape(3, N_HEADS, HEAD_DIM, n_gates)
    w_c = jnp.einsum('trc,tcg->trg', bq, rows[0], precision=hi) \
        + jnp.einsum('trc,tcg->trg', bk, rows[1], precision=hi)
    w_v = jnp.einsum('trc,tcg->trg', bv, rows[2], precision=hi)
    folded = jnp.stack([w_c, w_v]).reshape(2, D_MLSTM, n_gates)
    return (jnp.einsum('krg,gl->krl', folded, place, precision=hi),
            jnp.dot(b_gates.reshape(1, n_gates), place, precision=hi))


def _pool_band_matrices():
    t = np.arange(CHUNK)[:, None]
    r = np.arange(2 * CHUNK)[None, :] - HALO
    mats = []
    for w in POOL_WINDOWS:
        left = (w - 1) // 2
        right = w - 1 - left
        mats.append(((r >= t - left) & (r <= t + right)).astype(np.float32))
    return jnp.asarray(np.stack(mats), dtype=BF16)


def _conv_shift_matrix():
    t = np.arange(CHUNK)[:, None]
    r = np.arange(2 * CHUNK)[None, :] - HALO
    blocks = [r == t + (tap - CONV_WIDTH // 2) for tap in CONV_SIDE_TAPS]
    return jnp.asarray(np.concatenate(blocks, axis=0).astype(np.float32), dtype=BF16)


def _full(shape):
    return pl.BlockSpec(shape, lambda b, j: (0,) * len(shape))


def _halo_specs(tile, n_halo, col):
    per = tile // HALO
    prev = pl.BlockSpec((1, HALO, D_MODEL), lambda b, i: (b, jnp.maximum(i * per - 1, 0), col))
    nxt = pl.BlockSpec((1, HALO, D_MODEL),
                       lambda b, i: (b, jnp.minimum((i + 1) * per, n_halo - 1), col))
    return prev, nxt


def kernel(x, norm_in_g, w_in, pool_w, pool_scale, conv_w, conv_b, w_q, w_k, w_v, w_gates,
           b_gates, mh_norm_w, skip_w, w_out, norm_out_g):
    B, S, D = x.shape
    assert D == D_MODEL and S % FRONT_TILE == 0 and S % OUT_TILE == 0 and FRONT_TILE % CHUNK == 0
    assert (FRONT_TILE // CHUNK) * GROUP_LANES <= DIR_LANES and CONV_WIDTH // 2 <= HALO
    assert norm_in_g.shape[0] == 1, "single-layer block"
    nc = S // CHUNK
    n_halo = S // HALO
    arb2 = pltpu.CompilerParams(dimension_semantics=("arbitrary", "arbitrary"),
                                vmem_limit_bytes=VMEM_LIMIT)

    bq, bk, bv = (_block_diag_tiles(w[0]) for w in (w_q, w_k, w_v))
    wg, bg = _gate_weights(w_gates[0], b_gates[0], bq, bk, bv)
    wq_t, wk_t, wv_t = bq.astype(BF16), (bk * float(HEAD_DIM) ** -0.5).astype(BF16), bv.astype(BF16)
    conv_w8 = jnp.pad(conv_w[0], ((0, 8 - CONV_WIDTH), (0, 0)))

    def const(shape):
        return pl.BlockSpec(shape, lambda b, i: (0,) * len(shape), pipeline_mode=pl.Buffered(1))

    x_prev, x_next = _halo_specs(FRONT_TILE, n_halo, 0)
    front_seq = pl.BlockSpec((1, FRONT_TILE, D), lambda b, i: (b, i, 0))
    front_cols = pl.BlockSpec((1, FRONT_TILE, DIR_LANES), lambda b, i: (b, i, 0))
    cpt = FRONT_TILE // CHUNK
    front_rows = pl.BlockSpec((1, cpt, SCAN_ROWS, CHUNK), lambda b, i: (b, i, 0, 0))
    front_kt = pl.BlockSpec((1, cpt, D, CHUNK), lambda b, i: (b, i, 0, 0))
    seq_bf = jax.ShapeDtypeStruct((B, S, D), BF16)
    cols_shape = jax.ShapeDtypeStruct((B, S, DIR_LANES), F32)
    rows_shape = jax.ShapeDtypeStruct((B, nc, SCAN_ROWS, CHUNK), F32)
    kt_shape = jax.ShapeDtypeStruct((B, nc, D, CHUNK), BF16)
    pool_x, silu_pz, silu_mz, q, kt, v, c, cols_f, cols_b, rows_f, rows_b = pl.pallas_call(
        _front_kernel,
        grid=(B, S // FRONT_TILE),
        in_specs=[x_prev, front_seq, x_next, const((1, D)), const((D, 4 * D)),
                  const((len(CONV_SIDE_TAPS) * CHUNK, 2 * CHUNK)),
                  const((8, D)), const((1, D)),
                  const((N_HEADS, HEAD_DIM, HEAD_DIM)), const((N_HEADS, HEAD_DIM, HEAD_DIM)),
                  const((N_HEADS, HEAD_DIM, HEAD_DIM)),
                  const((2, D, N_DIRS * DIR_LANES)), const((1, N_DIRS * DIR_LANES))],
        out_specs=[front_seq, front_seq, front_seq, front_seq, front_kt, front_seq, front_seq,
                   front_cols, front_cols, front_rows, front_rows],
        out_shape=[seq_bf, seq_bf, seq_bf, seq_bf, kt_shape, seq_bf, seq_bf,
                   cols_shape, cols_shape, rows_shape, rows_shape],
        compiler_params=arb2,
        name="front",
    )(x, x, x, norm_in_g[0][None, :], w_in[0].astype(BF16), _conv_shift_matrix(), conv_w8,
      conv_b[0][None, :], wq_t, wk_t, wv_t, wg.astype(BF16), bg)

    def last(rows, lane):
        return jnp.concatenate([rows[:, :, F_LANE:F_LANE + N_HEADS, lane],
                                rows[:, :, CM_LANE:CM_LANE + N_HEADS, lane]], axis=-1)

    chunk_scalars = jnp.concatenate([last(rows_f, CHUNK - 1), last(rows_b, 0)], axis=-1).reshape(-1)

    sweep_steps = nc // SWEEP_CHUNKS
    sweep_rows = SWEEP_CHUNKS * CHUNK

    def sweep_specs(idx):
        seq = pl.BlockSpec((1, sweep_rows, D), lambda b, j: (b, idx(j), 0))
        cols = pl.BlockSpec((1, sweep_rows, DIR_LANES), lambda b, j: (b, idx(j), 0))
        rows = pl.BlockSpec((1, SWEEP_CHUNKS, 8, CHUNK), lambda b, j: (b, idx(j), 0, 0))
        kt_spec = pl.BlockSpec((1, SWEEP_CHUNKS, D, CHUNK), lambda b, j: (b, idx(j), 0, 0))
        return seq, [seq, kt_spec, seq, cols, rows]

    seq_f, in_f = sweep_specs(lambda j: j)
    seq_b, in_b = sweep_specs(lambda j: sweep_steps - 1 - j)
    n_state = N_DIRS * N_HEADS
    h_fwd, h_bwd = pl.pallas_call(
        _sweep_kernel,
        grid=(B, sweep_steps),
        in_specs=[pl.BlockSpec(memory_space=pltpu.SMEM)] + in_f + in_b,
        out_specs=[seq_f, seq_b],
        out_shape=[seq_bf, seq_bf],
        scratch_shapes=[pltpu.VMEM((n_state, HEAD_DIM, HEAD_DIM), F32),
                        pltpu.VMEM((n_state, 8, HEAD_DIM), F32),
                        pltpu.SMEM((n_state,), F32)],
        compiler_params=arb2,
        name="sweep",
    )(chunk_scalars, q, kt, v, cols_f, rows_f, q, kt, v, cols_b, rows_b)

    px_prev, px_next = _halo_specs(OUT_TILE, n_halo, 0)
    pool_w_scaled = pool_w[0] * pool_scale[0].reshape(len(POOL_WINDOWS), 1, POOL_GROUP_DIM)

    out_seq = pl.BlockSpec((1, OUT_TILE, D), lambda b, i: (b, i, 0))
    out = pl.pallas_call(
        _combine_kernel,
        grid=(B, S // OUT_TILE),
        in_specs=[out_seq, out_seq, out_seq, out_seq,
                  px_prev, out_seq, px_next, out_seq, out_seq,
                  _full((len(POOL_WINDOWS), CHUNK, 2 * CHUNK)),
                  _full((len(POOL_WINDOWS), POOL_GROUP_DIM, POOL_GROUP_DIM)),
                  _full((1, D)), _full((1, D)),
                  _full((2 * D, D)), _full((1, D))],
        out_specs=out_seq,
        out_shape=jax.ShapeDtypeStruct((B, S, D), F32),
        compiler_params=arb2,
        name="combine",
    )(h_fwd, h_bwd, c, silu_mz, pool_x, pool_x, pool_x, silu_pz, x,
      _pool_band_matrices(), pool_w_scaled.astype(BF16),
      mh_norm_w[0][None, :], skip_w[0][None, :].astype(BF16), w_out[0].astype(BF16),
      norm_out_g[None, :])
    return out
```

```python
import numpy as np
import jax
import jax.numpy as jnp
from jax import lax
from jax.experimental import pallas as pl
from jax.experimental.pallas import tpu as pltpu

D_MODEL = 1024
D_POOL = 1024
D_MLSTM = 1024
POOL_WINDOWS = (2, 4, 8, 16)
POOL_GROUP_DIM = D_POOL // len(POOL_WINDOWS)
N_HEADS = 4
HEAD_DIM = 256
QKV_BLOCK = 4
CONV_WIDTH = 5
CHUNK = 128
N_DIRS = 2
EPS = 1e-6
LOG2E = 1.4426950408889634

HALO = 16
DIR_LANES = 128
GROUP_LANES = 16
I_LANE, CM_LANE, F_LANE = 0, 4, 8
SCAN_ROWS = 16
CONV_SIDE_TAPS = (0, 1, 3, 4)
FRONT_TILE = 512
OUT_TILE = 512
OUT_ROWS = 256
SWEEP_GROUP = 4
SWEEP_CHUNKS = 4
VMEM_LIMIT = 58 * 1024 * 1024

F32 = jnp.float32
BF16 = jnp.bfloat16


def _silu(z):
    return z * (1.0 / (1.0 + jnp.exp(-z)))


def _log_sigmoid(g):
    return jnp.minimum(g, 0.0) - jnp.log1p(jnp.exp(-jnp.abs(g)))


def _token_scan(x, op, reverse):
    t = lax.broadcasted_iota(jnp.int32, x.shape, 0)
    k = 1
    while k < CHUNK:
        if reverse:
            shifted = pltpu.roll(x, CHUNK - k, 0)
            valid = t < CHUNK - k
        else:
            shifted = pltpu.roll(x, k, 0)
            valid = t >= k
        x = jnp.where(valid, op(x, shifted), x)
        k *= 2
    return x


def _front_kernel(xp_ref, x_ref, xn_ref, gin_ref, win_ref, shift_ref, convw_ref, convb_ref,
                  wq_ref, wk_ref, wv_ref, wg_ref, bg_ref,
                  px_ref, szp_ref, szm_ref, q_ref, kt_ref, v_ref, c_ref,
                  colsf_ref, colsb_ref, rowsf_ref, rowsb_ref,
                  win_sc):
    i = pl.program_id(1)
    n_tiles = pl.num_programs(1)
    tile = FRONT_TILE
    n_chunks = tile // CHUNK

    @pl.when((pl.program_id(0) == 0) & (i == 0))
    def _():
        for group in range(4):
            gs = slice(group * D_MODEL, (group + 1) * D_MODEL)
            win_sc[:, gs] = win_ref[:, gs].astype(BF16)

    x_ext = jnp.concatenate([xp_ref[0], x_ref[0], xn_ref[0]], axis=0)
    ms = jnp.mean(x_ext * x_ext, axis=-1, keepdims=True)
    u_ext = (x_ext * lax.rsqrt(ms + EPS) * gin_ref[...]).astype(BF16)
    u = u_ext[HALO:HALO + tile, :]

    def project(lhs, group):
        return jnp.dot(lhs, win_sc[:, group * D_MODEL:(group + 1) * D_MODEL],
                       preferred_element_type=F32)

    mx_ext = project(u_ext, 2).astype(BF16)
    mx_bf = mx_ext[HALO:HALO + tile, :]
    zero_halo = jnp.zeros((HALO, D_MLSTM), BF16)
    ext = jnp.concatenate(
        [jnp.where(i == 0, zero_halo, mx_ext[0:HALO, :]), mx_bf,
         jnp.where(i == n_tiles - 1, zero_halo, mx_ext[HALO + tile:, :]),
         jnp.zeros((CHUNK - 2 * HALO, D_MLSTM), BF16)], axis=0)

    pad = CONV_WIDTH // 2
    conv_parts = []
    for ch in range(n_chunks):
        rs = slice(ch * CHUNK, (ch + 1) * CHUNK)
        shifted = jnp.dot(shift_ref[...], ext[ch * CHUNK:(ch + 2) * CHUNK, :],
                          preferred_element_type=F32)
        part = convb_ref[...] + mx_bf[rs, :].astype(F32) * convw_ref[pad:pad + 1, :]
        for n, tap in enumerate(CONV_SIDE_TAPS):
            part = part + shifted[n * CHUNK:(n + 1) * CHUNK, :] * convw_ref[tap:tap + 1, :]
        conv_parts.append(part)
    c_bf = _silu(jnp.concatenate(conv_parts, axis=0)).astype(BF16)
    c_ref[0] = c_bf
    szp_ref[0] = _silu(project(u, 1)).astype(BF16)

    gates = bg_ref[...] + jnp.dot(c_bf, wg_ref[0], preferred_element_type=F32) \
        + jnp.dot(mx_bf, wg_ref[1], preferred_element_type=F32)

    sub = lax.broadcasted_iota(jnp.int32, (CHUNK, DIR_LANES), 1) % GROUP_LANES
    for d, (cols_ref, rows_ref) in enumerate(((colsf_ref, rowsf_ref), (colsb_ref, rowsb_ref))):
        ds = slice(d * DIR_LANES, (d + 1) * DIR_LANES)
        packed = gates[0:CHUNK, ds]
        for ch in range(1, n_chunks):
            packed = packed + pltpu.roll(gates[ch * CHUNK:(ch + 1) * CHUNK, ds], ch * GROUP_LANES, 1)
        b = _token_scan(_log_sigmoid(packed), jnp.add, reverse=(d == 1))
        a = packed - pltpu.roll(b, DIR_LANES - (F_LANE - I_LANE), 1)
        cm = _token_scan(a, jnp.maximum, reverse=(d == 1))
        scan = LOG2E * jnp.where(sub < CM_LANE, a,
                                 jnp.where(sub < F_LANE, pltpu.roll(cm, CM_LANE - I_LANE, 1), b))
        for ch in range(n_chunks):
            cols = scan if ch == 0 else pltpu.roll(scan, DIR_LANES - ch * GROUP_LANES, 1)
            cols_ref[0, ch * CHUNK:(ch + 1) * CHUNK, :] = cols
            rows_ref[0, ch] = cols.T[0:SCAN_ROWS, :]
    szm_ref[0] = _silu(project(u, 3)).astype(BF16)

    for h in range(N_HEADS):
        hs = slice(h * HEAD_DIM, (h + 1) * HEAD_DIM)
        q_ref[0, :, hs] = jnp.dot(c_bf[:, hs], wq_ref[h], preferred_element_type=F32).astype(BF16)
        v_ref[0, :, hs] = jnp.dot(mx_bf[:, hs], wv_ref[h], preferred_element_type=F32).astype(BF16)
        kh = jnp.dot(c_bf[:, hs], wk_ref[h], preferred_element_type=F32)
        for ch in range(n_chunks):
            kt_ref[0, ch, hs, :] = kh[ch * CHUNK:(ch + 1) * CHUNK, :].T.astype(BF16)

    px_ref[0] = project(u, 0).astype(BF16)


def _lane_bcast(tile, lane):
    return jnp.broadcast_to(tile[:, lane:lane + 1], tile.shape)


def _sweep_state_phase(h, off, carry, q_ref, kt_ref, v_ref, rows_ref, sc_ref, sc_base):
    hs = slice(h * HEAD_DIM, (h + 1) * HEAD_DIM)
    rs = pl.ds(pl.multiple_of(off * CHUNK, CHUNK), CHUNK)
    c_old, n_old, m = carry
    b_last = sc_ref[sc_base + h]
    cm_last = sc_ref[sc_base + N_HEADS + h]
    m_last = jnp.maximum(m, cm_last)

    a_row = rows_ref[0, off, I_LANE + h:I_LANE + h + 1, :]
    ws_row = jnp.exp2(a_row - m_last)
    decay = jnp.exp2(jnp.full((1, HEAD_DIM), m - m_last, F32))

    qh, vh = q_ref[0, rs, hs], v_ref[0, rs, hs]
    kt = kt_ref[0, off, hs, :]
    q_c = jnp.dot(qh, c_old.astype(BF16), preferred_element_type=F32)
    qk = jnp.dot(qh, kt, preferred_element_type=F32)
    q_n = qh.astype(F32) * n_old[0:1, :]
    q_n = q_n[:, :CHUNK] + q_n[:, CHUNK:]

    ws_bf = ws_row.astype(BF16)
    kv = jnp.dot(kt * ws_bf, vh, preferred_element_type=F32)
    ws8 = jnp.broadcast_to(ws_bf, (8, CHUNK))
    kn = lax.dot_general(ws8, kt, (((1,), (1,)), ((), ())), preferred_element_type=F32)
    new_carry = (decay * c_old + kv, decay * n_old + kn, b_last + m_last)
    return new_carry, (m, a_row, q_c, qk, q_n)


def _sweep_output_phase(h, off, state, v_ref, cols_ref, causal, h_ref):
    hs = slice(h * HEAD_DIM, (h + 1) * HEAD_DIM)
    rs = pl.ds(pl.multiple_of(off * CHUNK, CHUNK), CHUNK)
    cols = cols_ref[0, rs, :]
    m, a_row, q_c, qk, q_n = state
    big_m = jnp.maximum(_lane_bcast(cols, CM_LANE + h), m)
    dmat = jnp.where(causal, jnp.exp2(a_row - big_m), 0.0)
    inter_w = jnp.exp2(m - big_m)
    exp_neg_mt = jnp.exp2(-(_lane_bcast(cols, F_LANE + h) + big_m))
    s = qk * dmat
    den = jnp.sum(s + inter_w * q_n, axis=-1, keepdims=True)
    inv = 1.0 / jnp.maximum(jnp.abs(den), exp_neg_mt[:, 0:1])
    inv_b = jnp.broadcast_to(inv, (CHUNK, CHUNK))
    num = jnp.dot(s.astype(BF16), v_ref[0, rs, hs], preferred_element_type=F32)
    out = (num + jnp.concatenate([inter_w, inter_w], axis=1) * q_c) \
        * jnp.concatenate([inv_b, inv_b], axis=1)
    h_ref[0, rs, hs] = out.astype(h_ref.dtype)


def _sweep_kernel(sc_ref,
                  qf_ref, ktf_ref, vf_ref, colsf_ref, rowsf_ref,
                  qb_ref, ktb_ref, vb_ref, colsb_ref, rowsb_ref,
                  hf_ref, hb_ref,
                  c_sc, n_sc, m_sc):
    b = pl.program_id(0)
    j = pl.program_id(1)
    nc = pl.num_programs(1) * SWEEP_CHUNKS

    @pl.when(j == 0)
    def _():
        c_sc[...] = jnp.zeros_like(c_sc)
        n_sc[...] = jnp.zeros_like(n_sc)
        for st in range(N_DIRS * N_HEADS):
            m_sc[st] = jnp.float32(0.0)

    per_chunk = N_DIRS * 2 * N_HEADS
    dirs = ((qf_ref, ktf_ref, vf_ref, colsf_ref, rowsf_ref, hf_ref),
            (qb_ref, ktb_ref, vb_ref, colsb_ref, rowsb_ref, hb_ref))
    units = [(d, h) for h in range(N_HEADS) for d in range(N_DIRS)]
    t_idx = lax.broadcasted_iota(jnp.int32, (CHUNK, CHUNK), 0)
    s_idx = lax.broadcasted_iota(jnp.int32, (CHUNK, CHUNK), 1)
    causal = (s_idx <= t_idx, s_idx >= t_idx)

    def one_chunk(sub, _):
        off = (sub, SWEEP_CHUNKS - 1 - sub)
        chunk_f = j * SWEEP_CHUNKS + sub
        base = ((b * nc + chunk_f) * per_chunk,
                (b * nc + (nc - 1 - chunk_f)) * per_chunk + 2 * N_HEADS)
        for g in range(0, len(units), SWEEP_GROUP):
            states = {}
            for d, h in units[g:g + SWEEP_GROUP]:
                st = d * N_HEADS + h
                q_ref, kt_ref, v_ref, _, rows_ref, _ = dirs[d]
                carry, states[d, h] = _sweep_state_phase(
                    h, off[d], (c_sc[st], n_sc[st], m_sc[st]), q_ref, kt_ref, v_ref, rows_ref,
                    sc_ref, base[d])
                c_sc[st], n_sc[st], m_sc[st] = carry
            for d, h in units[g:g + SWEEP_GROUP]:
                _sweep_output_phase(h, off[d], states[d, h], dirs[d][2], dirs[d][3], causal[d],
                                    dirs[d][5])

    lax.fori_loop(0, SWEEP_CHUNKS, one_chunk, None)


def _combine_kernel(hf_ref, hb_ref, c_ref, szm_ref, pxp_ref, px_ref, pxn_ref, szp_ref, x_ref,
                    invc_ref, pmat_ref, poolw_ref, mhw_ref, skipw_ref, wout_ref, gout_ref,
                    o_ref,
                    wout_sc):
    i = pl.program_id(1)
    n_tiles = pl.num_programs(1)
    tile = OUT_TILE

    @pl.when((pl.program_id(0) == 0) & (i == 0))
    def _():
        wout_sc[...] = wout_ref[...].astype(BF16)

    zero_halo = jnp.zeros((HALO, D_POOL), BF16)
    px_main = px_ref[0]
    ext = jnp.concatenate(
        [jnp.where(i == 0, zero_halo, pxp_ref[0]), px_main,
         jnp.where(i == n_tiles - 1, zero_halo, pxn_ref[0]),
         jnp.zeros((CHUNK - 2 * HALO, D_POOL), BF16)], axis=0)

    def branches(r0):
        rs = slice(r0, r0 + OUT_ROWS)

        inv_count = invc_ref[rs, :]
        y_p_parts = []
        for g in range(len(POOL_WINDOWS)):
            gs = slice(g * POOL_GROUP_DIM, (g + 1) * POOL_GROUP_DIM)
            total = jnp.concatenate(
                [jnp.dot(pmat_ref[g], ext[r0 + ch * CHUNK:r0 + (ch + 2) * CHUNK, gs],
                         preferred_element_type=F32) for ch in range(OUT_ROWS // CHUNK)], axis=0)
            pooled = total * inv_count[:, g:g + 1] - px_main[rs, gs].astype(F32)
            mixed = jnp.dot(pooled.astype(BF16), poolw_ref[g], preferred_element_type=F32)
            y_p_parts.append(mixed.astype(BF16) * szp_ref[0, rs, gs])
        y_p = jnp.concatenate(y_p_parts, axis=1)

        y_m_parts = []
        for h in range(N_HEADS):
            hs = slice(h * HEAD_DIM, (h + 1) * HEAD_DIM)
            ht = hf_ref[0, rs, hs].astype(F32) + hb_ref[0, rs, hs].astype(F32)
            mu = jnp.mean(ht, axis=-1, keepdims=True)
            dlt = ht - mu
            var = jnp.mean(dlt * dlt, axis=-1, keepdims=True)
            hn = (dlt * lax.rsqrt(var + EPS) * mhw_ref[:, hs]).astype(BF16)
            y_m_parts.append((hn + skipw_ref[:, hs] * c_ref[0, rs, hs]) * szm_ref[0, rs, hs])
        return y_p, jnp.concatenate(y_m_parts, axis=1)

    def project(r0, y_p, y_m):
        rs = slice(r0, r0 + OUT_ROWS)
        hres = x_ref[0, rs, :] + jnp.dot(y_p, wout_sc[0:D_POOL, :], preferred_element_type=F32) \
            + jnp.dot(y_m, wout_sc[D_POOL:, :], preferred_element_type=F32)
        ms = jnp.mean(hres * hres, axis=-1, keepdims=True)
        o_ref[0, rs, :] = hres * lax.rsqrt(ms + EPS) * gout_ref[...]

    starts = list(range(0, tile, OUT_ROWS))
    pending = branches(starts[0])
    for prev, nxt in zip(starts[:-1], starts[1:]):
        upcoming = branches(nxt)
        project(prev, *pending)
        pending = upcoming
    project(starts[-1], *pending)


def _block_diag_tiles(w):
    rows = w.reshape(N_HEADS, HEAD_DIM, QKV_BLOCK)
    col = np.arange(HEAD_DIM)
    spread = jnp.asarray((col[None, :] % QKV_BLOCK == np.arange(QKV_BLOCK)[:, None]), w.dtype)
    tiled = jnp.einsum('tro,oc->trc', rows, spread, precision=lax.Precision.HIGHEST)
    same_block = jnp.asarray(col[:, None] // QKV_BLOCK == col[None, :] // QKV_BLOCK)
    return jnp.where(same_block[None], tiled, 0.0)


def _gate_weights(w_gates, b_gates, bq, bk, bv):
    n_gates = N_DIRS * 2 * N_HEADS
    place = np.zeros((n_gates, N_DIRS * DIR_LANES), np.float32)
    for d in range(N_DIRS):
        for g in range(2 * N_HEADS):
            lane = (I_LANE + g) if g < N_HEADS else (F_LANE + g - N_HEADS)
            place[d * 2 * N_HEADS + g, d * DIR_LANES + lane] = 1.0
    place = jnp.asarray(place)
    hi = lax.Precision.HIGHEST
    rows = jnp.transpose(w_gates, (1, 0, 2)).reshape(3, N_HEADS, HEAD_DIM, n_gates)
    w_c = jnp.einsum('trc,tcg->trg', bq, rows[0], precision=hi) \
        + jnp.einsum('trc,tcg->trg', bk, rows[1], precision=hi)
    w_v = jnp.einsum('trc,tcg->trg', bv, rows[2], precision=hi)
    folded = jnp.stack([w_c, w_v]).reshape(2, D_MLSTM, n_gates)
    return (jnp.einsum('krg,gl->krl', folded, place, precision=hi),
            jnp.dot(b_gates.reshape(1, n_gates), place, precision=hi))


def _pool_band_matrices():
    t = np.arange(CHUNK)[:, None]
    r = np.arange(2 * CHUNK)[None, :] - HALO
    mats = []
    for w in POOL_WINDOWS:
        left = (w - 1) // 2
        right = w - 1 - left
        mats.append(((r >= t - left) & (r <= t + right)).astype(np.float32))
    return jnp.asarray(np.stack(mats), dtype=BF16)


def _pool_inverse_counts(seq_len):
    t = np.arange(seq_len)
    inv = np.ones((seq_len, 128), np.float32)
    for g, w in enumerate(POOL_WINDOWS):
        left = (w - 1) // 2
        right = w - 1 - left
        count = np.minimum(t + right, seq_len - 1) - np.maximum(t - left, 0) + 1
        inv[:, g] = 1.0 / count
    return jnp.asarray(inv)


def _conv_shift_matrix():
    t = np.arange(CHUNK)[:, None]
    r = np.arange(2 * CHUNK)[None, :] - HALO
    blocks = [r == t + (tap - CONV_WIDTH // 2) for tap in CONV_SIDE_TAPS]
    return jnp.asarray(np.concatenate(blocks, axis=0).astype(np.float32), dtype=BF16)


def _halo_specs(tile, n_halo, col):
    per = tile // HALO
    prev = pl.BlockSpec((1, HALO, D_MODEL), lambda b, i: (b, jnp.maximum(i * per - 1, 0), col))
    nxt = pl.BlockSpec((1, HALO, D_MODEL),
                       lambda b, i: (b, jnp.minimum((i + 1) * per, n_halo - 1), col))
    return prev, nxt


def kernel(x, norm_in_g, w_in, pool_w, pool_scale, conv_w, conv_b, w_q, w_k, w_v, w_gates,
           b_gates, mh_norm_w, skip_w, w_out, norm_out_g):
    B, S, D = x.shape
    assert D == D_MODEL and S % FRONT_TILE == 0 and S % OUT_TILE == 0 and FRONT_TILE % CHUNK == 0
    assert (FRONT_TILE // CHUNK) * GROUP_LANES <= DIR_LANES and CONV_WIDTH // 2 <= HALO
    assert norm_in_g.shape[0] == 1, "single-layer block"
    nc = S // CHUNK
    n_halo = S // HALO
    arb2 = pltpu.CompilerParams(dimension_semantics=("arbitrary", "arbitrary"),
                                vmem_limit_bytes=VMEM_LIMIT)

    bq, bk, bv = (_block_diag_tiles(w[0]) for w in (w_q, w_k, w_v))
    wg, bg = _gate_weights(w_gates[0], b_gates[0], bq, bk, bv)
    wq_t, wk_t, wv_t = bq.astype(BF16), (bk * float(HEAD_DIM) ** -0.5).astype(BF16), bv.astype(BF16)
    conv_w8 = jnp.pad(conv_w[0], ((0, 8 - CONV_WIDTH), (0, 0)))

    def const(shape):
        return pl.BlockSpec(shape, lambda b, i: (0,) * len(shape), pipeline_mode=pl.Buffered(1))

    x_prev, x_next = _halo_specs(FRONT_TILE, n_halo, 0)
    front_seq = pl.BlockSpec((1, FRONT_TILE, D), lambda b, i: (b, i, 0))
    front_cols = pl.BlockSpec((1, FRONT_TILE, DIR_LANES), lambda b, i: (b, i, 0))
    cpt = FRONT_TILE // CHUNK
    front_rows = pl.BlockSpec((1, cpt, SCAN_ROWS, CHUNK), lambda b, i: (b, i, 0, 0))
    front_kt = pl.BlockSpec((1, cpt, D, CHUNK), lambda b, i: (b, i, 0, 0))
    seq_bf = jax.ShapeDtypeStruct((B, S, D), BF16)
    cols_shape = jax.ShapeDtypeStruct((B, S, DIR_LANES), F32)
    rows_shape = jax.ShapeDtypeStruct((B, nc, SCAN_ROWS, CHUNK), F32)
    kt_shape = jax.ShapeDtypeStruct((B, nc, D, CHUNK), BF16)
    pool_x, silu_pz, silu_mz, q, kt, v, c, cols_f, cols_b, rows_f, rows_b = pl.pallas_call(
        _front_kernel,
        grid=(B, S // FRONT_TILE),
        in_specs=[x_prev, front_seq, x_next, const((1, D)), const((D, 4 * D)),
                  const((len(CONV_SIDE_TAPS) * CHUNK, 2 * CHUNK)),
                  const((8, D)), const((1, D)),
                  const((N_HEADS, HEAD_DIM, HEAD_DIM)), const((N_HEADS, HEAD_DIM, HEAD_DIM)),
                  const((N_HEADS, HEAD_DIM, HEAD_DIM)),
                  const((2, D, N_DIRS * DIR_LANES)), const((1, N_DIRS * DIR_LANES))],
        out_specs=[front_seq, front_seq, front_seq, front_seq, front_kt, front_seq, front_seq,
                   front_cols, front_cols, front_rows, front_rows],
        out_shape=[seq_bf, seq_bf, seq_bf, seq_bf, kt_shape, seq_bf, seq_bf,
                   cols_shape, cols_shape, rows_shape, rows_shape],
        scratch_shapes=[pltpu.VMEM((D, 4 * D), BF16)],
        compiler_params=arb2,
        name="front",
    )(x, x, x, norm_in_g[0][None, :], w_in[0], _conv_shift_matrix(), conv_w8,
      conv_b[0][None, :], wq_t, wk_t, wv_t, wg.astype(BF16), bg)

    def last(rows, lane):
        return jnp.concatenate([rows[:, :, F_LANE:F_LANE + N_HEADS, lane],
                                rows[:, :, CM_LANE:CM_LANE + N_HEADS, lane]], axis=-1)

    chunk_scalars = jnp.concatenate([last(rows_f, CHUNK - 1), last(rows_b, 0)], axis=-1).reshape(-1)

    sweep_steps = nc // SWEEP_CHUNKS
    sweep_rows = SWEEP_CHUNKS * CHUNK

    def sweep_specs(idx):
        seq = pl.BlockSpec((1, sweep_rows, D), lambda b, j: (b, idx(j), 0))
        cols = pl.BlockSpec((1, sweep_rows, DIR_LANES), lambda b, j: (b, idx(j), 0))
        rows = pl.BlockSpec((1, SWEEP_CHUNKS, 8, CHUNK), lambda b, j: (b, idx(j), 0, 0))
        kt_spec = pl.BlockSpec((1, SWEEP_CHUNKS, D, CHUNK), lambda b, j: (b, idx(j), 0, 0))
        return seq, [seq, kt_spec, seq, cols, rows]

    seq_f, in_f = sweep_specs(lambda j: j)
    seq_b, in_b = sweep_specs(lambda j: sweep_steps - 1 - j)
    n_state = N_DIRS * N_HEADS
    h_fwd, h_bwd = pl.pallas_call(
        _sweep_kernel,
        grid=(B, sweep_steps),
        in_specs=[pl.BlockSpec(memory_space=pltpu.SMEM)] + in_f + in_b,
        out_specs=[seq_f, seq_b],
        out_shape=[seq_bf, seq_bf],
        scratch_shapes=[pltpu.VMEM((n_state, HEAD_DIM, HEAD_DIM), F32),
                        pltpu.VMEM((n_state, 8, HEAD_DIM), F32),
                        pltpu.SMEM((n_state,), F32)],
        compiler_params=arb2,
        name="sweep",
    )(chunk_scalars, q, kt, v, cols_f, rows_f, q, kt, v, cols_b, rows_b)

    px_prev, px_next = _halo_specs(OUT_TILE, n_halo, 0)
    pool_w_scaled = pool_w[0] * pool_scale[0].reshape(len(POOL_WINDOWS), 1, POOL_GROUP_DIM)

    out_seq = pl.BlockSpec((1, OUT_TILE, D), lambda b, i: (b, i, 0))
    out = pl.pallas_call(
        _combine_kernel,
        grid=(B, S // OUT_TILE),
        in_specs=[out_seq, out_seq, out_seq, out_seq,
                  px_prev, out_seq, px_next, out_seq, out_seq,
                  pl.BlockSpec((OUT_TILE, 128), lambda b, i: (i, 0)),
                  const((len(POOL_WINDOWS), CHUNK, 2 * CHUNK)),
                  const((len(POOL_WINDOWS), POOL_GROUP_DIM, POOL_GROUP_DIM)),
                  const((1, D)), const((1, D)),
                  const((2 * D, D)), const((1, D))],
        out_specs=out_seq,
        out_shape=jax.ShapeDtypeStruct((B, S, D), F32),
        scratch_shapes=[pltpu.VMEM((2 * D, D), BF16)],
        compiler_params=arb2,
        name="combine",
    )(h_fwd, h_bwd, c, silu_mz, pool_x, pool_x, pool_x, silu_pz, x,
      _pool_inverse_counts(S), _pool_band_matrices(), pool_w_scaled.astype(BF16),
      mh_norm_w[0][None, :], skip_w[0][None, :].astype(BF16), w_out[0], norm_out_g[None, :])
    return out
```

```python
import numpy as np
import jax
import jax.numpy as jnp
from jax import lax
from jax.experimental import pallas as pl
from jax.experimental.pallas import tpu as pltpu

D_MODEL = 1024
D_POOL = 1024
D_MLSTM = 1024
POOL_WINDOWS = (2, 4, 8, 16)
POOL_GROUP_DIM = D_POOL // len(POOL_WINDOWS)
N_HEADS = 4
HEAD_DIM = 256
QKV_BLOCK = 4
CONV_WIDTH = 5
CHUNK = 128
N_DIRS = 2
EPS = 1e-6
LOG2E = 1.4426950408889634

HALO = 16
DIR_LANES = 128
GROUP_LANES = 16
I_LANE, CM_LANE, F_LANE = 0, 4, 8
SCAN_ROWS = 16
CONV_SIDE_TAPS = (0, 1, 3, 4)
FRONT_TILE = 512
OUT_TILE = 512
OUT_ROWS = 256
SWEEP_GROUP = 4
SWEEP_CHUNKS = 4
VMEM_LIMIT = 58 * 1024 * 1024

F32 = jnp.float32
BF16 = jnp.bfloat16


def _silu(z):
    return z * (1.0 / (1.0 + jnp.exp(-z)))


def _log_sigmoid(g):
    return jnp.minimum(g, 0.0) - jnp.log1p(jnp.exp(-jnp.abs(g)))


def _token_scan(x, op, reverse):
    t = lax.broadcasted_iota(jnp.int32, x.shape, 0)
    k = 1
    while k < CHUNK:
        if reverse:
            shifted = pltpu.roll(x, CHUNK - k, 0)
            valid = t < CHUNK - k
        else:
            shifted = pltpu.roll(x, k, 0)
            valid = t >= k
        x = jnp.where(valid, op(x, shifted), x)
        k *= 2
    return x


def _front_kernel(xp_ref, x_ref, xn_ref, gin_ref, win_ref, shift_ref, convw_ref, convb_ref,
                  wqkv_ref, wg_ref, bg_ref,
                  px_ref, szp_ref, szm_ref, q_ref, kt_ref, v_ref, c_ref,
                  colsf_ref, colsb_ref, rowsf_ref, rowsb_ref,
                  win_sc):
    i = pl.program_id(1)
    n_tiles = pl.num_programs(1)
    tile = FRONT_TILE
    n_chunks = tile // CHUNK

    @pl.when((pl.program_id(0) == 0) & (i == 0))
    def _():
        for group in range(4):
            gs = slice(group * D_MODEL, (group + 1) * D_MODEL)
            win_sc[:, gs] = win_ref[:, gs].astype(BF16)

    x_ext = jnp.concatenate([xp_ref[0], x_ref[0], xn_ref[0]], axis=0)
    ms = jnp.mean(x_ext * x_ext, axis=-1, keepdims=True)
    u_ext = (x_ext * lax.rsqrt(ms + EPS) * gin_ref[...]).astype(BF16)
    u = u_ext[HALO:HALO + tile, :]

    def project(lhs, group):
        return jnp.dot(lhs, win_sc[:, group * D_MODEL:(group + 1) * D_MODEL],
                       preferred_element_type=F32)

    mx_ext = project(u_ext, 2).astype(BF16)
    mx_bf = mx_ext[HALO:HALO + tile, :]
    zero_halo = jnp.zeros((HALO, D_MLSTM), BF16)
    ext = jnp.concatenate(
        [jnp.where(i == 0, zero_halo, mx_ext[0:HALO, :]), mx_bf,
         jnp.where(i == n_tiles - 1, zero_halo, mx_ext[HALO + tile:, :]),
         jnp.zeros((CHUNK - 2 * HALO, D_MLSTM), BF16)], axis=0)

    pad = CONV_WIDTH // 2
    conv_parts = []
    for ch in range(n_chunks):
        rs = slice(ch * CHUNK, (ch + 1) * CHUNK)
        shifted = jnp.dot(shift_ref[...], ext[ch * CHUNK:(ch + 2) * CHUNK, :],
                          preferred_element_type=F32)
        part = convb_ref[...] + mx_bf[rs, :].astype(F32) * convw_ref[pad:pad + 1, :]
        for n, tap in enumerate(CONV_SIDE_TAPS):
            part = part + shifted[n * CHUNK:(n + 1) * CHUNK, :] * convw_ref[tap:tap + 1, :]
        conv_parts.append(part)
    c_bf = _silu(jnp.concatenate(conv_parts, axis=0)).astype(BF16)
    c_ref[0] = c_bf
    szp_ref[0] = _silu(project(u, 1)).astype(BF16)

    gates = bg_ref[...] + jnp.dot(c_bf, wg_ref[0], preferred_element_type=F32) \
        + jnp.dot(mx_bf, wg_ref[1], preferred_element_type=F32)

    sub = lax.broadcasted_iota(jnp.int32, (CHUNK, DIR_LANES), 1) % GROUP_LANES
    for d, (cols_ref, rows_ref) in enumerate(((colsf_ref, rowsf_ref), (colsb_ref, rowsb_ref))):
        ds = slice(d * DIR_LANES, (d + 1) * DIR_LANES)
        packed = gates[0:CHUNK, ds]
        for ch in range(1, n_chunks):
            packed = packed + pltpu.roll(gates[ch * CHUNK:(ch + 1) * CHUNK, ds], ch * GROUP_LANES, 1)
        b = _token_scan(_log_sigmoid(packed), jnp.add, reverse=(d == 1))
        a = packed - pltpu.roll(b, DIR_LANES - (F_LANE - I_LANE), 1)
        cm = _token_scan(a, jnp.maximum, reverse=(d == 1))
        scan = LOG2E * jnp.where(sub < CM_LANE, a,
                                 jnp.where(sub < F_LANE, pltpu.roll(cm, CM_LANE - I_LANE, 1), b))
        for ch in range(n_chunks):
            cols = scan if ch == 0 else pltpu.roll(scan, DIR_LANES - ch * GROUP_LANES, 1)
            cols_ref[0, ch * CHUNK:(ch + 1) * CHUNK, :] = cols
            rows_ref[0, ch] = cols.T[0:SCAN_ROWS, :]
    szm_ref[0] = _silu(project(u, 3)).astype(BF16)

    for h in range(N_HEADS):
        hs = slice(h * HEAD_DIM, (h + 1) * HEAD_DIM)
        q_ref[0, :, hs] = jnp.dot(c_bf[:, hs], wqkv_ref[0, h],
                                  preferred_element_type=F32).astype(BF16)
        v_ref[0, :, hs] = jnp.dot(mx_bf[:, hs], wqkv_ref[2, h],
                                  preferred_element_type=F32).astype(BF16)
        kh = jnp.dot(c_bf[:, hs], wqkv_ref[1, h], preferred_element_type=F32)
        for ch in range(n_chunks):
            kt_ref[0, ch, hs, :] = kh[ch * CHUNK:(ch + 1) * CHUNK, :].T.astype(BF16)

    px_ref[0] = project(u, 0).astype(BF16)


def _lane_bcast(tile, lane):
    return jnp.broadcast_to(tile[:, lane:lane + 1], tile.shape)


def _sweep_state_phase(h, off, carry, q_ref, kt_ref, v_ref, rows_ref, sc_ref, sc_base):
    hs = slice(h * HEAD_DIM, (h + 1) * HEAD_DIM)
    rs = pl.ds(pl.multiple_of(off * CHUNK, CHUNK), CHUNK)
    c_old, n_old, m = carry
    b_last = sc_ref[sc_base + h]
    cm_last = sc_ref[sc_base + N_HEADS + h]
    m_last = jnp.maximum(m, cm_last)

    a_row = rows_ref[0, off, I_LANE + h:I_LANE + h + 1, :]
    ws_row = jnp.exp2(a_row - m_last)
    decay = jnp.exp2(jnp.full((1, HEAD_DIM), m - m_last, F32))

    qh, vh = q_ref[0, rs, hs], v_ref[0, rs, hs]
    kt = kt_ref[0, off, hs, :]
    q_c = jnp.dot(qh, c_old.astype(BF16), preferred_element_type=F32)
    qk = jnp.dot(qh, kt, preferred_element_type=F32)
    q_n = qh.astype(F32) * n_old[0:1, :]
    q_n = q_n[:, :CHUNK] + q_n[:, CHUNK:]

    ws_bf = ws_row.astype(BF16)
    kv = jnp.dot(kt * ws_bf, vh, preferred_element_type=F32)
    ws8 = jnp.broadcast_to(ws_bf, (8, CHUNK))
    kn = lax.dot_general(ws8, kt, (((1,), (1,)), ((), ())), preferred_element_type=F32)
    new_carry = (decay * c_old + kv, decay * n_old + kn, b_last + m_last)
    return new_carry, (m, a_row, q_c, qk, q_n)


def _sweep_output_phase(h, off, state, v_ref, cols_ref, causal, h_ref):
    hs = slice(h * HEAD_DIM, (h + 1) * HEAD_DIM)
    rs = pl.ds(pl.multiple_of(off * CHUNK, CHUNK), CHUNK)
    cols = cols_ref[0, rs, :]
    m, a_row, q_c, qk, q_n = state
    big_m = jnp.maximum(_lane_bcast(cols, CM_LANE + h), m)
    dmat = jnp.where(causal, jnp.exp2(a_row - big_m), 0.0)
    inter_w = jnp.exp2(m - big_m)
    exp_neg_mt = jnp.exp2(-(_lane_bcast(cols, F_LANE + h) + big_m))
    s = qk * dmat
    den = jnp.sum(s + inter_w * q_n, axis=-1, keepdims=True)
    inv = 1.0 / jnp.maximum(jnp.abs(den), exp_neg_mt[:, 0:1])
    inv_b = jnp.broadcast_to(inv, (CHUNK, CHUNK))
    num = jnp.dot(s.astype(BF16), v_ref[0, rs, hs], preferred_element_type=F32)
    out = (num + jnp.concatenate([inter_w, inter_w], axis=1) * q_c) \
        * jnp.concatenate([inv_b, inv_b], axis=1)
    h_ref[0, rs, hs] = out.astype(h_ref.dtype)


def _sweep_kernel(sc_ref,
                  qf_ref, ktf_ref, vf_ref, colsf_ref, rowsf_ref,
                  qb_ref, ktb_ref, vb_ref, colsb_ref, rowsb_ref,
                  hf_ref, hb_ref,
                  c_sc, n_sc, m_sc):
    b = pl.program_id(0)
    j = pl.program_id(1)
    nc = pl.num_programs(1) * SWEEP_CHUNKS

    @pl.when(j == 0)
    def _():
        c_sc[...] = jnp.zeros_like(c_sc)
        n_sc[...] = jnp.zeros_like(n_sc)
        for st in range(N_DIRS * N_HEADS):
            m_sc[st] = jnp.float32(0.0)

    per_chunk = N_DIRS * 2 * N_HEADS
    dirs = ((qf_ref, ktf_ref, vf_ref, colsf_ref, rowsf_ref, hf_ref),
            (qb_ref, ktb_ref, vb_ref, colsb_ref, rowsb_ref, hb_ref))
    units = [(d, h) for h in range(N_HEADS) for d in range(N_DIRS)]
    t_idx = lax.broadcasted_iota(jnp.int32, (CHUNK, CHUNK), 0)
    s_idx = lax.broadcasted_iota(jnp.int32, (CHUNK, CHUNK), 1)
    causal = (s_idx <= t_idx, s_idx >= t_idx)

    def one_chunk(sub, _):
        off = (sub, SWEEP_CHUNKS - 1 - sub)
        chunk_f = j * SWEEP_CHUNKS + sub
        base = ((b * nc + chunk_f) * per_chunk,
                (b * nc + (nc - 1 - chunk_f)) * per_chunk + 2 * N_HEADS)
        for g in range(0, len(units), SWEEP_GROUP):
            states = {}
            for d, h in units[g:g + SWEEP_GROUP]:
                st = d * N_HEADS + h
                q_ref, kt_ref, v_ref, _, rows_ref, _ = dirs[d]
                carry, states[d, h] = _sweep_state_phase(
                    h, off[d], (c_sc[st], n_sc[st], m_sc[st]), q_ref, kt_ref, v_ref, rows_ref,
                    sc_ref, base[d])
                c_sc[st], n_sc[st], m_sc[st] = carry
            for d, h in units[g:g + SWEEP_GROUP]:
                _sweep_output_phase(h, off[d], states[d, h], dirs[d][2], dirs[d][3], causal[d],
                                    dirs[d][5])

    lax.fori_loop(0, SWEEP_CHUNKS, one_chunk, None)


def _combine_kernel(hf_ref, hb_ref, c_ref, szm_ref, pxp_ref, px_ref, pxn_ref, szp_ref, x_ref,
                    invc_ref, pmat_ref, poolw_ref, mhw_ref, skipw_ref, wout_ref, gout_ref,
                    o_ref,
                    wout_sc):
    i = pl.program_id(1)
    n_tiles = pl.num_programs(1)
    tile = OUT_TILE

    @pl.when((pl.program_id(0) == 0) & (i == 0))
    def _():
        wout_sc[...] = wout_ref[...].astype(BF16)

    zero_halo = jnp.zeros((HALO, D_POOL), BF16)
    px_main = px_ref[0]
    ext = jnp.concatenate(
        [jnp.where(i == 0, zero_halo, pxp_ref[0]), px_main,
         jnp.where(i == n_tiles - 1, zero_halo, pxn_ref[0]),
         jnp.zeros((CHUNK - 2 * HALO, D_POOL), BF16)], axis=0)

    def branches(r0):
        rs = slice(r0, r0 + OUT_ROWS)

        inv_count = invc_ref[rs, :]
        y_p_parts = []
        for g in range(len(POOL_WINDOWS)):
            gs = slice(g * POOL_GROUP_DIM, (g + 1) * POOL_GROUP_DIM)
            total = jnp.concatenate(
                [jnp.dot(pmat_ref[g], ext[r0 + ch * CHUNK:r0 + (ch + 2) * CHUNK, gs],
                         preferred_element_type=F32) for ch in range(OUT_ROWS // CHUNK)], axis=0)
            pooled = total * inv_count[:, g:g + 1] - px_main[rs, gs].astype(F32)
            mixed = jnp.dot(pooled.astype(BF16), poolw_ref[g], preferred_element_type=F32)
            y_p_parts.append(mixed.astype(BF16) * szp_ref[0, rs, gs])
        y_p = jnp.concatenate(y_p_parts, axis=1)

        y_m_parts = []
        for h in range(N_HEADS):
            hs = slice(h * HEAD_DIM, (h + 1) * HEAD_DIM)
            ht = hf_ref[0, rs, hs].astype(F32) + hb_ref[0, rs, hs].astype(F32)
            mu = jnp.mean(ht, axis=-1, keepdims=True)
            dlt = ht - mu
            var = jnp.mean(dlt * dlt, axis=-1, keepdims=True)
            hn = (dlt * lax.rsqrt(var + EPS) * mhw_ref[:, hs]).astype(BF16)
            y_m_parts.append((hn + skipw_ref[:, hs] * c_ref[0, rs, hs]) * szm_ref[0, rs, hs])
        return y_p, jnp.concatenate(y_m_parts, axis=1)

    def project(r0, y_p, y_m):
        rs = slice(r0, r0 + OUT_ROWS)
        hres = x_ref[0, rs, :] + jnp.dot(y_p, wout_sc[0:D_POOL, :], preferred_element_type=F32) \
            + jnp.dot(y_m, wout_sc[D_POOL:, :], preferred_element_type=F32)
        ms = jnp.mean(hres * hres, axis=-1, keepdims=True)
        o_ref[0, rs, :] = hres * lax.rsqrt(ms + EPS) * gout_ref[...]

    starts = list(range(0, tile, OUT_ROWS))
    pending = branches(starts[0])
    for prev, nxt in zip(starts[:-1], starts[1:]):
        upcoming = branches(nxt)
        project(prev, *pending)
        pending = upcoming
    project(starts[-1], *pending)


def _block_diag_tiles(w):
    rows = w.reshape(3 * N_HEADS, HEAD_DIM, QKV_BLOCK)
    col = np.arange(HEAD_DIM)
    spread = jnp.asarray((col[None, :] % QKV_BLOCK == np.arange(QKV_BLOCK)[:, None]), w.dtype)
    tiled = jnp.einsum('tro,oc->trc', rows, spread, precision=lax.Precision.HIGHEST)
    same_block = jnp.asarray(col[:, None] // QKV_BLOCK == col[None, :] // QKV_BLOCK)
    return jnp.where(same_block[None], tiled, 0.0).reshape(3, N_HEADS, HEAD_DIM, HEAD_DIM)


def _gate_weights(w_gates, b_gates, tiles):
    n_gates = N_DIRS * 2 * N_HEADS
    place = np.zeros((n_gates, N_DIRS * DIR_LANES), np.float32)
    for d in range(N_DIRS):
        for g in range(2 * N_HEADS):
            lane = (I_LANE + g) if g < N_HEADS else (F_LANE + g - N_HEADS)
            place[d * 2 * N_HEADS + g, d * DIR_LANES + lane] = 1.0
    place = jnp.asarray(place)
    hi = lax.Precision.HIGHEST
    rows = jnp.transpose(w_gates, (1, 0, 2)).reshape(3, N_HEADS, HEAD_DIM, n_gates)
    per_map = jnp.einsum('ptrc,ptcg->ptrg', tiles, rows, precision=hi)
    folded = jnp.stack([per_map[0] + per_map[1], per_map[2]]).reshape(2, D_MLSTM, n_gates)
    return (jnp.einsum('krg,gl->krl', folded, place, precision=hi),
            jnp.dot(b_gates.reshape(1, n_gates), place, precision=hi))


def _pool_band_matrices():
    t = np.arange(CHUNK)[:, None]
    r = np.arange(2 * CHUNK)[None, :] - HALO
    mats = []
    for w in POOL_WINDOWS:
        left = (w - 1) // 2
        right = w - 1 - left
        mats.append(((r >= t - left) & (r <= t + right)).astype(np.float32))
    return jnp.asarray(np.stack(mats), dtype=BF16)


def _pool_inverse_counts(seq_len):
    t = np.arange(seq_len)
    inv = np.ones((seq_len, 128), np.float32)
    for g, w in enumerate(POOL_WINDOWS):
        left = (w - 1) // 2
        right = w - 1 - left
        count = np.minimum(t + right, seq_len - 1) - np.maximum(t - left, 0) + 1
        inv[:, g] = 1.0 / count
    return jnp.asarray(inv)


def _conv_shift_matrix():
    t = np.arange(CHUNK)[:, None]
    r = np.arange(2 * CHUNK)[None, :] - HALO
    blocks = [r == t + (tap - CONV_WIDTH // 2) for tap in CONV_SIDE_TAPS]
    return jnp.asarray(np.concatenate(blocks, axis=0).astype(np.float32), dtype=BF16)


def _halo_specs(tile, n_halo, col):
    per = tile // HALO
    prev = pl.BlockSpec((1, HALO, D_MODEL), lambda b, i: (b, jnp.maximum(i * per - 1, 0), col))
    nxt = pl.BlockSpec((1, HALO, D_MODEL),
                       lambda b, i: (b, jnp.minimum((i + 1) * per, n_halo - 1), col))
    return prev, nxt


def kernel(x, norm_in_g, w_in, pool_w, pool_scale, conv_w, conv_b, w_q, w_k, w_v, w_gates,
           b_gates, mh_norm_w, skip_w, w_out, norm_out_g):
    B, S, D = x.shape
    assert D == D_MODEL and S % FRONT_TILE == 0 and S % OUT_TILE == 0 and FRONT_TILE % CHUNK == 0
    assert (FRONT_TILE // CHUNK) * GROUP_LANES <= DIR_LANES and CONV_WIDTH // 2 <= HALO
    assert norm_in_g.shape[0] == 1, "single-layer block"
    nc = S // CHUNK
    n_halo = S // HALO
    arb2 = pltpu.CompilerParams(dimension_semantics=("arbitrary", "arbitrary"),
                                vmem_limit_bytes=VMEM_LIMIT)

    tiles = _block_diag_tiles(jnp.stack([w_q[0], w_k[0], w_v[0]]))
    wg, bg = _gate_weights(w_gates[0], b_gates[0], tiles)
    qkv_scale = jnp.asarray([1.0, float(HEAD_DIM) ** -0.5, 1.0], F32).reshape(3, 1, 1, 1)
    wqkv = (tiles * qkv_scale).astype(BF16)
    conv_w8 = jnp.pad(conv_w[0], ((0, 8 - CONV_WIDTH), (0, 0)))

    def const(shape):
        return pl.BlockSpec(shape, lambda b, i: (0,) * len(shape), pipeline_mode=pl.Buffered(1))

    x_prev, x_next = _halo_specs(FRONT_TILE, n_halo, 0)
    front_seq = pl.BlockSpec((1, FRONT_TILE, D), lambda b, i: (b, i, 0))
    front_cols = pl.BlockSpec((1, FRONT_TILE, DIR_LANES), lambda b, i: (b, i, 0))
    cpt = FRONT_TILE // CHUNK
    front_rows = pl.BlockSpec((1, cpt, SCAN_ROWS, CHUNK), lambda b, i: (b, i, 0, 0))
    front_kt = pl.BlockSpec((1, cpt, D, CHUNK), lambda b, i: (b, i, 0, 0))
    seq_bf = jax.ShapeDtypeStruct((B, S, D), BF16)
    cols_shape = jax.ShapeDtypeStruct((B, S, DIR_LANES), F32)
    rows_shape = jax.ShapeDtypeStruct((B, nc, SCAN_ROWS, CHUNK), F32)
    kt_shape = jax.ShapeDtypeStruct((B, nc, D, CHUNK), BF16)
    pool_x, silu_pz, silu_mz, q, kt, v, c, cols_f, cols_b, rows_f, rows_b = pl.pallas_call(
        _front_kernel,
        grid=(B, S // FRONT_TILE),
        in_specs=[x_prev, front_seq, x_next, const((1, D)), const((D, 4 * D)),
                  const((len(CONV_SIDE_TAPS) * CHUNK, 2 * CHUNK)),
                  const((8, D)), const((1, D)),
                  const((3, N_HEADS, HEAD_DIM, HEAD_DIM)),
                  const((2, D, N_DIRS * DIR_LANES)), const((1, N_DIRS * DIR_LANES))],
        out_specs=[front_seq, front_seq, front_seq, front_seq, front_kt, front_seq, front_seq,
                   front_cols, front_cols, front_rows, front_rows],
        out_shape=[seq_bf, seq_bf, seq_bf, seq_bf, kt_shape, seq_bf, seq_bf,
                   cols_shape, cols_shape, rows_shape, rows_shape],
        scratch_shapes=[pltpu.VMEM((D, 4 * D), BF16)],
        compiler_params=arb2,
        name="front",
    )(x, x, x, norm_in_g[0][None, :], w_in[0], _conv_shift_matrix(), conv_w8,
      conv_b[0][None, :], wqkv, wg.astype(BF16), bg)

    def last(rows, lane):
        return jnp.concatenate([rows[:, :, F_LANE:F_LANE + N_HEADS, lane],
                                rows[:, :, CM_LANE:CM_LANE + N_HEADS, lane]], axis=-1)

    chunk_scalars = jnp.concatenate([last(rows_f, CHUNK - 1), last(rows_b, 0)], axis=-1).reshape(-1)

    sweep_steps = nc // SWEEP_CHUNKS
    sweep_rows = SWEEP_CHUNKS * CHUNK

    def sweep_specs(idx):
        seq = pl.BlockSpec((1, sweep_rows, D), lambda b, j: (b, idx(j), 0))
        cols = pl.BlockSpec((1, sweep_rows, DIR_LANES), lambda b, j: (b, idx(j), 0))
        rows = pl.BlockSpec((1, SWEEP_CHUNKS, 8, CHUNK), lambda b, j: (b, idx(j), 0, 0))
        kt_spec = pl.BlockSpec((1, SWEEP_CHUNKS, D, CHUNK), lambda b, j: (b, idx(j), 0, 0))
        return seq, [seq, kt_spec, seq, cols, rows]

    seq_f, in_f = sweep_specs(lambda j: j)
    seq_b, in_b = sweep_specs(lambda j: sweep_steps - 1 - j)
    n_state = N_DIRS * N_HEADS
    h_fwd, h_bwd = pl.pallas_call(
        _sweep_kernel,
        grid=(B, sweep_steps),
        in_specs=[pl.BlockSpec(memory_space=pltpu.SMEM)] + in_f + in_b,
        out_specs=[seq_f, seq_b],
        out_shape=[seq_bf, seq_bf],
        scratch_shapes=[pltpu.VMEM((n_state, HEAD_DIM, HEAD_DIM), F32),
                        pltpu.VMEM((n_state, 8, HEAD_DIM), F32),
                        pltpu.SMEM((n_state,), F32)],
        compiler_params=arb2,
        name="sweep",
    )(chunk_scalars, q, kt, v, cols_f, rows_f, q, kt, v, cols_b, rows_b)

    px_prev, px_next = _halo_specs(OUT_TILE, n_halo, 0)
    pool_w_scaled = pool_w[0] * pool_scale[0].reshape(len(POOL_WINDOWS), 1, POOL_GROUP_DIM)

    out_seq = pl.BlockSpec((1, OUT_TILE, D), lambda b, i: (b, i, 0))
    out = pl.pallas_call(
        _combine_kernel,
        grid=(B, S // OUT_TILE),
        in_specs=[out_seq, out_seq, out_seq, out_seq,
                  px_prev, out_seq, px_next, out_seq, out_seq,
                  pl.BlockSpec((OUT_TILE, 128), lambda b, i: (i, 0)),
                  const((len(POOL_WINDOWS), CHUNK, 2 * CHUNK)),
                  const((len(POOL_WINDOWS), POOL_GROUP_DIM, POOL_GROUP_DIM)),
                  const((1, D)), const((1, D)),
                  const((2 * D, D)), const((1, D))],
        out_specs=out_seq,
        out_shape=jax.ShapeDtypeStruct((B, S, D), F32),
        scratch_shapes=[pltpu.VMEM((2 * D, D), BF16)],
        compiler_params=arb2,
        name="combine",
    )(h_fwd, h_bwd, c, silu_mz, pool_x, pool_x, pool_x, silu_pz, x,
      _pool_inverse_counts(S), _pool_band_matrices(), pool_w_scaled.astype(BF16),
      mh_norm_w[0][None, :], skip_w[0][None, :].astype(BF16), w_out[0], norm_out_g[None, :])
    return out
```

```python
import numpy as np
import jax
import jax.numpy as jnp
from jax import lax
from jax.experimental import pallas as pl
from jax.experimental.pallas import tpu as pltpu

D_MODEL = 1024
D_POOL = 1024
D_MLSTM = 1024
POOL_WINDOWS = (2, 4, 8, 16)
POOL_GROUP_DIM = D_POOL // len(POOL_WINDOWS)
N_HEADS = 4
HEAD_DIM = 256
QKV_BLOCK = 4
CONV_WIDTH = 5
CHUNK = 128
N_DIRS = 2
EPS = 1e-6
LOG2E = 1.4426950408889634

HALO = 16
DIR_LANES = 128
GROUP_LANES = 16
I_LANE, CM_LANE, F_LANE = 0, 4, 8
SCAN_ROWS = 16
CONV_SIDE_TAPS = (0, 1, 3, 4)
FRONT_TILE = 512
OUT_TILE = 512
OUT_ROWS = 256
SWEEP_GROUP = 4
SWEEP_CHUNKS = 4
VMEM_LIMIT = 58 * 1024 * 1024

F32 = jnp.float32
BF16 = jnp.bfloat16


def _silu(z):
    return z * (1.0 / (1.0 + jnp.exp(-z)))


def _log_sigmoid(g):
    return jnp.minimum(g, 0.0) - jnp.log1p(jnp.exp(-jnp.abs(g)))


def _token_scan(x, op, reverse):
    t = lax.broadcasted_iota(jnp.int32, x.shape, 0)
    k = 1
    while k < CHUNK:
        if reverse:
            shifted = pltpu.roll(x, CHUNK - k, 0)
            valid = t < CHUNK - k
        else:
            shifted = pltpu.roll(x, k, 0)
            valid = t >= k
        x = jnp.where(valid, op(x, shifted), x)
        k *= 2
    return x


def _front_kernel(xp_ref, x_ref, xn_ref, gin_ref, win_ref, convw_ref, convb_ref,
                  wqkv_ref, wg_ref, bg_ref,
                  px_ref, szp_ref, szm_ref, q_ref, kt_ref, v_ref, c_ref,
                  colsf_ref, colsb_ref, rowsf_ref, rowsb_ref,
                  win_sc):
    i = pl.program_id(1)
    n_tiles = pl.num_programs(1)
    tile = FRONT_TILE
    n_chunks = tile // CHUNK

    @pl.when((pl.program_id(0) == 0) & (i == 0))
    def _():
        for group in range(4):
            gs = slice(group * D_MODEL, (group + 1) * D_MODEL)
            win_sc[:, gs] = win_ref[:, gs].astype(BF16)

    x_ext = jnp.concatenate([xp_ref[0], x_ref[0], xn_ref[0]], axis=0)
    ms = jnp.mean(x_ext * x_ext, axis=-1, keepdims=True)
    u_ext = (x_ext * lax.rsqrt(ms + EPS) * gin_ref[...]).astype(BF16)
    u = u_ext[HALO:HALO + tile, :]

    def project(lhs, group):
        return jnp.dot(lhs, win_sc[:, group * D_MODEL:(group + 1) * D_MODEL],
                       preferred_element_type=F32)

    mx32 = project(u_ext, 2)
    mx_bf = mx32[HALO:HALO + tile, :].astype(BF16)
    zero_halo = jnp.zeros((HALO, D_MLSTM), F32)
    rows = tile + 2 * HALO
    ext = jnp.concatenate(
        [jnp.where(i == 0, zero_halo, mx32[0:HALO, :]), mx32[HALO:HALO + tile, :],
         jnp.where(i == n_tiles - 1, zero_halo, mx32[HALO + tile:, :])], axis=0)

    pad = CONV_WIDTH // 2
    conv = convb_ref[...] + ext[HALO:HALO + tile, :] * convw_ref[pad:pad + 1, :]
    for tap in range(CONV_WIDTH):
        if tap != pad:
            shifted = pltpu.roll(ext, (rows - (tap - pad)) % rows, 0)
            conv = conv + shifted[HALO:HALO + tile, :] * convw_ref[tap:tap + 1, :]
    c_bf = _silu(conv).astype(BF16)
    c_ref[0] = c_bf
    szp_ref[0] = _silu(project(u, 1)).astype(BF16)

    gates = bg_ref[...] + jnp.dot(c_bf, wg_ref[0], preferred_element_type=F32) \
        + jnp.dot(mx_bf, wg_ref[1], preferred_element_type=F32)

    sub = lax.broadcasted_iota(jnp.int32, (CHUNK, DIR_LANES), 1) % GROUP_LANES
    for d, (cols_ref, rows_ref) in enumerate(((colsf_ref, rowsf_ref), (colsb_ref, rowsb_ref))):
        ds = slice(d * DIR_LANES, (d + 1) * DIR_LANES)
        packed = gates[0:CHUNK, ds]
        for ch in range(1, n_chunks):
            packed = packed + pltpu.roll(gates[ch * CHUNK:(ch + 1) * CHUNK, ds], ch * GROUP_LANES, 1)
        b = _token_scan(_log_sigmoid(packed), jnp.add, reverse=(d == 1))
        a = packed - pltpu.roll(b, DIR_LANES - (F_LANE - I_LANE), 1)
        cm = _token_scan(a, jnp.maximum, reverse=(d == 1))
        scan = LOG2E * jnp.where(sub < CM_LANE, a,
                                 jnp.where(sub < F_LANE, pltpu.roll(cm, CM_LANE - I_LANE, 1), b))
        for ch in range(n_chunks):
            cols = scan if ch == 0 else pltpu.roll(scan, DIR_LANES - ch * GROUP_LANES, 1)
            cols_ref[0, ch * CHUNK:(ch + 1) * CHUNK, :] = cols
            rows_ref[0, ch] = cols.T[0:SCAN_ROWS, :]
    szm_ref[0] = _silu(project(u, 3)).astype(BF16)

    for h in range(N_HEADS):
        hs = slice(h * HEAD_DIM, (h + 1) * HEAD_DIM)
        q_ref[0, :, hs] = jnp.dot(c_bf[:, hs], wqkv_ref[0, h],
                                  preferred_element_type=F32).astype(BF16)
        v_ref[0, :, hs] = jnp.dot(mx_bf[:, hs], wqkv_ref[2, h],
                                  preferred_element_type=F32).astype(BF16)
        kh = jnp.dot(c_bf[:, hs], wqkv_ref[1, h], preferred_element_type=F32)
        for ch in range(n_chunks):
            kt_ref[0, ch, hs, :] = kh[ch * CHUNK:(ch + 1) * CHUNK, :].T.astype(BF16)

    px_ref[0] = project(u, 0).astype(BF16)


def _lane_bcast(tile, lane):
    return jnp.broadcast_to(tile[:, lane:lane + 1], tile.shape)


def _sweep_state_phase(h, off, carry, q_ref, kt_ref, v_ref, rows_ref, sc_ref, sc_base):
    hs = slice(h * HEAD_DIM, (h + 1) * HEAD_DIM)
    rs = pl.ds(pl.multiple_of(off * CHUNK, CHUNK), CHUNK)
    c_old, n_old, m = carry
    b_last = sc_ref[sc_base + h]
    cm_last = sc_ref[sc_base + N_HEADS + h]
    m_last = jnp.maximum(m, cm_last)

    a_row = rows_ref[0, off, I_LANE + h:I_LANE + h + 1, :]
    ws_row = jnp.exp2(a_row - m_last)
    decay = jnp.exp2(jnp.full((1, HEAD_DIM), m - m_last, F32))

    qh, vh = q_ref[0, rs, hs], v_ref[0, rs, hs]
    kt = kt_ref[0, off, hs, :]
    q_c = jnp.dot(qh, c_old.astype(BF16), preferred_element_type=F32)
    qk = jnp.dot(qh, kt, preferred_element_type=F32)
    q_n = qh.astype(F32) * n_old[0:1, :]
    q_n = q_n[:, :CHUNK] + q_n[:, CHUNK:]

    ws_bf = ws_row.astype(BF16)
    kv = jnp.dot(kt * ws_bf, vh, preferred_element_type=F32)
    ws8 = jnp.broadcast_to(ws_bf, (8, CHUNK))
    kn = lax.dot_general(ws8, kt, (((1,), (1,)), ((), ())), preferred_element_type=F32)
    new_carry = (decay * c_old + kv, decay * n_old + kn, b_last + m_last)
    return new_carry, (m, a_row, q_c, qk, q_n)


def _sweep_output_phase(h, off, state, v_ref, cols_ref, causal, h_ref):
    hs = slice(h * HEAD_DIM, (h + 1) * HEAD_DIM)
    rs = pl.ds(pl.multiple_of(off * CHUNK, CHUNK), CHUNK)
    cols = cols_ref[0, rs, :]
    m, a_row, q_c, qk, q_n = state
    big_m = jnp.maximum(_lane_bcast(cols, CM_LANE + h), m)
    dmat = jnp.where(causal, jnp.exp2(a_row - big_m), 0.0)
    inter_w = jnp.exp2(m - big_m)
    exp_neg_mt = jnp.exp2(-(_lane_bcast(cols, F_LANE + h) + big_m))
    s = qk * dmat
    den = jnp.sum(s + inter_w * q_n, axis=-1, keepdims=True)
    inv = 1.0 / jnp.maximum(jnp.abs(den), exp_neg_mt[:, 0:1])
    inv_b = jnp.broadcast_to(inv, (CHUNK, CHUNK))
    num = jnp.dot(s.astype(BF16), v_ref[0, rs, hs], preferred_element_type=F32)
    out = (num + jnp.concatenate([inter_w, inter_w], axis=1) * q_c) \
        * jnp.concatenate([inv_b, inv_b], axis=1)
    h_ref[0, rs, hs] = out.astype(h_ref.dtype)


def _sweep_kernel(sc_ref,
                  qf_ref, ktf_ref, vf_ref, colsf_ref, rowsf_ref,
                  qb_ref, ktb_ref, vb_ref, colsb_ref, rowsb_ref,
                  hf_ref, hb_ref,
                  c_sc, n_sc, m_sc):
    b = pl.program_id(0)
    j = pl.program_id(1)
    nc = pl.num_programs(1) * SWEEP_CHUNKS

    @pl.when(j == 0)
    def _():
        c_sc[...] = jnp.zeros_like(c_sc)
        n_sc[...] = jnp.zeros_like(n_sc)
        for st in range(N_DIRS * N_HEADS):
            m_sc[st] = jnp.float32(0.0)

    per_chunk = N_DIRS * 2 * N_HEADS
    dirs = ((qf_ref, ktf_ref, vf_ref, colsf_ref, rowsf_ref, hf_ref),
            (qb_ref, ktb_ref, vb_ref, colsb_ref, rowsb_ref, hb_ref))
    units = [(d, h) for h in range(N_HEADS) for d in range(N_DIRS)]
    t_idx = lax.broadcasted_iota(jnp.int32, (CHUNK, CHUNK), 0)
    s_idx = lax.broadcasted_iota(jnp.int32, (CHUNK, CHUNK), 1)
    causal = (s_idx <= t_idx, s_idx >= t_idx)

    def one_chunk(sub, _):
        off = (sub, SWEEP_CHUNKS - 1 - sub)
        chunk_f = j * SWEEP_CHUNKS + sub
        base = ((b * nc + chunk_f) * per_chunk,
                (b * nc + (nc - 1 - chunk_f)) * per_chunk + 2 * N_HEADS)
        for g in range(0, len(units), SWEEP_GROUP):
            states = {}
            for d, h in units[g:g + SWEEP_GROUP]:
                st = d * N_HEADS + h
                q_ref, kt_ref, v_ref, _, rows_ref, _ = dirs[d]
                carry, states[d, h] = _sweep_state_phase(
                    h, off[d], (c_sc[st], n_sc[st], m_sc[st]), q_ref, kt_ref, v_ref, rows_ref,
                    sc_ref, base[d])
                c_sc[st], n_sc[st], m_sc[st] = carry
            for d, h in units[g:g + SWEEP_GROUP]:
                _sweep_output_phase(h, off[d], states[d, h], dirs[d][2], dirs[d][3], causal[d],
                                    dirs[d][5])

    lax.fori_loop(0, SWEEP_CHUNKS, one_chunk, None)


def _combine_kernel(hf_ref, hb_ref, c_ref, szm_ref, pxp_ref, px_ref, pxn_ref, szp_ref, x_ref,
                    invc_ref, pmat_ref, poolw_ref, mhw_ref, skipw_ref, wout_ref, gout_ref,
                    o_ref,
                    wout_sc):
    i = pl.program_id(1)
    n_tiles = pl.num_programs(1)
    tile = OUT_TILE

    @pl.when((pl.program_id(0) == 0) & (i == 0))
    def _():
        wout_sc[...] = wout_ref[...].astype(BF16)

    zero_halo = jnp.zeros((HALO, D_POOL), BF16)
    px_main = px_ref[0]
    ext = jnp.concatenate(
        [jnp.where(i == 0, zero_halo, pxp_ref[0]), px_main,
         jnp.where(i == n_tiles - 1, zero_halo, pxn_ref[0]),
         jnp.zeros((CHUNK - 2 * HALO, D_POOL), BF16)], axis=0)

    def branches(r0):
        rs = slice(r0, r0 + OUT_ROWS)

        inv_count = invc_ref[rs, :]
        y_p_parts = []
        for g in range(len(POOL_WINDOWS)):
            gs = slice(g * POOL_GROUP_DIM, (g + 1) * POOL_GROUP_DIM)
            total = jnp.concatenate(
                [jnp.dot(pmat_ref[g], ext[r0 + ch * CHUNK:r0 + (ch + 2) * CHUNK, gs],
                         preferred_element_type=F32) for ch in range(OUT_ROWS // CHUNK)], axis=0)
            pooled = total * inv_count[:, g:g + 1] - px_main[rs, gs].astype(F32)
            mixed = jnp.dot(pooled.astype(BF16), poolw_ref[g], preferred_element_type=F32)
            y_p_parts.append(mixed.astype(BF16) * szp_ref[0, rs, gs])
        y_p = jnp.concatenate(y_p_parts, axis=1)

        y_m_parts = []
        for h in range(N_HEADS):
            hs = slice(h * HEAD_DIM, (h + 1) * HEAD_DIM)
            ht = hf_ref[0, rs, hs].astype(F32) + hb_ref[0, rs, hs].astype(F32)
            mu = jnp.mean(ht, axis=-1, keepdims=True)
            dlt = ht - mu
            var = jnp.mean(dlt * dlt, axis=-1, keepdims=True)
            hn = (dlt * lax.rsqrt(var + EPS) * mhw_ref[:, hs]).astype(BF16)
            y_m_parts.append((hn + skipw_ref[:, hs] * c_ref[0, rs, hs]) * szm_ref[0, rs, hs])
        return y_p, jnp.concatenate(y_m_parts, axis=1)

    def project(r0, y_p, y_m):
        rs = slice(r0, r0 + OUT_ROWS)
        hres = x_ref[0, rs, :] + jnp.dot(y_p, wout_sc[0:D_POOL, :], preferred_element_type=F32) \
            + jnp.dot(y_m, wout_sc[D_POOL:, :], preferred_element_type=F32)
        ms = jnp.mean(hres * hres, axis=-1, keepdims=True)
        o_ref[0, rs, :] = hres * lax.rsqrt(ms + EPS) * gout_ref[...]

    starts = list(range(0, tile, OUT_ROWS))
    pending = branches(starts[0])
    for prev, nxt in zip(starts[:-1], starts[1:]):
        upcoming = branches(nxt)
        project(prev, *pending)
        pending = upcoming
    project(starts[-1], *pending)


def _block_diag_tiles(w):
    rows = w.reshape(3 * N_HEADS, HEAD_DIM, QKV_BLOCK)
    col = np.arange(HEAD_DIM)
    spread = jnp.asarray((col[None, :] % QKV_BLOCK == np.arange(QKV_BLOCK)[:, None]), w.dtype)
    tiled = jnp.einsum('tro,oc->trc', rows, spread, precision=lax.Precision.HIGHEST)
    same_block = jnp.asarray(col[:, None] // QKV_BLOCK == col[None, :] // QKV_BLOCK)
    return jnp.where(same_block[None], tiled, 0.0).reshape(3, N_HEADS, HEAD_DIM, HEAD_DIM)


def _gate_weights(w_gates, b_gates, tiles):
    n_gates = N_DIRS * 2 * N_HEADS
    place = np.zeros((n_gates, N_DIRS * DIR_LANES), np.float32)
    for d in range(N_DIRS):
        for g in range(2 * N_HEADS):
            lane = (I_LANE + g) if g < N_HEADS else (F_LANE + g - N_HEADS)
            place[d * 2 * N_HEADS + g, d * DIR_LANES + lane] = 1.0
    place = jnp.asarray(place)
    hi = lax.Precision.HIGHEST
    rows = jnp.transpose(w_gates, (1, 0, 2)).reshape(3, N_HEADS, HEAD_DIM, n_gates)
    per_map = jnp.einsum('ptrc,ptcg->ptrg', tiles, rows, precision=hi)
    folded = jnp.stack([per_map[0] + per_map[1], per_map[2]]).reshape(2, D_MLSTM, n_gates)
    return (jnp.einsum('krg,gl->krl', folded, place, precision=hi),
            jnp.dot(b_gates.reshape(1, n_gates), place, precision=hi))


def _pool_band_matrices():
    t = np.arange(CHUNK)[:, None]
    r = np.arange(2 * CHUNK)[None, :] - HALO
    mats = []
    for w in POOL_WINDOWS:
        left = (w - 1) // 2
        right = w - 1 - left
        mats.append(((r >= t - left) & (r <= t + right)).astype(np.float32))
    return jnp.asarray(np.stack(mats), dtype=BF16)


def _pool_inverse_counts(seq_len):
    t = np.arange(seq_len)
    inv = np.ones((seq_len, 128), np.float32)
    for g, w in enumerate(POOL_WINDOWS):
        left = (w - 1) // 2
        right = w - 1 - left
        count = np.minimum(t + right, seq_len - 1) - np.maximum(t - left, 0) + 1
        inv[:, g] = 1.0 / count
    return jnp.asarray(inv)


def _conv_shift_matrix():
    t = np.arange(CHUNK)[:, None]
    r = np.arange(2 * CHUNK)[None, :] - HALO
    blocks = [r == t + (tap - CONV_WIDTH // 2) for tap in CONV_SIDE_TAPS]
    return jnp.asarray(np.concatenate(blocks, axis=0).astype(np.float32), dtype=BF16)


def _halo_specs(tile, n_halo, col):
    per = tile // HALO
    prev = pl.BlockSpec((1, HALO, D_MODEL), lambda b, i: (b, jnp.maximum(i * per - 1, 0), col))
    nxt = pl.BlockSpec((1, HALO, D_MODEL),
                       lambda b, i: (b, jnp.minimum((i + 1) * per, n_halo - 1), col))
    return prev, nxt


def kernel(x, norm_in_g, w_in, pool_w, pool_scale, conv_w, conv_b, w_q, w_k, w_v, w_gates,
           b_gates, mh_norm_w, skip_w, w_out, norm_out_g):
    B, S, D = x.shape
    assert D == D_MODEL and S % FRONT_TILE == 0 and S % OUT_TILE == 0 and FRONT_TILE % CHUNK == 0
    assert (FRONT_TILE // CHUNK) * GROUP_LANES <= DIR_LANES and CONV_WIDTH // 2 <= HALO
    assert norm_in_g.shape[0] == 1, "single-layer block"
    nc = S // CHUNK
    n_halo = S // HALO
    arb2 = pltpu.CompilerParams(dimension_semantics=("arbitrary", "arbitrary"),
                                vmem_limit_bytes=VMEM_LIMIT)

    tiles = _block_diag_tiles(jnp.stack([w_q[0], w_k[0], w_v[0]]))
    wg, bg = _gate_weights(w_gates[0], b_gates[0], tiles)
    qkv_scale = jnp.asarray([1.0, float(HEAD_DIM) ** -0.5, 1.0], F32).reshape(3, 1, 1, 1)
    wqkv = (tiles * qkv_scale).astype(BF16)
    conv_w8 = jnp.pad(conv_w[0], ((0, 8 - CONV_WIDTH), (0, 0)))

    def const(shape):
        return pl.BlockSpec(shape, lambda b, i: (0,) * len(shape), pipeline_mode=pl.Buffered(1))

    x_prev, x_next = _halo_specs(FRONT_TILE, n_halo, 0)
    front_seq = pl.BlockSpec((1, FRONT_TILE, D), lambda b, i: (b, i, 0))
    front_cols = pl.BlockSpec((1, FRONT_TILE, DIR_LANES), lambda b, i: (b, i, 0))
    cpt = FRONT_TILE // CHUNK
    front_rows = pl.BlockSpec((1, cpt, SCAN_ROWS, CHUNK), lambda b, i: (b, i, 0, 0))
    front_kt = pl.BlockSpec((1, cpt, D, CHUNK), lambda b, i: (b, i, 0, 0))
    seq_bf = jax.ShapeDtypeStruct((B, S, D), BF16)
    cols_shape = jax.ShapeDtypeStruct((B, S, DIR_LANES), F32)
    rows_shape = jax.ShapeDtypeStruct((B, nc, SCAN_ROWS, CHUNK), F32)
    kt_shape = jax.ShapeDtypeStruct((B, nc, D, CHUNK), BF16)
    pool_x, silu_pz, silu_mz, q, kt, v, c, cols_f, cols_b, rows_f, rows_b = pl.pallas_call(
        _front_kernel,
        grid=(B, S // FRONT_TILE),
        in_specs=[x_prev, front_seq, x_next, const((1, D)), const((D, 4 * D)),
                  const((8, D)), const((1, D)),
                  const((3, N_HEADS, HEAD_DIM, HEAD_DIM)),
                  const((2, D, N_DIRS * DIR_LANES)), const((1, N_DIRS * DIR_LANES))],
        out_specs=[front_seq, front_seq, front_seq, front_seq, front_kt, front_seq, front_seq,
                   front_cols, front_cols, front_rows, front_rows],
        out_shape=[seq_bf, seq_bf, seq_bf, seq_bf, kt_shape, seq_bf, seq_bf,
                   cols_shape, cols_shape, rows_shape, rows_shape],
        scratch_shapes=[pltpu.VMEM((D, 4 * D), BF16)],
        compiler_params=arb2,
        name="front",
    )(x, x, x, norm_in_g[0][None, :], w_in[0], conv_w8,
      conv_b[0][None, :], wqkv, wg.astype(BF16), bg)

    def last(rows, lane):
        return jnp.concatenate([rows[:, :, F_LANE:F_LANE + N_HEADS, lane],
                                rows[:, :, CM_LANE:CM_LANE + N_HEADS, lane]], axis=-1)

    chunk_scalars = jnp.concatenate([last(rows_f, CHUNK - 1), last(rows_b, 0)], axis=-1).reshape(-1)

    sweep_steps = nc // SWEEP_CHUNKS
    sweep_rows = SWEEP_CHUNKS * CHUNK

    def sweep_specs(idx):
        seq = pl.BlockSpec((1, sweep_rows, D), lambda b, j: (b, idx(j), 0))
        cols = pl.BlockSpec((1, sweep_rows, DIR_LANES), lambda b, j: (b, idx(j), 0))
        rows = pl.BlockSpec((1, SWEEP_CHUNKS, 8, CHUNK), lambda b, j: (b, idx(j), 0, 0))
        kt_spec = pl.BlockSpec((1, SWEEP_CHUNKS, D, CHUNK), lambda b, j: (b, idx(j), 0, 0))
        return seq, [seq, kt_spec, seq, cols, rows]

    seq_f, in_f = sweep_specs(lambda j: j)
    seq_b, in_b = sweep_specs(lambda j: sweep_steps - 1 - j)
    n_state = N_DIRS * N_HEADS
    h_fwd, h_bwd = pl.pallas_call(
        _sweep_kernel,
        grid=(B, sweep_steps),
        in_specs=[pl.BlockSpec(memory_space=pltpu.SMEM)] + in_f + in_b,
        out_specs=[seq_f, seq_b],
        out_shape=[seq_bf, seq_bf],
        scratch_shapes=[pltpu.VMEM((n_state, HEAD_DIM, HEAD_DIM), F32),
                        pltpu.VMEM((n_state, 8, HEAD_DIM), F32),
                        pltpu.SMEM((n_state,), F32)],
        compiler_params=arb2,
        name="sweep",
    )(chunk_scalars, q, kt, v, cols_f, rows_f, q, kt, v, cols_b, rows_b)

    px_prev, px_next = _halo_specs(OUT_TILE, n_halo, 0)
    pool_w_scaled = pool_w[0] * pool_scale[0].reshape(len(POOL_WINDOWS), 1, POOL_GROUP_DIM)

    out_seq = pl.BlockSpec((1, OUT_TILE, D), lambda b, i: (b, i, 0))
    out = pl.pallas_call(
        _combine_kernel,
        grid=(B, S // OUT_TILE),
        in_specs=[out_seq, out_seq, out_seq, out_seq,
                  px_prev, out_seq, px_next, out_seq, out_seq,
                  pl.BlockSpec((OUT_TILE, 128), lambda b, i: (i, 0)),
                  const((len(POOL_WINDOWS), CHUNK, 2 * CHUNK)),
                  const((len(POOL_WINDOWS), POOL_GROUP_DIM, POOL_GROUP_DIM)),
                  const((1, D)), const((1, D)),
                  const((2 * D, D)), const((1, D))],
        out_specs=out_seq,
        out_shape=jax.ShapeDtypeStruct((B, S, D), F32),
        scratch_shapes=[pltpu.VMEM((2 * D, D), BF16)],
        compiler_params=arb2,
        name="combine",
    )(h_fwd, h_bwd, c, silu_mz, pool_x, pool_x, pool_x, silu_pz, x,
      _pool_inverse_counts(S), _pool_band_matrices(), pool_w_scaled.astype(BF16),
      mh_norm_w[0][None, :], skip_w[0][None, :].astype(BF16), w_out[0], norm_out_g[None, :])
    return out
```

```python
import numpy as np
import jax
import jax.numpy as jnp
from jax import lax
from jax.experimental import pallas as pl
from jax.experimental.pallas import tpu as pltpu

D_MODEL = 1024
D_POOL = 1024
D_MLSTM = 1024
POOL_WINDOWS = (2, 4, 8, 16)
POOL_GROUP_DIM = D_POOL // len(POOL_WINDOWS)
N_HEADS = 4
HEAD_DIM = 256
QKV_BLOCK = 4
CONV_WIDTH = 5
CHUNK = 128
N_DIRS = 2
EPS = 1e-6
LOG2E = 1.4426950408889634

HALO = 16
DIR_LANES = 128
GROUP_LANES = 16
I_LANE, CM_LANE, F_LANE = 0, 4, 8
SCAN_ROWS = 16
CONV_SIDE_TAPS = (0, 1, 3, 4)
FRONT_TILE = 512
OUT_TILE = 512
OUT_ROWS = 256
SWEEP_GROUP = 4
SWEEP_CHUNKS = 4
VMEM_LIMIT = 58 * 1024 * 1024

F32 = jnp.float32
BF16 = jnp.bfloat16


def _silu(z):
    return z * (1.0 / (1.0 + jnp.exp(-z)))


def _log_sigmoid(g):
    return jnp.minimum(g, 0.0) - jnp.log1p(jnp.exp(-jnp.abs(g)))


def _token_scan(x, op, reverse):
    t = lax.broadcasted_iota(jnp.int32, x.shape, 0)
    k = 1
    while k < CHUNK:
        if reverse:
            shifted = pltpu.roll(x, CHUNK - k, 0)
            valid = t < CHUNK - k
        else:
            shifted = pltpu.roll(x, k, 0)
            valid = t >= k
        x = jnp.where(valid, op(x, shifted), x)
        k *= 2
    return x


def _front_kernel(xp_ref, x_ref, xn_ref, gin_ref, win_ref, convw_ref, convb_ref,
                  wqkv_ref, wg_ref, bg_ref,
                  px_ref, szp_ref, szm_ref, q_ref, kt_ref, v_ref, c_ref,
                  colsf_ref, colsb_ref, rowsf_ref, rowsb_ref,
                  win_sc):
    i = pl.program_id(1)
    n_tiles = pl.num_programs(1)
    tile = FRONT_TILE
    n_chunks = tile // CHUNK

    @pl.when((pl.program_id(0) == 0) & (i == 0))
    def _():
        for group in range(4):
            gs = slice(group * D_MODEL, (group + 1) * D_MODEL)
            win_sc[:, gs] = win_ref[:, gs].astype(BF16)

    x_ext = jnp.concatenate([xp_ref[0], x_ref[0], xn_ref[0]], axis=0)
    ms = jnp.mean(x_ext * x_ext, axis=-1, keepdims=True)
    u_ext = (x_ext * lax.rsqrt(ms + EPS) * gin_ref[...]).astype(BF16)
    u = u_ext[HALO:HALO + tile, :]

    def project(lhs, group):
        return jnp.dot(lhs, win_sc[:, group * D_MODEL:(group + 1) * D_MODEL],
                       preferred_element_type=F32)

    mx32 = project(u_ext, 2)
    mx_bf = mx32[HALO:HALO + tile, :].astype(BF16)
    zero_halo = jnp.zeros((HALO, D_MLSTM), F32)
    rows = tile + 2 * HALO
    ext = jnp.concatenate(
        [jnp.where(i == 0, zero_halo, mx32[0:HALO, :]), mx32[HALO:HALO + tile, :],
         jnp.where(i == n_tiles - 1, zero_halo, mx32[HALO + tile:, :])], axis=0)

    pad = CONV_WIDTH // 2
    conv = convb_ref[...] + ext[HALO:HALO + tile, :] * convw_ref[pad:pad + 1, :]
    for tap in range(CONV_WIDTH):
        if tap != pad:
            shifted = pltpu.roll(ext, (rows - (tap - pad)) % rows, 0)
            conv = conv + shifted[HALO:HALO + tile, :] * convw_ref[tap:tap + 1, :]
    c_bf = _silu(conv).astype(BF16)
    c_ref[0] = c_bf
    szp_ref[0] = _silu(project(u, 1)).astype(BF16)

    gates = bg_ref[...] + jnp.dot(c_bf, wg_ref[0], preferred_element_type=F32) \
        + jnp.dot(mx_bf, wg_ref[1], preferred_element_type=F32)

    sub = lax.broadcasted_iota(jnp.int32, (CHUNK, DIR_LANES), 1) % GROUP_LANES
    for d, (cols_ref, rows_ref) in enumerate(((colsf_ref, rowsf_ref), (colsb_ref, rowsb_ref))):
        ds = slice(d * DIR_LANES, (d + 1) * DIR_LANES)
        packed = gates[0:CHUNK, ds]
        for ch in range(1, n_chunks):
            packed = packed + pltpu.roll(gates[ch * CHUNK:(ch + 1) * CHUNK, ds], ch * GROUP_LANES, 1)
        b = _token_scan(_log_sigmoid(packed), jnp.add, reverse=(d == 1))
        a = packed - pltpu.roll(b, DIR_LANES - (F_LANE - I_LANE), 1)
        cm = _token_scan(a, jnp.maximum, reverse=(d == 1))
        scan = LOG2E * jnp.where(sub < CM_LANE, a,
                                 jnp.where(sub < F_LANE, pltpu.roll(cm, CM_LANE - I_LANE, 1), b))
        for ch in range(n_chunks):
            cols = scan if ch == 0 else pltpu.roll(scan, DIR_LANES - ch * GROUP_LANES, 1)
            cols_ref[0, ch * CHUNK:(ch + 1) * CHUNK, :] = cols
            rows_ref[0, ch] = cols.T[0:SCAN_ROWS, :]
    szm_ref[0] = _silu(project(u, 3)).astype(BF16)

    for h in range(N_HEADS):
        hs = slice(h * HEAD_DIM, (h + 1) * HEAD_DIM)
        q_ref[0, :, hs] = jnp.dot(c_bf[:, hs], wqkv_ref[0, h],
                                  preferred_element_type=F32).astype(BF16)
        v_ref[0, :, hs] = jnp.dot(mx_bf[:, hs], wqkv_ref[2, h],
                                  preferred_element_type=F32).astype(BF16)
        kh = jnp.dot(c_bf[:, hs], wqkv_ref[1, h], preferred_element_type=F32)
        for ch in range(n_chunks):
            kt_ref[0, ch, hs, :] = kh[ch * CHUNK:(ch + 1) * CHUNK, :].T.astype(BF16)

    px_ref[0] = project(u, 0).astype(BF16)


def _lane_bcast(tile, lane):
    return jnp.broadcast_to(tile[:, lane:lane + 1], tile.shape)


def _sweep_state_phase(h, off, carry, q_ref, kt_ref, v_ref, rows_ref, sc_ref, sc_base):
    hs = slice(h * HEAD_DIM, (h + 1) * HEAD_DIM)
    rs = pl.ds(pl.multiple_of(off * CHUNK, CHUNK), CHUNK)
    c_old, n_old, m = carry
    b_last = sc_ref[sc_base + h]
    cm_last = sc_ref[sc_base + N_HEADS + h]
    m_last = jnp.maximum(m, cm_last)

    a_row = rows_ref[0, off, I_LANE + h:I_LANE + h + 1, :]
    ws_row = jnp.exp2(a_row - m_last)
    decay = jnp.exp2(jnp.full((1, HEAD_DIM), m - m_last, F32))

    qh, vh = q_ref[0, rs, hs], v_ref[0, rs, hs]
    kt = kt_ref[0, off, hs, :]
    q_c = jnp.dot(qh, c_old.astype(BF16), preferred_element_type=F32)
    qk = jnp.dot(qh, kt, preferred_element_type=F32)

    ws_bf = ws_row.astype(BF16)
    kv = jnp.dot(kt * ws_bf, vh, preferred_element_type=F32)
    ws8 = jnp.broadcast_to(ws_bf, (8, CHUNK))
    kn = lax.dot_general(ws8, kt, (((1,), (1,)), ((), ())), preferred_element_type=F32)
    new_carry = (decay * c_old + kv, decay * n_old + kn, b_last + m_last)
    return new_carry, (m, a_row, q_c, qk, n_old[0:1, :])


def _sweep_output_phase(h, off, state, q_ref, v_ref, cols_ref, causal, h_ref, inv_ref):
    hs = slice(h * HEAD_DIM, (h + 1) * HEAD_DIM)
    rs = pl.ds(pl.multiple_of(off * CHUNK, CHUNK), CHUNK)
    cols = cols_ref[0, rs, :]
    m, a_row, q_c, qk, n_row = state
    q_n = q_ref[0, rs, hs].astype(F32) * n_row
    q_n = q_n[:, :CHUNK] + q_n[:, CHUNK:]
    big_m = jnp.maximum(_lane_bcast(cols, CM_LANE + h), m)
    dmat = jnp.where(causal, jnp.exp2(a_row - big_m), 0.0)
    inter_w = jnp.exp2(m - big_m)
    exp_neg_mt = jnp.exp2(-(_lane_bcast(cols, F_LANE + h) + big_m))
    s = qk * dmat
    den = jnp.sum(s + inter_w * q_n, axis=-1, keepdims=True)
    inv_ref[0, rs, h:h + 1] = 1.0 / jnp.maximum(jnp.abs(den), exp_neg_mt[:, 0:1])
    num = jnp.dot(s.astype(BF16), v_ref[0, rs, hs], preferred_element_type=F32)
    out = num + jnp.concatenate([inter_w, inter_w], axis=1) * q_c
    h_ref[0, rs, hs] = out.astype(h_ref.dtype)


def _sweep_kernel(sc_ref,
                  qf_ref, ktf_ref, vf_ref, colsf_ref, rowsf_ref,
                  qb_ref, ktb_ref, vb_ref, colsb_ref, rowsb_ref,
                  hf_ref, hb_ref, invf_ref, invb_ref,
                  c_sc, n_sc, m_sc):
    b = pl.program_id(0)
    j = pl.program_id(1)
    nc = pl.num_programs(1) * SWEEP_CHUNKS

    @pl.when(j == 0)
    def _():
        c_sc[...] = jnp.zeros_like(c_sc)
        n_sc[...] = jnp.zeros_like(n_sc)
        for st in range(N_DIRS * N_HEADS):
            m_sc[st] = jnp.float32(0.0)

    per_chunk = N_DIRS * 2 * N_HEADS
    dirs = ((qf_ref, ktf_ref, vf_ref, colsf_ref, rowsf_ref, hf_ref, invf_ref),
            (qb_ref, ktb_ref, vb_ref, colsb_ref, rowsb_ref, hb_ref, invb_ref))
    units = [(d, h) for h in range(N_HEADS) for d in range(N_DIRS)]
    t_idx = lax.broadcasted_iota(jnp.int32, (CHUNK, CHUNK), 0)
    s_idx = lax.broadcasted_iota(jnp.int32, (CHUNK, CHUNK), 1)
    causal = (s_idx <= t_idx, s_idx >= t_idx)

    def one_chunk(sub, _):
        off = (sub, SWEEP_CHUNKS - 1 - sub)
        chunk_f = j * SWEEP_CHUNKS + sub
        base = ((b * nc + chunk_f) * per_chunk,
                (b * nc + (nc - 1 - chunk_f)) * per_chunk + 2 * N_HEADS)
        for g in range(0, len(units), SWEEP_GROUP):
            states = {}
            for d, h in units[g:g + SWEEP_GROUP]:
                st = d * N_HEADS + h
                q_ref, kt_ref, v_ref, _, rows_ref, _, _ = dirs[d]
                carry, states[d, h] = _sweep_state_phase(
                    h, off[d], (c_sc[st], n_sc[st], m_sc[st]), q_ref, kt_ref, v_ref, rows_ref,
                    sc_ref, base[d])
                c_sc[st], n_sc[st], m_sc[st] = carry
            for d, h in units[g:g + SWEEP_GROUP]:
                _sweep_output_phase(h, off[d], states[d, h], dirs[d][0], dirs[d][2], dirs[d][3],
                                    causal[d], dirs[d][5], dirs[d][6])

    lax.fori_loop(0, SWEEP_CHUNKS, one_chunk, None)


def _combine_kernel(hf_ref, hb_ref, invf_ref, invb_ref, c_ref, szm_ref,
                    pxp_ref, px_ref, pxn_ref, szp_ref, x_ref,
                    invc_ref, pmat_ref, poolw_ref, mhw_ref, skipw_ref, wout_ref, gout_ref,
                    o_ref,
                    wout_sc):
    i = pl.program_id(1)
    n_tiles = pl.num_programs(1)
    tile = OUT_TILE

    @pl.when((pl.program_id(0) == 0) & (i == 0))
    def _():
        wout_sc[...] = wout_ref[...].astype(BF16)

    zero_halo = jnp.zeros((HALO, D_POOL), BF16)
    px_main = px_ref[0]
    ext = jnp.concatenate(
        [jnp.where(i == 0, zero_halo, pxp_ref[0]), px_main,
         jnp.where(i == n_tiles - 1, zero_halo, pxn_ref[0]),
         jnp.zeros((CHUNK - 2 * HALO, D_POOL), BF16)], axis=0)

    def branches(r0):
        rs = slice(r0, r0 + OUT_ROWS)

        inv_count = invc_ref[rs, :]
        y_p_parts = []
        for g in range(len(POOL_WINDOWS)):
            gs = slice(g * POOL_GROUP_DIM, (g + 1) * POOL_GROUP_DIM)
            total = jnp.concatenate(
                [jnp.dot(pmat_ref[g], ext[r0 + ch * CHUNK:r0 + (ch + 2) * CHUNK, gs],
                         preferred_element_type=F32) for ch in range(OUT_ROWS // CHUNK)], axis=0)
            pooled = total * inv_count[:, g:g + 1] - px_main[rs, gs].astype(F32)
            mixed = jnp.dot(pooled.astype(BF16), poolw_ref[g], preferred_element_type=F32)
            y_p_parts.append(mixed.astype(BF16) * szp_ref[0, rs, gs])
        y_p = jnp.concatenate(y_p_parts, axis=1)

        y_m_parts = []
        for h in range(N_HEADS):
            hs = slice(h * HEAD_DIM, (h + 1) * HEAD_DIM)
            ht = hf_ref[0, rs, hs].astype(F32) * invf_ref[0, rs, h:h + 1] \
                + hb_ref[0, rs, hs].astype(F32) * invb_ref[0, rs, h:h + 1]
            mu = jnp.mean(ht, axis=-1, keepdims=True)
            dlt = ht - mu
            var = jnp.mean(dlt * dlt, axis=-1, keepdims=True)
            hn = (dlt * lax.rsqrt(var + EPS) * mhw_ref[:, hs]).astype(BF16)
            y_m_parts.append((hn + skipw_ref[:, hs] * c_ref[0, rs, hs]) * szm_ref[0, rs, hs])
        return y_p, jnp.concatenate(y_m_parts, axis=1)

    def project(r0, y_p, y_m):
        rs = slice(r0, r0 + OUT_ROWS)
        hres = x_ref[0, rs, :] + jnp.dot(y_p, wout_sc[0:D_POOL, :], preferred_element_type=F32) \
            + jnp.dot(y_m, wout_sc[D_POOL:, :], preferred_element_type=F32)
        ms = jnp.mean(hres * hres, axis=-1, keepdims=True)
        o_ref[0, rs, :] = hres * lax.rsqrt(ms + EPS) * gout_ref[...]

    starts = list(range(0, tile, OUT_ROWS))
    pending = branches(starts[0])
    for prev, nxt in zip(starts[:-1], starts[1:]):
        upcoming = branches(nxt)
        project(prev, *pending)
        pending = upcoming
    project(starts[-1], *pending)


def _block_diag_tiles(w):
    rows = w.reshape(3 * N_HEADS, HEAD_DIM, QKV_BLOCK)
    col = np.arange(HEAD_DIM)
    spread = jnp.asarray((col[None, :] % QKV_BLOCK == np.arange(QKV_BLOCK)[:, None]), w.dtype)
    tiled = jnp.einsum('tro,oc->trc', rows, spread, precision=lax.Precision.HIGHEST)
    same_block = jnp.asarray(col[:, None] // QKV_BLOCK == col[None, :] // QKV_BLOCK)
    return jnp.where(same_block[None], tiled, 0.0).reshape(3, N_HEADS, HEAD_DIM, HEAD_DIM)


def _gate_weights(w_gates, b_gates, tiles):
    n_gates = N_DIRS * 2 * N_HEADS
    place = np.zeros((n_gates, N_DIRS * DIR_LANES), np.float32)
    for d in range(N_DIRS):
        for g in range(2 * N_HEADS):
            lane = (I_LANE + g) if g < N_HEADS else (F_LANE + g - N_HEADS)
            place[d * 2 * N_HEADS + g, d * DIR_LANES + lane] = 1.0
    place = jnp.asarray(place)
    hi = lax.Precision.HIGHEST
    rows = jnp.transpose(w_gates, (1, 0, 2)).reshape(3, N_HEADS, HEAD_DIM, n_gates)
    per_map = jnp.einsum('ptrc,ptcg->ptrg', tiles, rows, precision=hi)
    folded = jnp.stack([per_map[0] + per_map[1], per_map[2]]).reshape(2, D_MLSTM, n_gates)
    return (jnp.einsum('krg,gl->krl', folded, place, precision=hi),
            jnp.dot(b_gates.reshape(1, n_gates), place, precision=hi))


def _pool_band_matrices():
    t = np.arange(CHUNK)[:, None]
    r = np.arange(2 * CHUNK)[None, :] - HALO
    mats = []
    for w in POOL_WINDOWS:
        left = (w - 1) // 2
        right = w - 1 - left
        mats.append(((r >= t - left) & (r <= t + right)).astype(np.float32))
    return jnp.asarray(np.stack(mats), dtype=BF16)


def _pool_inverse_counts(seq_len):
    t = np.arange(seq_len)
    inv = np.ones((seq_len, 128), np.float32)
    for g, w in enumerate(POOL_WINDOWS):
        left = (w - 1) // 2
        right = w - 1 - left
        count = np.minimum(t + right, seq_len - 1) - np.maximum(t - left, 0) + 1
        inv[:, g] = 1.0 / count
    return jnp.asarray(inv)


def _conv_shift_matrix():
    t = np.arange(CHUNK)[:, None]
    r = np.arange(2 * CHUNK)[None, :] - HALO
    blocks = [r == t + (tap - CONV_WIDTH // 2) for tap in CONV_SIDE_TAPS]
    return jnp.asarray(np.concatenate(blocks, axis=0).astype(np.float32), dtype=BF16)


def _halo_specs(tile, n_halo, col):
    per = tile // HALO
    prev = pl.BlockSpec((1, HALO, D_MODEL), lambda b, i: (b, jnp.maximum(i * per - 1, 0), col))
    nxt = pl.BlockSpec((1, HALO, D_MODEL),
                       lambda b, i: (b, jnp.minimum((i + 1) * per, n_halo - 1), col))
    return prev, nxt


def kernel(x, norm_in_g, w_in, pool_w, pool_scale, conv_w, conv_b, w_q, w_k, w_v, w_gates,
           b_gates, mh_norm_w, skip_w, w_out, norm_out_g):
    B, S, D = x.shape
    assert D == D_MODEL and S % FRONT_TILE == 0 and S % OUT_TILE == 0 and FRONT_TILE % CHUNK == 0
    assert (FRONT_TILE // CHUNK) * GROUP_LANES <= DIR_LANES and CONV_WIDTH // 2 <= HALO
    assert norm_in_g.shape[0] == 1, "single-layer block"
    nc = S // CHUNK
    n_halo = S // HALO
    arb2 = pltpu.CompilerParams(dimension_semantics=("arbitrary", "arbitrary"),
                                vmem_limit_bytes=VMEM_LIMIT)

    tiles = _block_diag_tiles(jnp.stack([w_q[0], w_k[0], w_v[0]]))
    wg, bg = _gate_weights(w_gates[0], b_gates[0], tiles)
    qkv_scale = jnp.asarray([1.0, float(HEAD_DIM) ** -0.5, 1.0], F32).reshape(3, 1, 1, 1)
    wqkv = (tiles * qkv_scale).astype(BF16)
    conv_w8 = jnp.pad(conv_w[0], ((0, 8 - CONV_WIDTH), (0, 0)))

    def const(shape):
        return pl.BlockSpec(shape, lambda b, i: (0,) * len(shape), pipeline_mode=pl.Buffered(1))

    x_prev, x_next = _halo_specs(FRONT_TILE, n_halo, 0)
    front_seq = pl.BlockSpec((1, FRONT_TILE, D), lambda b, i: (b, i, 0))
    front_cols = pl.BlockSpec((1, FRONT_TILE, DIR_LANES), lambda b, i: (b, i, 0))
    cpt = FRONT_TILE // CHUNK
    front_rows = pl.BlockSpec((1, cpt, SCAN_ROWS, CHUNK), lambda b, i: (b, i, 0, 0))
    front_kt = pl.BlockSpec((1, cpt, D, CHUNK), lambda b, i: (b, i, 0, 0))
    seq_bf = jax.ShapeDtypeStruct((B, S, D), BF16)
    cols_shape = jax.ShapeDtypeStruct((B, S, DIR_LANES), F32)
    rows_shape = jax.ShapeDtypeStruct((B, nc, SCAN_ROWS, CHUNK), F32)
    kt_shape = jax.ShapeDtypeStruct((B, nc, D, CHUNK), BF16)
    pool_x, silu_pz, silu_mz, q, kt, v, c, cols_f, cols_b, rows_f, rows_b = pl.pallas_call(
        _front_kernel,
        grid=(B, S // FRONT_TILE),
        in_specs=[x_prev, front_seq, x_next, const((1, D)), const((D, 4 * D)),
                  const((8, D)), const((1, D)),
                  const((3, N_HEADS, HEAD_DIM, HEAD_DIM)),
                  const((2, D, N_DIRS * DIR_LANES)), const((1, N_DIRS * DIR_LANES))],
        out_specs=[front_seq, front_seq, front_seq, front_seq, front_kt, front_seq, front_seq,
                   front_cols, front_cols, front_rows, front_rows],
        out_shape=[seq_bf, seq_bf, seq_bf, seq_bf, kt_shape, seq_bf, seq_bf,
                   cols_shape, cols_shape, rows_shape, rows_shape],
        scratch_shapes=[pltpu.VMEM((D, 4 * D), BF16)],
        compiler_params=arb2,
        name="front",
    )(x, x, x, norm_in_g[0][None, :], w_in[0], conv_w8,
      conv_b[0][None, :], wqkv, wg.astype(BF16), bg)

    def last(rows, lane):
        return jnp.concatenate([rows[:, :, F_LANE:F_LANE + N_HEADS, lane],
                                rows[:, :, CM_LANE:CM_LANE + N_HEADS, lane]], axis=-1)

    chunk_scalars = jnp.concatenate([last(rows_f, CHUNK - 1), last(rows_b, 0)], axis=-1).reshape(-1)

    sweep_steps = nc // SWEEP_CHUNKS
    sweep_rows = SWEEP_CHUNKS * CHUNK

    def sweep_specs(idx):
        seq = pl.BlockSpec((1, sweep_rows, D), lambda b, j: (b, idx(j), 0))
        cols = pl.BlockSpec((1, sweep_rows, DIR_LANES), lambda b, j: (b, idx(j), 0))
        rows = pl.BlockSpec((1, SWEEP_CHUNKS, 8, CHUNK), lambda b, j: (b, idx(j), 0, 0))
        kt_spec = pl.BlockSpec((1, SWEEP_CHUNKS, D, CHUNK), lambda b, j: (b, idx(j), 0, 0))
        inv = pl.BlockSpec((1, sweep_rows, N_HEADS), lambda b, j: (b, idx(j), 0))
        return seq, inv, [seq, kt_spec, seq, cols, rows]

    seq_f, inv_f, in_f = sweep_specs(lambda j: j)
    seq_b, inv_b, in_b = sweep_specs(lambda j: sweep_steps - 1 - j)
    n_state = N_DIRS * N_HEADS
    inv_shape = jax.ShapeDtypeStruct((B, S, N_HEADS), F32)
    h_fwd, h_bwd, invn_fwd, invn_bwd = pl.pallas_call(
        _sweep_kernel,
        grid=(B, sweep_steps),
        in_specs=[pl.BlockSpec(memory_space=pltpu.SMEM)] + in_f + in_b,
        out_specs=[seq_f, seq_b, inv_f, inv_b],
        out_shape=[seq_bf, seq_bf, inv_shape, inv_shape],
        scratch_shapes=[pltpu.VMEM((n_state, HEAD_DIM, HEAD_DIM), F32),
                        pltpu.VMEM((n_state, 8, HEAD_DIM), F32),
                        pltpu.SMEM((n_state,), F32)],
        compiler_params=arb2,
        name="sweep",
    )(chunk_scalars, q, kt, v, cols_f, rows_f, q, kt, v, cols_b, rows_b)

    px_prev, px_next = _halo_specs(OUT_TILE, n_halo, 0)
    pool_w_scaled = pool_w[0] * pool_scale[0].reshape(len(POOL_WINDOWS), 1, POOL_GROUP_DIM)

    out_seq = pl.BlockSpec((1, OUT_TILE, D), lambda b, i: (b, i, 0))
    out_inv = pl.BlockSpec((1, OUT_TILE, N_HEADS), lambda b, i: (b, i, 0))
    out = pl.pallas_call(
        _combine_kernel,
        grid=(B, S // OUT_TILE),
        in_specs=[out_seq, out_seq, out_inv, out_inv, out_seq, out_seq,
                  px_prev, out_seq, px_next, out_seq, out_seq,
                  pl.BlockSpec((OUT_TILE, 128), lambda b, i: (i, 0)),
                  const((len(POOL_WINDOWS), CHUNK, 2 * CHUNK)),
                  const((len(POOL_WINDOWS), POOL_GROUP_DIM, POOL_GROUP_DIM)),
                  const((1, D)), const((1, D)),
                  const((2 * D, D)), const((1, D))],
        out_specs=out_seq,
        out_shape=jax.ShapeDtypeStruct((B, S, D), F32),
        scratch_shapes=[pltpu.VMEM((2 * D, D), BF16)],
        compiler_params=arb2,
        name="combine",
    )(h_fwd, h_bwd, invn_fwd, invn_bwd, c, silu_mz, pool_x, pool_x, pool_x, silu_pz, x,
      _pool_inverse_counts(S), _pool_band_matrices(), pool_w_scaled.astype(BF16),
      mh_norm_w[0][None, :], skip_w[0][None, :].astype(BF16), w_out[0], norm_out_g[None, :])
    return out
```

```python
import numpy as np
import jax
import jax.numpy as jnp
from jax import lax
from jax.experimental import pallas as pl
from jax.experimental.pallas import tpu as pltpu

D_MODEL = 1024
D_POOL = 1024
D_MLSTM = 1024
POOL_WINDOWS = (2, 4, 8, 16)
POOL_GROUP_DIM = D_POOL // len(POOL_WINDOWS)
N_HEADS = 4
HEAD_DIM = 256
QKV_BLOCK = 4
CONV_WIDTH = 5
CHUNK = 128
N_DIRS = 2
EPS = 1e-6
LOG2E = 1.4426950408889634

HALO = 16
DIR_LANES = 128
GROUP_LANES = 16
I_LANE, CM_LANE, F_LANE = 0, 4, 8
SCAN_ROWS = 16
CONV_SIDE_TAPS = (0, 1, 3, 4)
FRONT_TILE = 512
OUT_TILE = 512
OUT_ROWS = 256
SWEEP_GROUP = 4
SWEEP_CHUNKS = 4
VMEM_LIMIT = 58 * 1024 * 1024

F32 = jnp.float32
BF16 = jnp.bfloat16


def _silu(z):
    return z * (1.0 / (1.0 + jnp.exp(-z)))


def _log_sigmoid(g):
    return jnp.minimum(g, 0.0) - jnp.log1p(jnp.exp(-jnp.abs(g)))


def _token_scan(x, op, reverse):
    t = lax.broadcasted_iota(jnp.int32, x.shape, 0)
    k = 1
    while k < CHUNK:
        if reverse:
            shifted = pltpu.roll(x, CHUNK - k, 0)
            valid = t < CHUNK - k
        else:
            shifted = pltpu.roll(x, k, 0)
            valid = t >= k
        x = jnp.where(valid, op(x, shifted), x)
        k *= 2
    return x


def _front_kernel(xp_ref, x_ref, xn_ref, gin_ref, win_ref, convw_ref, convb_ref,
                  wqkv_ref, wg_ref, bg_ref,
                  px_ref, szp_ref, szm_ref, q_ref, kt_ref, v_ref, c_ref,
                  colsf_ref, colsb_ref, rowsf_ref, rowsb_ref,
                  win_sc):
    i = pl.program_id(1)
    n_tiles = pl.num_programs(1)
    tile = FRONT_TILE
    n_chunks = tile // CHUNK

    @pl.when((pl.program_id(0) == 0) & (i == 0))
    def _():
        for group in range(4):
            gs = slice(group * D_MODEL, (group + 1) * D_MODEL)
            win_sc[:, gs] = win_ref[:, gs].astype(BF16)

    x_ext = jnp.concatenate([xp_ref[0], x_ref[0], xn_ref[0]], axis=0)
    ms = jnp.mean(x_ext * x_ext, axis=-1, keepdims=True)
    u_ext = (x_ext * lax.rsqrt(ms + EPS) * gin_ref[...]).astype(BF16)
    u = u_ext[HALO:HALO + tile, :]

    def project(lhs, group):
        return jnp.dot(lhs, win_sc[:, group * D_MODEL:(group + 1) * D_MODEL],
                       preferred_element_type=F32)

    mx32 = project(u_ext, 2)
    mx_bf = mx32[HALO:HALO + tile, :].astype(BF16)
    zero_halo = jnp.zeros((HALO, D_MLSTM), F32)
    rows = tile + 2 * HALO
    ext = jnp.concatenate(
        [jnp.where(i == 0, zero_halo, mx32[0:HALO, :]), mx32[HALO:HALO + tile, :],
         jnp.where(i == n_tiles - 1, zero_halo, mx32[HALO + tile:, :])], axis=0)

    pad = CONV_WIDTH // 2
    conv = convb_ref[...] + ext[HALO:HALO + tile, :] * convw_ref[pad:pad + 1, :]
    for tap in range(CONV_WIDTH):
        if tap != pad:
            shifted = pltpu.roll(ext, (rows - (tap - pad)) % rows, 0)
            conv = conv + shifted[HALO:HALO + tile, :] * convw_ref[tap:tap + 1, :]
    c_bf = _silu(conv).astype(BF16)
    c_ref[0] = c_bf
    szp_ref[0] = _silu(project(u, 1)).astype(BF16)

    gates = bg_ref[...] + jnp.dot(c_bf, wg_ref[0], preferred_element_type=F32) \
        + jnp.dot(mx_bf, wg_ref[1], preferred_element_type=F32)

    sub = lax.broadcasted_iota(jnp.int32, (CHUNK, DIR_LANES), 1) % GROUP_LANES
    for d, (cols_ref, rows_ref) in enumerate(((colsf_ref, rowsf_ref), (colsb_ref, rowsb_ref))):
        ds = slice(d * DIR_LANES, (d + 1) * DIR_LANES)
        packed = gates[0:CHUNK, ds]
        for ch in range(1, n_chunks):
            packed = packed + pltpu.roll(gates[ch * CHUNK:(ch + 1) * CHUNK, ds], ch * GROUP_LANES, 1)
        b = _token_scan(_log_sigmoid(packed), jnp.add, reverse=(d == 1))
        a = packed - pltpu.roll(b, DIR_LANES - (F_LANE - I_LANE), 1)
        cm = _token_scan(a, jnp.maximum, reverse=(d == 1))
        scan = LOG2E * jnp.where(sub < CM_LANE, a,
                                 jnp.where(sub < F_LANE, pltpu.roll(cm, CM_LANE - I_LANE, 1), b))
        for ch in range(n_chunks):
            cols = scan if ch == 0 else pltpu.roll(scan, DIR_LANES - ch * GROUP_LANES, 1)
            cols_ref[0, ch * CHUNK:(ch + 1) * CHUNK, :] = cols
            rows_ref[0, ch] = cols.T[0:SCAN_ROWS, :]
    szm_ref[0] = _silu(project(u, 3)).astype(BF16)

    for h in range(N_HEADS):
        hs = slice(h * HEAD_DIM, (h + 1) * HEAD_DIM)
        q_ref[0, :, hs] = jnp.dot(c_bf[:, hs], wqkv_ref[0, h],
                                  preferred_element_type=F32).astype(BF16)
        v_ref[0, :, hs] = jnp.dot(mx_bf[:, hs], wqkv_ref[2, h],
                                  preferred_element_type=F32).astype(BF16)
        kh = jnp.dot(c_bf[:, hs], wqkv_ref[1, h], preferred_element_type=F32)
        for ch in range(n_chunks):
            kt_ref[0, ch, hs, :] = kh[ch * CHUNK:(ch + 1) * CHUNK, :].T.astype(BF16)

    px_ref[0] = project(u, 0).astype(BF16)


def _lane_bcast(tile, lane):
    return jnp.broadcast_to(tile[:, lane:lane + 1], tile.shape)


def _sweep_state_phase(h, off, carry, q_ref, kt_ref, v_ref, rows_ref, sc_ref, sc_base):
    hs = slice(h * HEAD_DIM, (h + 1) * HEAD_DIM)
    rs = pl.ds(pl.multiple_of(off * CHUNK, CHUNK), CHUNK)
    c_old, n_old, m = carry
    b_last = sc_ref[sc_base + h]
    cm_last = sc_ref[sc_base + N_HEADS + h]
    m_last = jnp.maximum(m, cm_last)

    a_row = rows_ref[0, off, I_LANE + h:I_LANE + h + 1, :]
    ws_row = jnp.exp2(a_row - m_last)
    decay = jnp.exp2(jnp.full((1, HEAD_DIM), m - m_last, F32))

    qh, vh = q_ref[0, rs, hs], v_ref[0, rs, hs]
    kt = kt_ref[0, off, hs, :]
    q_c = jnp.dot(qh, c_old.astype(BF16), preferred_element_type=F32)
    qk = jnp.dot(qh, kt, preferred_element_type=F32)

    ws_bf = ws_row.astype(BF16)
    kv = jnp.dot(kt * ws_bf, vh, preferred_element_type=F32)
    ws8 = jnp.broadcast_to(ws_bf, (8, CHUNK))
    kn = lax.dot_general(ws8, kt, (((1,), (1,)), ((), ())), preferred_element_type=F32)
    new_carry = (decay * c_old + kv, decay * n_old + kn, b_last + m_last)
    return new_carry, (m, a_row, q_c, qk, n_old[0:1, :])


def _sweep_output_phase(h, off, state, q_ref, v_ref, cols_ref, causal, h_ref):
    hs = slice(h * HEAD_DIM, (h + 1) * HEAD_DIM)
    rs = pl.ds(pl.multiple_of(off * CHUNK, CHUNK), CHUNK)
    cols = cols_ref[0, rs, :]
    m, a_row, q_c, qk, n_row = state
    q_n = q_ref[0, rs, hs].astype(F32) * n_row
    q_n = q_n[:, :CHUNK] + q_n[:, CHUNK:]
    big_m = jnp.maximum(_lane_bcast(cols, CM_LANE + h), m)
    dmat = jnp.where(causal, jnp.exp2(a_row - big_m), 0.0)
    inter_w = jnp.exp2(m - big_m)
    exp_neg_mt = jnp.exp2(-(_lane_bcast(cols, F_LANE + h) + big_m))
    s = qk * dmat
    den = jnp.sum(s + inter_w * q_n, axis=-1, keepdims=True)
    inv = 1.0 / jnp.maximum(jnp.abs(den), exp_neg_mt[:, 0:1])
    inv_b = jnp.broadcast_to(inv, (CHUNK, CHUNK))
    num = jnp.dot(s.astype(BF16), v_ref[0, rs, hs], preferred_element_type=F32)
    out = (num + jnp.concatenate([inter_w, inter_w], axis=1) * q_c) \
        * jnp.concatenate([inv_b, inv_b], axis=1)
    h_ref[0, rs, hs] = out.astype(h_ref.dtype)


def _sweep_kernel(sc_ref,
                  qf_ref, ktf_ref, vf_ref, colsf_ref, rowsf_ref,
                  qb_ref, ktb_ref, vb_ref, colsb_ref, rowsb_ref,
                  hf_ref, hb_ref,
                  c_sc, n_sc, m_sc):
    b = pl.program_id(0)
    j = pl.program_id(1)
    nc = pl.num_programs(1) * SWEEP_CHUNKS

    @pl.when(j == 0)
    def _():
        c_sc[...] = jnp.zeros_like(c_sc)
        n_sc[...] = jnp.zeros_like(n_sc)
        for st in range(N_DIRS * N_HEADS):
            m_sc[st] = jnp.float32(0.0)

    per_chunk = N_DIRS * 2 * N_HEADS
    dirs = ((qf_ref, ktf_ref, vf_ref, colsf_ref, rowsf_ref, hf_ref),
            (qb_ref, ktb_ref, vb_ref, colsb_ref, rowsb_ref, hb_ref))
    units = [(d, h) for h in range(N_HEADS) for d in range(N_DIRS)]
    t_idx = lax.broadcasted_iota(jnp.int32, (CHUNK, CHUNK), 0)
    s_idx = lax.broadcasted_iota(jnp.int32, (CHUNK, CHUNK), 1)
    causal = (s_idx <= t_idx, s_idx >= t_idx)

    def one_chunk(sub, _):
        off = (sub, SWEEP_CHUNKS - 1 - sub)
        chunk_f = j * SWEEP_CHUNKS + sub
        base = ((b * nc + chunk_f) * per_chunk,
                (b * nc + (nc - 1 - chunk_f)) * per_chunk + 2 * N_HEADS)
        for g in range(0, len(units), SWEEP_GROUP):
            states = {}
            for d, h in units[g:g + SWEEP_GROUP]:
                st = d * N_HEADS + h
                q_ref, kt_ref, v_ref, _, rows_ref, _ = dirs[d]
                carry, states[d, h] = _sweep_state_phase(
                    h, off[d], (c_sc[st], n_sc[st], m_sc[st]), q_ref, kt_ref, v_ref, rows_ref,
                    sc_ref, base[d])
                c_sc[st], n_sc[st], m_sc[st] = carry
            for d, h in units[g:g + SWEEP_GROUP]:
                _sweep_output_phase(h, off[d], states[d, h], dirs[d][0], dirs[d][2], dirs[d][3],
                                    causal[d], dirs[d][5])

    lax.fori_loop(0, SWEEP_CHUNKS, one_chunk, None)


def _combine_kernel(hf_ref, hb_ref, c_ref, szm_ref, pxp_ref, px_ref, pxn_ref, szp_ref, x_ref,
                    invc_ref, pmat_ref, poolw_ref, mhw_ref, skipw_ref, wout_ref, gout_ref,
                    o_ref,
                    wout_sc):
    i = pl.program_id(1)
    n_tiles = pl.num_programs(1)
    tile = OUT_TILE

    @pl.when((pl.program_id(0) == 0) & (i == 0))
    def _():
        wout_sc[...] = wout_ref[...].astype(BF16)

    zero_halo = jnp.zeros((HALO, D_POOL), BF16)
    px_main = px_ref[0]
    ext = jnp.concatenate(
        [jnp.where(i == 0, zero_halo, pxp_ref[0]), px_main,
         jnp.where(i == n_tiles - 1, zero_halo, pxn_ref[0]),
         jnp.zeros((CHUNK - 2 * HALO, D_POOL), BF16)], axis=0)

    def branches(r0):
        rs = slice(r0, r0 + OUT_ROWS)

        inv_count = invc_ref[rs, :]
        y_p_parts = []
        for g in range(len(POOL_WINDOWS)):
            gs = slice(g * POOL_GROUP_DIM, (g + 1) * POOL_GROUP_DIM)
            total = jnp.concatenate(
                [jnp.dot(pmat_ref[g], ext[r0 + ch * CHUNK:r0 + (ch + 2) * CHUNK, gs],
                         preferred_element_type=F32) for ch in range(OUT_ROWS // CHUNK)], axis=0)
            pooled = total * inv_count[:, g:g + 1] - px_main[rs, gs].astype(F32)
            mixed = jnp.dot(pooled.astype(BF16), poolw_ref[g], preferred_element_type=F32)
            y_p_parts.append(mixed.astype(BF16) * szp_ref[0, rs, gs])
        y_p = jnp.concatenate(y_p_parts, axis=1)

        y_m_parts = []
        for h in range(N_HEADS):
            hs = slice(h * HEAD_DIM, (h + 1) * HEAD_DIM)
            ht = hf_ref[0, rs, hs].astype(F32) + hb_ref[0, rs, hs].astype(F32)
            mu = jnp.mean(ht, axis=-1, keepdims=True)
            dlt = ht - mu
            var = jnp.mean(dlt * dlt, axis=-1, keepdims=True)
            hn = (dlt * lax.rsqrt(var + EPS) * mhw_ref[:, hs]).astype(BF16)
            y_m_parts.append((hn + skipw_ref[:, hs] * c_ref[0, rs, hs]) * szm_ref[0, rs, hs])
        return y_p, jnp.concatenate(y_m_parts, axis=1)

    def project(r0, y_p, y_m):
        rs = slice(r0, r0 + OUT_ROWS)
        hres = x_ref[0, rs, :] + jnp.dot(y_p, wout_sc[0:D_POOL, :], preferred_element_type=F32) \
            + jnp.dot(y_m, wout_sc[D_POOL:, :], preferred_element_type=F32)
        ms = jnp.mean(hres * hres, axis=-1, keepdims=True)
        o_ref[0, rs, :] = hres * lax.rsqrt(ms + EPS) * gout_ref[...]

    starts = list(range(0, tile, OUT_ROWS))
    pending = branches(starts[0])
    for prev, nxt in zip(starts[:-1], starts[1:]):
        upcoming = branches(nxt)
        project(prev, *pending)
        pending = upcoming
    project(starts[-1], *pending)


def _block_diag_tiles(w):
    rows = w.reshape(3 * N_HEADS, HEAD_DIM, QKV_BLOCK)
    col = np.arange(HEAD_DIM)
    spread = jnp.asarray((col[None, :] % QKV_BLOCK == np.arange(QKV_BLOCK)[:, None]), w.dtype)
    tiled = jnp.einsum('tro,oc->trc', rows, spread, precision=lax.Precision.HIGHEST)
    same_block = jnp.asarray(col[:, None] // QKV_BLOCK == col[None, :] // QKV_BLOCK)
    return jnp.where(same_block[None], tiled, 0.0).reshape(3, N_HEADS, HEAD_DIM, HEAD_DIM)


def _gate_weights(w_gates, b_gates, tiles):
    n_gates = N_DIRS * 2 * N_HEADS
    place = np.zeros((n_gates, N_DIRS * DIR_LANES), np.float32)
    for d in range(N_DIRS):
        for g in range(2 * N_HEADS):
            lane = (I_LANE + g) if g < N_HEADS else (F_LANE + g - N_HEADS)
            place[d * 2 * N_HEADS + g, d * DIR_LANES + lane] = 1.0
    place = jnp.asarray(place)
    hi = lax.Precision.HIGHEST
    rows = jnp.transpose(w_gates, (1, 0, 2)).reshape(3, N_HEADS, HEAD_DIM, n_gates)
    per_map = jnp.einsum('ptrc,ptcg->ptrg', tiles, rows, precision=hi)
    folded = jnp.stack([per_map[0] + per_map[1], per_map[2]]).reshape(2, D_MLSTM, n_gates)
    return (jnp.einsum('krg,gl->krl', folded, place, precision=hi),
            jnp.dot(b_gates.reshape(1, n_gates), place, precision=hi))


def _pool_band_matrices():
    t = np.arange(CHUNK)[:, None]
    r = np.arange(2 * CHUNK)[None, :] - HALO
    mats = []
    for w in POOL_WINDOWS:
        left = (w - 1) // 2
        right = w - 1 - left
        mats.append(((r >= t - left) & (r <= t + right)).astype(np.float32))
    return jnp.asarray(np.stack(mats), dtype=BF16)


def _pool_inverse_counts(seq_len):
    t = np.arange(seq_len)
    inv = np.ones((seq_len, 128), np.float32)
    for g, w in enumerate(POOL_WINDOWS):
        left = (w - 1) // 2
        right = w - 1 - left
        count = np.minimum(t + right, seq_len - 1) - np.maximum(t - left, 0) + 1
        inv[:, g] = 1.0 / count
    return jnp.asarray(inv)


def _conv_shift_matrix():
    t = np.arange(CHUNK)[:, None]
    r = np.arange(2 * CHUNK)[None, :] - HALO
    blocks = [r == t + (tap - CONV_WIDTH // 2) for tap in CONV_SIDE_TAPS]
    return jnp.asarray(np.concatenate(blocks, axis=0).astype(np.float32), dtype=BF16)


def _halo_specs(tile, n_halo, col):
    per = tile // HALO
    prev = pl.BlockSpec((1, HALO, D_MODEL), lambda b, i: (b, jnp.maximum(i * per - 1, 0), col))
    nxt = pl.BlockSpec((1, HALO, D_MODEL),
                       lambda b, i: (b, jnp.minimum((i + 1) * per, n_halo - 1), col))
    return prev, nxt


def kernel(x, norm_in_g, w_in, pool_w, pool_scale, conv_w, conv_b, w_q, w_k, w_v, w_gates,
           b_gates, mh_norm_w, skip_w, w_out, norm_out_g):
    B, S, D = x.shape
    assert D == D_MODEL and S % FRONT_TILE == 0 and S % OUT_TILE == 0 and FRONT_TILE % CHUNK == 0
    assert (FRONT_TILE // CHUNK) * GROUP_LANES <= DIR_LANES and CONV_WIDTH // 2 <= HALO
    assert norm_in_g.shape[0] == 1, "single-layer block"
    nc = S // CHUNK
    n_halo = S // HALO
    arb2 = pltpu.CompilerParams(dimension_semantics=("arbitrary", "arbitrary"),
                                vmem_limit_bytes=VMEM_LIMIT)

    tiles = _block_diag_tiles(jnp.stack([w_q[0], w_k[0], w_v[0]]))
    wg, bg = _gate_weights(w_gates[0], b_gates[0], tiles)
    qkv_scale = jnp.asarray([1.0, float(HEAD_DIM) ** -0.5, 1.0], F32).reshape(3, 1, 1, 1)
    wqkv = (tiles * qkv_scale).astype(BF16)
    conv_w8 = jnp.pad(conv_w[0], ((0, 8 - CONV_WIDTH), (0, 0)))

    def const(shape):
        return pl.BlockSpec(shape, lambda b, i: (0,) * len(shape), pipeline_mode=pl.Buffered(1))

    x_prev, x_next = _halo_specs(FRONT_TILE, n_halo, 0)
    front_seq = pl.BlockSpec((1, FRONT_TILE, D), lambda b, i: (b, i, 0))
    front_cols = pl.BlockSpec((1, FRONT_TILE, DIR_LANES), lambda b, i: (b, i, 0))
    cpt = FRONT_TILE // CHUNK
    front_rows = pl.BlockSpec((1, cpt, SCAN_ROWS, CHUNK), lambda b, i: (b, i, 0, 0))
    front_kt = pl.BlockSpec((1, cpt, D, CHUNK), lambda b, i: (b, i, 0, 0))
    seq_bf = jax.ShapeDtypeStruct((B, S, D), BF16)
    cols_shape = jax.ShapeDtypeStruct((B, S, DIR_LANES), F32)
    rows_shape = jax.ShapeDtypeStruct((B, nc, SCAN_ROWS, CHUNK), F32)
    kt_shape = jax.ShapeDtypeStruct((B, nc, D, CHUNK), BF16)
    pool_x, silu_pz, silu_mz, q, kt, v, c, cols_f, cols_b, rows_f, rows_b = pl.pallas_call(
        _front_kernel,
        grid=(B, S // FRONT_TILE),
        in_specs=[x_prev, front_seq, x_next, const((1, D)), const((D, 4 * D)),
                  const((8, D)), const((1, D)),
                  const((3, N_HEADS, HEAD_DIM, HEAD_DIM)),
                  const((2, D, N_DIRS * DIR_LANES)), const((1, N_DIRS * DIR_LANES))],
        out_specs=[front_seq, front_seq, front_seq, front_seq, front_kt, front_seq, front_seq,
                   front_cols, front_cols, front_rows, front_rows],
        out_shape=[seq_bf, seq_bf, seq_bf, seq_bf, kt_shape, seq_bf, seq_bf,
                   cols_shape, cols_shape, rows_shape, rows_shape],
        scratch_shapes=[pltpu.VMEM((D, 4 * D), BF16)],
        compiler_params=arb2,
        name="front",
    )(x, x, x, norm_in_g[0][None, :], w_in[0], conv_w8,
      conv_b[0][None, :], wqkv, wg.astype(BF16), bg)

    def last(rows, lane):
        return jnp.concatenate([rows[:, :, F_LANE:F_LANE + N_HEADS, lane],
                                rows[:, :, CM_LANE:CM_LANE + N_HEADS, lane]], axis=-1)

    chunk_scalars = jnp.concatenate([last(rows_f, CHUNK - 1), last(rows_b, 0)], axis=-1).reshape(-1)

    sweep_steps = nc // SWEEP_CHUNKS
    sweep_rows = SWEEP_CHUNKS * CHUNK

    def sweep_specs(idx):
        seq = pl.BlockSpec((1, sweep_rows, D), lambda b, j: (b, idx(j), 0))
        cols = pl.BlockSpec((1, sweep_rows, DIR_LANES), lambda b, j: (b, idx(j), 0))
        rows = pl.BlockSpec((1, SWEEP_CHUNKS, 8, CHUNK), lambda b, j: (b, idx(j), 0, 0))
        kt_spec = pl.BlockSpec((1, SWEEP_CHUNKS, D, CHUNK), lambda b, j: (b, idx(j), 0, 0))
        return seq, [seq, kt_spec, seq, cols, rows]

    seq_f, in_f = sweep_specs(lambda j: j)
    seq_b, in_b = sweep_specs(lambda j: sweep_steps - 1 - j)
    n_state = N_DIRS * N_HEADS
    h_fwd, h_bwd = pl.pallas_call(
        _sweep_kernel,
        grid=(B, sweep_steps),
        in_specs=[pl.BlockSpec(memory_space=pltpu.SMEM)] + in_f + in_b,
        out_specs=[seq_f, seq_b],
        out_shape=[seq_bf, seq_bf],
        scratch_shapes=[pltpu.VMEM((n_state, HEAD_DIM, HEAD_DIM), F32),
                        pltpu.VMEM((n_state, 8, HEAD_DIM), F32),
                        pltpu.SMEM((n_state,), F32)],
        compiler_params=arb2,
        name="sweep",
    )(chunk_scalars, q, kt, v, cols_f, rows_f, q, kt, v, cols_b, rows_b)

    px_prev, px_next = _halo_specs(OUT_TILE, n_halo, 0)
    pool_w_scaled = pool_w[0] * pool_scale[0].reshape(len(POOL_WINDOWS), 1, POOL_GROUP_DIM)

    out_seq = pl.BlockSpec((1, OUT_TILE, D), lambda b, i: (b, i, 0))
    out = pl.pallas_call(
        _combine_kernel,
        grid=(B, S // OUT_TILE),
        in_specs=[out_seq, out_seq, out_seq, out_seq,
                  px_prev, out_seq, px_next, out_seq, out_seq,
                  pl.BlockSpec((OUT_TILE, 128), lambda b, i: (i, 0)),
                  const((len(POOL_WINDOWS), CHUNK, 2 * CHUNK)),
                  const((len(POOL_WINDOWS), POOL_GROUP_DIM, POOL_GROUP_DIM)),
                  const((1, D)), const((1, D)),
                  const((2 * D, D)), const((1, D))],
        out_specs=out_seq,
        out_shape=jax.ShapeDtypeStruct((B, S, D), F32),
        scratch_shapes=[pltpu.VMEM((2 * D, D), BF16)],
        compiler_params=arb2,
        name="combine",
    )(h_fwd, h_bwd, c, silu_mz, pool_x, pool_x, pool_x, silu_pz, x,
      _pool_inverse_counts(S), _pool_band_matrices(), pool_w_scaled.astype(BF16),
      mh_norm_w[0][None, :], skip_w[0][None, :].astype(BF16), w_out[0], norm_out_g[None, :])
    return out
```

```python
import numpy as np
import jax
import jax.numpy as jnp
from jax import lax
from jax.experimental import pallas as pl
from jax.experimental.pallas import tpu as pltpu

D_MODEL = 1024
D_POOL = 1024
D_MLSTM = 1024
POOL_WINDOWS = (2, 4, 8, 16)
POOL_GROUP_DIM = D_POOL // len(POOL_WINDOWS)
N_HEADS = 4
HEAD_DIM = 256
QKV_BLOCK = 4
CONV_WIDTH = 5
CHUNK = 128
N_DIRS = 2
EPS = 1e-6
LOG2E = 1.4426950408889634

HALO = 16
DIR_LANES = 128
GROUP_LANES = 16
I_LANE, CM_LANE, F_LANE = 0, 4, 8
SCAN_ROWS = 16
CONV_SIDE_TAPS = (0, 1, 3, 4)
FRONT_TILE = 512
OUT_TILE = 512
OUT_ROWS = 256
SWEEP_GROUP = 4
SWEEP_CHUNKS = 8
VMEM_LIMIT = 58 * 1024 * 1024

F32 = jnp.float32
BF16 = jnp.bfloat16


def _silu(z):
    return z * (1.0 / (1.0 + jnp.exp(-z)))


def _log_sigmoid(g):
    return jnp.minimum(g, 0.0) - jnp.log1p(jnp.exp(-jnp.abs(g)))


def _token_scan(x, op, reverse):
    t = lax.broadcasted_iota(jnp.int32, x.shape, 0)
    k = 1
    while k < CHUNK:
        if reverse:
            shifted = pltpu.roll(x, CHUNK - k, 0)
            valid = t < CHUNK - k
        else:
            shifted = pltpu.roll(x, k, 0)
            valid = t >= k
        x = jnp.where(valid, op(x, shifted), x)
        k *= 2
    return x


def _front_kernel(xp_ref, x_ref, xn_ref, gin_ref, win_ref, convw_ref, convb_ref,
                  wqkv_ref, wg_ref, bg_ref,
                  px_ref, szp_ref, szm_ref, q_ref, kt_ref, v_ref, c_ref,
                  colsf_ref, colsb_ref, rowsf_ref, rowsb_ref,
                  win_sc):
    i = pl.program_id(1)
    n_tiles = pl.num_programs(1)
    tile = FRONT_TILE
    n_chunks = tile // CHUNK

    @pl.when((pl.program_id(0) == 0) & (i == 0))
    def _():
        for group in range(4):
            gs = slice(group * D_MODEL, (group + 1) * D_MODEL)
            win_sc[:, gs] = win_ref[:, gs].astype(BF16)

    x_ext = jnp.concatenate([xp_ref[0], x_ref[0], xn_ref[0]], axis=0)
    ms = jnp.mean(x_ext * x_ext, axis=-1, keepdims=True)
    u_ext = (x_ext * lax.rsqrt(ms + EPS) * gin_ref[...]).astype(BF16)
    u = u_ext[HALO:HALO + tile, :]

    def project(lhs, group):
        return jnp.dot(lhs, win_sc[:, group * D_MODEL:(group + 1) * D_MODEL],
                       preferred_element_type=F32)

    mx32 = project(u_ext, 2)
    mx_bf = mx32[HALO:HALO + tile, :].astype(BF16)
    zero_halo = jnp.zeros((HALO, D_MLSTM), F32)
    rows = tile + 2 * HALO
    ext = jnp.concatenate(
        [jnp.where(i == 0, zero_halo, mx32[0:HALO, :]), mx32[HALO:HALO + tile, :],
         jnp.where(i == n_tiles - 1, zero_halo, mx32[HALO + tile:, :])], axis=0)

    pad = CONV_WIDTH // 2
    conv = convb_ref[...] + ext[HALO:HALO + tile, :] * convw_ref[pad:pad + 1, :]
    for tap in range(CONV_WIDTH):
        if tap != pad:
            shifted = pltpu.roll(ext, (rows - (tap - pad)) % rows, 0)
            conv = conv + shifted[HALO:HALO + tile, :] * convw_ref[tap:tap + 1, :]
    c_bf = _silu(conv).astype(BF16)
    c_ref[0] = c_bf
    szp_ref[0] = _silu(project(u, 1)).astype(BF16)

    gates = bg_ref[...] + jnp.dot(c_bf, wg_ref[0], preferred_element_type=F32) \
        + jnp.dot(mx_bf, wg_ref[1], preferred_element_type=F32)

    sub = lax.broadcasted_iota(jnp.int32, (CHUNK, DIR_LANES), 1) % GROUP_LANES
    for d, (cols_ref, rows_ref) in enumerate(((colsf_ref, rowsf_ref), (colsb_ref, rowsb_ref))):
        ds = slice(d * DIR_LANES, (d + 1) * DIR_LANES)
        packed = gates[0:CHUNK, ds]
        for ch in range(1, n_chunks):
            packed = packed + pltpu.roll(gates[ch * CHUNK:(ch + 1) * CHUNK, ds], ch * GROUP_LANES, 1)
        b = _token_scan(_log_sigmoid(packed), jnp.add, reverse=(d == 1))
        a = packed - pltpu.roll(b, DIR_LANES - (F_LANE - I_LANE), 1)
        cm = _token_scan(a, jnp.maximum, reverse=(d == 1))
        scan = LOG2E * jnp.where(sub < CM_LANE, a,
                                 jnp.where(sub < F_LANE, pltpu.roll(cm, CM_LANE - I_LANE, 1), b))
        for ch in range(n_chunks):
            cols = scan if ch == 0 else pltpu.roll(scan, DIR_LANES - ch * GROUP_LANES, 1)
            cols_ref[0, ch * CHUNK:(ch + 1) * CHUNK, :] = cols
            rows_ref[0, ch] = cols.T[0:SCAN_ROWS, :]
    szm_ref[0] = _silu(project(u, 3)).astype(BF16)

    for h in range(N_HEADS):
        hs = slice(h * HEAD_DIM, (h + 1) * HEAD_DIM)
        q_ref[0, :, hs] = jnp.dot(c_bf[:, hs], wqkv_ref[0, h],
                                  preferred_element_type=F32).astype(BF16)
        v_ref[0, :, hs] = jnp.dot(mx_bf[:, hs], wqkv_ref[2, h],
                                  preferred_element_type=F32).astype(BF16)
        kh = jnp.dot(c_bf[:, hs], wqkv_ref[1, h], preferred_element_type=F32)
        for ch in range(n_chunks):
            kt_ref[0, ch, hs, :] = kh[ch * CHUNK:(ch + 1) * CHUNK, :].T.astype(BF16)

    px_ref[0] = project(u, 0).astype(BF16)


def _lane_bcast(tile, lane):
    return jnp.broadcast_to(tile[:, lane:lane + 1], tile.shape)


def _sweep_state_phase(h, off, carry, q_ref, kt_ref, v_ref, rows_ref, sc_ref, sc_base):
    hs = slice(h * HEAD_DIM, (h + 1) * HEAD_DIM)
    rs = pl.ds(pl.multiple_of(off * CHUNK, CHUNK), CHUNK)
    c_old, n_old, m = carry
    b_last = sc_ref[sc_base + h]
    cm_last = sc_ref[sc_base + N_HEADS + h]
    m_last = jnp.maximum(m, cm_last)

    a_row = rows_ref[0, off, I_LANE + h:I_LANE + h + 1, :]
    ws_row = jnp.exp2(a_row - m_last)
    decay = jnp.exp2(jnp.full((1, HEAD_DIM), m - m_last, F32))

    qh, vh = q_ref[0, rs, hs], v_ref[0, rs, hs]
    kt = kt_ref[0, off, hs, :]
    q_c = jnp.dot(qh, c_old.astype(BF16), preferred_element_type=F32)
    qk = jnp.dot(qh, kt, preferred_element_type=F32)

    ws_bf = ws_row.astype(BF16)
    kv = jnp.dot(kt * ws_bf, vh, preferred_element_type=F32)
    ws8 = jnp.broadcast_to(ws_bf, (8, CHUNK))
    kn = lax.dot_general(ws8, kt, (((1,), (1,)), ((), ())), preferred_element_type=F32)
    new_carry = (decay * c_old + kv, decay * n_old + kn, b_last + m_last)
    return new_carry, (m, a_row, q_c, qk, n_old[0:1, :])


def _sweep_output_phase(h, off, state, q_ref, v_ref, cols_ref, causal, h_ref):
    hs = slice(h * HEAD_DIM, (h + 1) * HEAD_DIM)
    rs = pl.ds(pl.multiple_of(off * CHUNK, CHUNK), CHUNK)
    cols = cols_ref[0, rs, :]
    m, a_row, q_c, qk, n_row = state
    q_n = q_ref[0, rs, hs].astype(F32) * n_row
    q_n = q_n[:, :CHUNK] + q_n[:, CHUNK:]
    big_m = jnp.maximum(_lane_bcast(cols, CM_LANE + h), m)
    dmat = jnp.where(causal, jnp.exp2(a_row - big_m), 0.0)
    inter_w = jnp.exp2(m - big_m)
    exp_neg_mt = jnp.exp2(-(_lane_bcast(cols, F_LANE + h) + big_m))
    s = qk * dmat
    den = jnp.sum(s + inter_w * q_n, axis=-1, keepdims=True)
    inv = 1.0 / jnp.maximum(jnp.abs(den), exp_neg_mt[:, 0:1])
    inv_b = jnp.broadcast_to(inv, (CHUNK, CHUNK))
    num = jnp.dot(s.astype(BF16), v_ref[0, rs, hs], preferred_element_type=F32)
    out = (num + jnp.concatenate([inter_w, inter_w], axis=1) * q_c) \
        * jnp.concatenate([inv_b, inv_b], axis=1)
    h_ref[0, rs, hs] = out.astype(h_ref.dtype)


def _sweep_kernel(sc_ref,
                  qf_ref, ktf_ref, vf_ref, colsf_ref, rowsf_ref,
                  qb_ref, ktb_ref, vb_ref, colsb_ref, rowsb_ref,
                  hf_ref, hb_ref,
                  c_sc, n_sc, m_sc):
    b = pl.program_id(0)
    j = pl.program_id(1)
    nc = pl.num_programs(1) * SWEEP_CHUNKS

    @pl.when(j == 0)
    def _():
        c_sc[...] = jnp.zeros_like(c_sc)
        n_sc[...] = jnp.zeros_like(n_sc)
        for st in range(N_DIRS * N_HEADS):
            m_sc[st] = jnp.float32(0.0)

    per_chunk = N_DIRS * 2 * N_HEADS
    dirs = ((qf_ref, ktf_ref, vf_ref, colsf_ref, rowsf_ref, hf_ref),
            (qb_ref, ktb_ref, vb_ref, colsb_ref, rowsb_ref, hb_ref))
    units = [(d, h) for h in range(N_HEADS) for d in range(N_DIRS)]
    t_idx = lax.broadcasted_iota(jnp.int32, (CHUNK, CHUNK), 0)
    s_idx = lax.broadcasted_iota(jnp.int32, (CHUNK, CHUNK), 1)
    causal = (s_idx <= t_idx, s_idx >= t_idx)

    def one_chunk(sub, _):
        off = (sub, SWEEP_CHUNKS - 1 - sub)
        chunk_f = j * SWEEP_CHUNKS + sub
        base = ((b * nc + chunk_f) * per_chunk,
                (b * nc + (nc - 1 - chunk_f)) * per_chunk + 2 * N_HEADS)
        for g in range(0, len(units), SWEEP_GROUP):
            states = {}
            for d, h in units[g:g + SWEEP_GROUP]:
                st = d * N_HEADS + h
                q_ref, kt_ref, v_ref, _, rows_ref, _ = dirs[d]
                carry, states[d, h] = _sweep_state_phase(
                    h, off[d], (c_sc[st], n_sc[st], m_sc[st]), q_ref, kt_ref, v_ref, rows_ref,
                    sc_ref, base[d])
                c_sc[st], n_sc[st], m_sc[st] = carry
            for d, h in units[g:g + SWEEP_GROUP]:
                _sweep_output_phase(h, off[d], states[d, h], dirs[d][0], dirs[d][2], dirs[d][3],
                                    causal[d], dirs[d][5])

    lax.fori_loop(0, SWEEP_CHUNKS, one_chunk, None)


def _combine_kernel(hf_ref, hb_ref, c_ref, szm_ref, pxp_ref, px_ref, pxn_ref, szp_ref, x_ref,
                    invc_ref, pmat_ref, poolw_ref, mhw_ref, skipw_ref, wout_ref, gout_ref,
                    o_ref,
                    wout_sc):
    i = pl.program_id(1)
    n_tiles = pl.num_programs(1)
    tile = OUT_TILE

    @pl.when((pl.program_id(0) == 0) & (i == 0))
    def _():
        wout_sc[...] = wout_ref[...].astype(BF16)

    zero_halo = jnp.zeros((HALO, D_POOL), BF16)
    px_main = px_ref[0]
    ext = jnp.concatenate(
        [jnp.where(i == 0, zero_halo, pxp_ref[0]), px_main,
         jnp.where(i == n_tiles - 1, zero_halo, pxn_ref[0]),
         jnp.zeros((CHUNK - 2 * HALO, D_POOL), BF16)], axis=0)

    def branches(r0):
        rs = slice(r0, r0 + OUT_ROWS)

        inv_count = invc_ref[rs, :]
        y_p_parts = []
        for g in range(len(POOL_WINDOWS)):
            gs = slice(g * POOL_GROUP_DIM, (g + 1) * POOL_GROUP_DIM)
            total = jnp.concatenate(
                [jnp.dot(pmat_ref[g], ext[r0 + ch * CHUNK:r0 + (ch + 2) * CHUNK, gs],
                         preferred_element_type=F32) for ch in range(OUT_ROWS // CHUNK)], axis=0)
            pooled = total * inv_count[:, g:g + 1] - px_main[rs, gs].astype(F32)
            mixed = jnp.dot(pooled.astype(BF16), poolw_ref[g], preferred_element_type=F32)
            y_p_parts.append(mixed.astype(BF16) * szp_ref[0, rs, gs])
        y_p = jnp.concatenate(y_p_parts, axis=1)

        y_m_parts = []
        for h in range(N_HEADS):
            hs = slice(h * HEAD_DIM, (h + 1) * HEAD_DIM)
            ht = hf_ref[0, rs, hs].astype(F32) + hb_ref[0, rs, hs].astype(F32)
            mu = jnp.mean(ht, axis=-1, keepdims=True)
            dlt = ht - mu
            var = jnp.mean(dlt * dlt, axis=-1, keepdims=True)
            hn = (dlt * lax.rsqrt(var + EPS) * mhw_ref[:, hs]).astype(BF16)
            y_m_parts.append((hn + skipw_ref[:, hs] * c_ref[0, rs, hs]) * szm_ref[0, rs, hs])
        return y_p, jnp.concatenate(y_m_parts, axis=1)

    def project(r0, y_p, y_m):
        rs = slice(r0, r0 + OUT_ROWS)
        hres = x_ref[0, rs, :] + jnp.dot(y_p, wout_sc[0:D_POOL, :], preferred_element_type=F32) \
            + jnp.dot(y_m, wout_sc[D_POOL:, :], preferred_element_type=F32)
        ms = jnp.mean(hres * hres, axis=-1, keepdims=True)
        o_ref[0, rs, :] = hres * lax.rsqrt(ms + EPS) * gout_ref[...]

    starts = list(range(0, tile, OUT_ROWS))
    pending = branches(starts[0])
    for prev, nxt in zip(starts[:-1], starts[1:]):
        upcoming = branches(nxt)
        project(prev, *pending)
        pending = upcoming
    project(starts[-1], *pending)


def _block_diag_tiles(w):
    rows = w.reshape(3 * N_HEADS, HEAD_DIM, QKV_BLOCK)
    col = np.arange(HEAD_DIM)
    spread = jnp.asarray((col[None, :] % QKV_BLOCK == np.arange(QKV_BLOCK)[:, None]), w.dtype)
    tiled = jnp.einsum('tro,oc->trc', rows, spread, precision=lax.Precision.HIGHEST)
    same_block = jnp.asarray(col[:, None] // QKV_BLOCK == col[None, :] // QKV_BLOCK)
    return jnp.where(same_block[None], tiled, 0.0).reshape(3, N_HEADS, HEAD_DIM, HEAD_DIM)


def _gate_weights(w_gates, b_gates, tiles):
    n_gates = N_DIRS * 2 * N_HEADS
    place = np.zeros((n_gates, N_DIRS * DIR_LANES), np.float32)
    for d in range(N_DIRS):
        for g in range(2 * N_HEADS):
            lane = (I_LANE + g) if g < N_HEADS else (F_LANE + g - N_HEADS)
            place[d * 2 * N_HEADS + g, d * DIR_LANES + lane] = 1.0
    place = jnp.asarray(place)
    hi = lax.Precision.HIGHEST
    rows = jnp.transpose(w_gates, (1, 0, 2)).reshape(3, N_HEADS, HEAD_DIM, n_gates)
    per_map = jnp.einsum('ptrc,ptcg->ptrg', tiles, rows, precision=hi)
    folded = jnp.stack([per_map[0] + per_map[1], per_map[2]]).reshape(2, D_MLSTM, n_gates)
    return (jnp.einsum('krg,gl->krl', folded, place, precision=hi),
            jnp.dot(b_gates.reshape(1, n_gates), place, precision=hi))


def _pool_band_matrices():
    t = np.arange(CHUNK)[:, None]
    r = np.arange(2 * CHUNK)[None, :] - HALO
    mats = []
    for w in POOL_WINDOWS:
        left = (w - 1) // 2
        right = w - 1 - left
        mats.append(((r >= t - left) & (r <= t + right)).astype(np.float32))
    return jnp.asarray(np.stack(mats), dtype=BF16)


def _pool_inverse_counts(seq_len):
    t = np.arange(seq_len)
    inv = np.ones((seq_len, 128), np.float32)
    for g, w in enumerate(POOL_WINDOWS):
        left = (w - 1) // 2
        right = w - 1 - left
        count = np.minimum(t + right, seq_len - 1) - np.maximum(t - left, 0) + 1
        inv[:, g] = 1.0 / count
    return jnp.asarray(inv)


def _conv_shift_matrix():
    t = np.arange(CHUNK)[:, None]
    r = np.arange(2 * CHUNK)[None, :] - HALO
    blocks = [r == t + (tap - CONV_WIDTH // 2) for tap in CONV_SIDE_TAPS]
    return jnp.asarray(np.concatenate(blocks, axis=0).astype(np.float32), dtype=BF16)


def _halo_specs(tile, n_halo, col):
    per = tile // HALO
    prev = pl.BlockSpec((1, HALO, D_MODEL), lambda b, i: (b, jnp.maximum(i * per - 1, 0), col))
    nxt = pl.BlockSpec((1, HALO, D_MODEL),
                       lambda b, i: (b, jnp.minimum((i + 1) * per, n_halo - 1), col))
    return prev, nxt


def kernel(x, norm_in_g, w_in, pool_w, pool_scale, conv_w, conv_b, w_q, w_k, w_v, w_gates,
           b_gates, mh_norm_w, skip_w, w_out, norm_out_g):
    B, S, D = x.shape
    assert D == D_MODEL and S % FRONT_TILE == 0 and S % OUT_TILE == 0 and FRONT_TILE % CHUNK == 0
    assert (FRONT_TILE // CHUNK) * GROUP_LANES <= DIR_LANES and CONV_WIDTH // 2 <= HALO
    assert norm_in_g.shape[0] == 1, "single-layer block"
    nc = S // CHUNK
    n_halo = S // HALO
    arb2 = pltpu.CompilerParams(dimension_semantics=("arbitrary", "arbitrary"),
                                vmem_limit_bytes=VMEM_LIMIT)

    tiles = _block_diag_tiles(jnp.stack([w_q[0], w_k[0], w_v[0]]))
    wg, bg = _gate_weights(w_gates[0], b_gates[0], tiles)
    qkv_scale = jnp.asarray([1.0, float(HEAD_DIM) ** -0.5, 1.0], F32).reshape(3, 1, 1, 1)
    wqkv = (tiles * qkv_scale).astype(BF16)
    conv_w8 = jnp.pad(conv_w[0], ((0, 8 - CONV_WIDTH), (0, 0)))

    def const(shape):
        return pl.BlockSpec(shape, lambda b, i: (0,) * len(shape), pipeline_mode=pl.Buffered(1))

    x_prev, x_next = _halo_specs(FRONT_TILE, n_halo, 0)
    front_seq = pl.BlockSpec((1, FRONT_TILE, D), lambda b, i: (b, i, 0))
    front_cols = pl.BlockSpec((1, FRONT_TILE, DIR_LANES), lambda b, i: (b, i, 0))
    cpt = FRONT_TILE // CHUNK
    front_rows = pl.BlockSpec((1, cpt, SCAN_ROWS, CHUNK), lambda b, i: (b, i, 0, 0))
    front_kt = pl.BlockSpec((1, cpt, D, CHUNK), lambda b, i: (b, i, 0, 0))
    seq_bf = jax.ShapeDtypeStruct((B, S, D), BF16)
    cols_shape = jax.ShapeDtypeStruct((B, S, DIR_LANES), F32)
    rows_shape = jax.ShapeDtypeStruct((B, nc, SCAN_ROWS, CHUNK), F32)
    kt_shape = jax.ShapeDtypeStruct((B, nc, D, CHUNK), BF16)
    pool_x, silu_pz, silu_mz, q, kt, v, c, cols_f, cols_b, rows_f, rows_b = pl.pallas_call(
        _front_kernel,
        grid=(B, S // FRONT_TILE),
        in_specs=[x_prev, front_seq, x_next, const((1, D)), const((D, 4 * D)),
                  const((8, D)), const((1, D)),
                  const((3, N_HEADS, HEAD_DIM, HEAD_DIM)),
                  const((2, D, N_DIRS * DIR_LANES)), const((1, N_DIRS * DIR_LANES))],
        out_specs=[front_seq, front_seq, front_seq, front_seq, front_kt, front_seq, front_seq,
                   front_cols, front_cols, front_rows, front_rows],
        out_shape=[seq_bf, seq_bf, seq_bf, seq_bf, kt_shape, seq_bf, seq_bf,
                   cols_shape, cols_shape, rows_shape, rows_shape],
        scratch_shapes=[pltpu.VMEM((D, 4 * D), BF16)],
        compiler_params=arb2,
        name="front",
    )(x, x, x, norm_in_g[0][None, :], w_in[0], conv_w8,
      conv_b[0][None, :], wqkv, wg.astype(BF16), bg)

    def last(rows, lane):
        return jnp.concatenate([rows[:, :, F_LANE:F_LANE + N_HEADS, lane],
                                rows[:, :, CM_LANE:CM_LANE + N_HEADS, lane]], axis=-1)

    chunk_scalars = jnp.concatenate([last(rows_f, CHUNK - 1), last(rows_b, 0)], axis=-1).reshape(-1)

    sweep_steps = nc // SWEEP_CHUNKS
    sweep_rows = SWEEP_CHUNKS * CHUNK

    def sweep_specs(idx):
        seq = pl.BlockSpec((1, sweep_rows, D), lambda b, j: (b, idx(j), 0))
        cols = pl.BlockSpec((1, sweep_rows, DIR_LANES), lambda b, j: (b, idx(j), 0))
        rows = pl.BlockSpec((1, SWEEP_CHUNKS, 8, CHUNK), lambda b, j: (b, idx(j), 0, 0))
        kt_spec = pl.BlockSpec((1, SWEEP_CHUNKS, D, CHUNK), lambda b, j: (b, idx(j), 0, 0))
        return seq, [seq, kt_spec, seq, cols, rows]

    seq_f, in_f = sweep_specs(lambda j: j)
    seq_b, in_b = sweep_specs(lambda j: sweep_steps - 1 - j)
    n_state = N_DIRS * N_HEADS
    h_fwd, h_bwd = pl.pallas_call(
        _sweep_kernel,
        grid=(B, sweep_steps),
        in_specs=[pl.BlockSpec(memory_space=pltpu.SMEM)] + in_f + in_b,
        out_specs=[seq_f, seq_b],
        out_shape=[seq_bf, seq_bf],
        scratch_shapes=[pltpu.VMEM((n_state, HEAD_DIM, HEAD_DIM), F32),
                        pltpu.VMEM((n_state, 8, HEAD_DIM), F32),
                        pltpu.SMEM((n_state,), F32)],
        compiler_params=arb2,
        name="sweep",
    )(chunk_scalars, q, kt, v, cols_f, rows_f, q, kt, v, cols_b, rows_b)

    px_prev, px_next = _halo_specs(OUT_TILE, n_halo, 0)
    pool_w_scaled = pool_w[0] * pool_scale[0].reshape(len(POOL_WINDOWS), 1, POOL_GROUP_DIM)

    out_seq = pl.BlockSpec((1, OUT_TILE, D), lambda b, i: (b, i, 0))
    out = pl.pallas_call(
        _combine_kernel,
        grid=(B, S // OUT_TILE),
        in_specs=[out_seq, out_seq, out_seq, out_seq,
                  px_prev, out_seq, px_next, out_seq, out_seq,
                  pl.BlockSpec((OUT_TILE, 128), lambda b, i: (i, 0)),
                  const((len(POOL_WINDOWS), CHUNK, 2 * CHUNK)),
                  const((len(POOL_WINDOWS), POOL_GROUP_DIM, POOL_GROUP_DIM)),
                  const((1, D)), const((1, D)),
                  const((2 * D, D)), const((1, D))],
        out_specs=out_seq,
        out_shape=jax.ShapeDtypeStruct((B, S, D), F32),
        scratch_shapes=[pltpu.VMEM((2 * D, D), BF16)],
        compiler_params=arb2,
        name="combine",
    )(h_fwd, h_bwd, c, silu_mz, pool_x, pool_x, pool_x, silu_pz, x,
      _pool_inverse_counts(S), _pool_band_matrices(), pool_w_scaled.astype(BF16),
      mh_norm_w[0][None, :], skip_w[0][None, :].astype(BF16), w_out[0], norm_out_g[None, :])
    return out
```

```python
import numpy as np
import jax
import jax.numpy as jnp
from jax import lax
from jax.experimental import pallas as pl
from jax.experimental.pallas import tpu as pltpu

D_MODEL = 1024
D_POOL = 1024
D_MLSTM = 1024
POOL_WINDOWS = (2, 4, 8, 16)
POOL_GROUP_DIM = D_POOL // len(POOL_WINDOWS)
N_HEADS = 4
HEAD_DIM = 256
QKV_BLOCK = 4
CONV_WIDTH = 5
CHUNK = 128
N_DIRS = 2
EPS = 1e-6
LOG2E = 1.4426950408889634

LANES, SUBLANES = 128, 8
HALO = 16
DIR_LANES = LANES
GROUP_LANES = 16
I_LANE, CM_LANE, F_LANE = 0, 4, 8
SCAN_ROWS = 16
CONV_SIDE_TAPS = (0, 1, 3, 4)
FRONT_TILE = 512
OUT_TILE = 512
OUT_ROWS = 256
SWEEP_GROUP = 4
SWEEP_CHUNKS = 8
VMEM_LIMIT = 58 * 1024 * 1024

F32 = jnp.float32
BF16 = jnp.bfloat16


def _silu(z):
    return z * (1.0 / (1.0 + jnp.exp(-z)))


def _log_sigmoid(g):
    return jnp.minimum(g, 0.0) - jnp.log1p(jnp.exp(-jnp.abs(g)))


def _token_scan(x, op, reverse):
    t = lax.broadcasted_iota(jnp.int32, x.shape, 0)
    k = 1
    while k < CHUNK:
        if reverse:
            shifted = pltpu.roll(x, CHUNK - k, 0)
            valid = t < CHUNK - k
        else:
            shifted = pltpu.roll(x, k, 0)
            valid = t >= k
        x = jnp.where(valid, op(x, shifted), x)
        k *= 2
    return x


def _front_kernel(xp_ref, x_ref, xn_ref, gin_ref, win_ref, convw_ref, convb_ref,
                  wqkv_ref, wg_ref, bg_ref,
                  px_ref, szp_ref, szm_ref, q_ref, kt_ref, v_ref, c_ref,
                  colsf_ref, colsb_ref, rowsf_ref, rowsb_ref,
                  win_sc):
    i = pl.program_id(1)
    n_tiles = pl.num_programs(1)
    tile = FRONT_TILE
    n_chunks = tile // CHUNK

    @pl.when((pl.program_id(0) == 0) & (i == 0))
    def _():
        for group in range(4):
            gs = slice(group * D_MODEL, (group + 1) * D_MODEL)
            win_sc[:, gs] = win_ref[:, gs].astype(BF16)

    x_ext = jnp.concatenate([xp_ref[0], x_ref[0], xn_ref[0]], axis=0)
    ms = jnp.mean(x_ext * x_ext, axis=-1, keepdims=True)
    u_ext = (x_ext * lax.rsqrt(ms + EPS) * gin_ref[...]).astype(BF16)
    u = u_ext[HALO:HALO + tile, :]

    def project(lhs, group):
        return jnp.dot(lhs, win_sc[:, group * D_MODEL:(group + 1) * D_MODEL],
                       preferred_element_type=F32)

    mx32 = project(u_ext, 2)
    mx_bf = mx32[HALO:HALO + tile, :].astype(BF16)
    szp_ref[0] = _silu(project(u, 1)).astype(BF16)
    zero_halo = jnp.zeros((HALO, D_MLSTM), F32)
    rows = tile + 2 * HALO
    ext = jnp.concatenate(
        [jnp.where(i == 0, zero_halo, mx32[0:HALO, :]), mx32[HALO:HALO + tile, :],
         jnp.where(i == n_tiles - 1, zero_halo, mx32[HALO + tile:, :])], axis=0)

    pad = CONV_WIDTH // 2
    conv = convb_ref[...] + ext[HALO:HALO + tile, :] * convw_ref[pad:pad + 1, :]
    for tap in range(CONV_WIDTH):
        if tap != pad:
            shifted = pltpu.roll(ext, (rows - (tap - pad)) % rows, 0)
            conv = conv + shifted[HALO:HALO + tile, :] * convw_ref[tap:tap + 1, :]
    c_bf = _silu(conv).astype(BF16)
    c_ref[0] = c_bf

    gates = bg_ref[...] + jnp.dot(c_bf, wg_ref[0], preferred_element_type=F32) \
        + jnp.dot(mx_bf, wg_ref[1], preferred_element_type=F32)

    sub = lax.broadcasted_iota(jnp.int32, (CHUNK, DIR_LANES), 1) % GROUP_LANES
    for d, (cols_ref, rows_ref) in enumerate(((colsf_ref, rowsf_ref), (colsb_ref, rowsb_ref))):
        ds = slice(d * DIR_LANES, (d + 1) * DIR_LANES)
        packed = gates[0:CHUNK, ds]
        for ch in range(1, n_chunks):
            packed = packed + pltpu.roll(gates[ch * CHUNK:(ch + 1) * CHUNK, ds], ch * GROUP_LANES, 1)
        b = _token_scan(_log_sigmoid(packed), jnp.add, reverse=(d == 1))
        a = packed - pltpu.roll(b, DIR_LANES - (F_LANE - I_LANE), 1)
        cm = _token_scan(a, jnp.maximum, reverse=(d == 1))
        scan = LOG2E * jnp.where(sub < CM_LANE, a,
                                 jnp.where(sub < F_LANE, pltpu.roll(cm, CM_LANE - I_LANE, 1), b))
        for ch in range(n_chunks):
            cols = scan if ch == 0 else pltpu.roll(scan, DIR_LANES - ch * GROUP_LANES, 1)
            cols_ref[0, ch * CHUNK:(ch + 1) * CHUNK, :] = cols
            rows_ref[0, ch] = cols.T[0:SCAN_ROWS, :]
    szm_ref[0] = _silu(project(u, 3)).astype(BF16)

    for h in range(N_HEADS):
        hs = slice(h * HEAD_DIM, (h + 1) * HEAD_DIM)
        q_ref[0, :, hs] = jnp.dot(c_bf[:, hs], wqkv_ref[0, h],
                                  preferred_element_type=F32).astype(BF16)
        v_ref[0, :, hs] = jnp.dot(mx_bf[:, hs], wqkv_ref[2, h],
                                  preferred_element_type=F32).astype(BF16)
        kh = jnp.dot(c_bf[:, hs], wqkv_ref[1, h], preferred_element_type=F32)
        for ch in range(n_chunks):
            kt_ref[0, ch, hs, :] = kh[ch * CHUNK:(ch + 1) * CHUNK, :].T.astype(BF16)

    px_ref[0] = project(u, 0).astype(BF16)


def _lane_bcast(tile, lane):
    return jnp.broadcast_to(tile[:, lane:lane + 1], tile.shape)


def _sweep_state_phase(h, off, carry, q_ref, kt_ref, v_ref, rows_ref, sc_ref, sc_base):
    hs = slice(h * HEAD_DIM, (h + 1) * HEAD_DIM)
    rs = pl.ds(pl.multiple_of(off * CHUNK, CHUNK), CHUNK)
    c_old, n_old, m = carry
    b_last = sc_ref[sc_base + h]
    cm_last = sc_ref[sc_base + N_HEADS + h]
    m_last = jnp.maximum(m, cm_last)

    a_row = rows_ref[0, off, I_LANE + h:I_LANE + h + 1, :]
    ws_row = jnp.exp2(a_row - m_last)
    decay = jnp.exp2(jnp.full((1, HEAD_DIM), m - m_last, F32))

    qh, vh = q_ref[0, rs, hs], v_ref[0, rs, hs]
    kt = kt_ref[0, off, hs, :]
    q_c = jnp.dot(qh, c_old.astype(BF16), preferred_element_type=F32)
    qk = jnp.dot(qh, kt, preferred_element_type=F32)

    ws_bf = ws_row.astype(BF16)
    kv = jnp.dot(kt * ws_bf, vh, preferred_element_type=F32)
    ws8 = jnp.broadcast_to(ws_bf, (SUBLANES, CHUNK))
    kn = lax.dot_general(ws8, kt, (((1,), (1,)), ((), ())), preferred_element_type=F32)
    new_carry = (decay * c_old + kv, decay * n_old + kn, b_last + m_last)
    return new_carry, (m, a_row, q_c, qk, n_old[0:1, :])


def _sweep_output_phase(h, off, state, q_ref, v_ref, cols_ref, causal, h_ref):
    hs = slice(h * HEAD_DIM, (h + 1) * HEAD_DIM)
    rs = pl.ds(pl.multiple_of(off * CHUNK, CHUNK), CHUNK)
    cols = cols_ref[0, rs, :]
    m, a_row, q_c, qk, n_row = state
    q_n = q_ref[0, rs, hs].astype(F32) * n_row
    q_n = q_n[:, :CHUNK] + q_n[:, CHUNK:]
    big_m = jnp.maximum(_lane_bcast(cols, CM_LANE + h), m)
    dmat = jnp.where(causal, jnp.exp2(a_row - big_m), 0.0)
    inter_w = jnp.exp2(m - big_m)
    exp_neg_mt = jnp.exp2(-(_lane_bcast(cols, F_LANE + h) + big_m))
    s = qk * dmat
    den = jnp.sum(s + inter_w * q_n, axis=-1, keepdims=True)
    inv = 1.0 / jnp.maximum(jnp.abs(den), exp_neg_mt[:, 0:1])
    inv_b = jnp.broadcast_to(inv, (CHUNK, CHUNK))
    num = jnp.dot(s.astype(BF16), v_ref[0, rs, hs], preferred_element_type=F32)
    out = (num + jnp.concatenate([inter_w, inter_w], axis=1) * q_c) \
        * jnp.concatenate([inv_b, inv_b], axis=1)
    h_ref[0, rs, hs] = out.astype(h_ref.dtype)


def _sweep_kernel(sc_ref,
                  qf_ref, ktf_ref, vf_ref, colsf_ref, rowsf_ref,
                  qb_ref, ktb_ref, vb_ref, colsb_ref, rowsb_ref,
                  hf_ref, hb_ref,
                  c_sc, n_sc, m_sc):
    b = pl.program_id(0)
    j = pl.program_id(1)
    nc = pl.num_programs(1) * SWEEP_CHUNKS

    @pl.when(j == 0)
    def _():
        c_sc[...] = jnp.zeros_like(c_sc)
        n_sc[...] = jnp.zeros_like(n_sc)
        for st in range(N_DIRS * N_HEADS):
            m_sc[st] = jnp.float32(0.0)

    per_chunk = N_DIRS * 2 * N_HEADS
    dirs = ((qf_ref, ktf_ref, vf_ref, colsf_ref, rowsf_ref, hf_ref),
            (qb_ref, ktb_ref, vb_ref, colsb_ref, rowsb_ref, hb_ref))
    units = [(d, h) for h in range(N_HEADS) for d in range(N_DIRS)]
    t_idx = lax.broadcasted_iota(jnp.int32, (CHUNK, CHUNK), 0)
    s_idx = lax.broadcasted_iota(jnp.int32, (CHUNK, CHUNK), 1)
    causal = (s_idx <= t_idx, s_idx >= t_idx)

    def one_chunk(sub, _):
        off = (sub, SWEEP_CHUNKS - 1 - sub)
        chunk_f = j * SWEEP_CHUNKS + sub
        base = ((b * nc + chunk_f) * per_chunk,
                (b * nc + (nc - 1 - chunk_f)) * per_chunk + 2 * N_HEADS)
        for g in range(0, len(units), SWEEP_GROUP):
            states = {}
            for d, h in units[g:g + SWEEP_GROUP]:
                st = d * N_HEADS + h
                q_ref, kt_ref, v_ref, _, rows_ref, _ = dirs[d]
                carry, states[d, h] = _sweep_state_phase(
                    h, off[d], (c_sc[st], n_sc[st], m_sc[st]), q_ref, kt_ref, v_ref, rows_ref,
                    sc_ref, base[d])
                c_sc[st], n_sc[st], m_sc[st] = carry
            for d, h in units[g:g + SWEEP_GROUP]:
                _sweep_output_phase(h, off[d], states[d, h], dirs[d][0], dirs[d][2], dirs[d][3],
                                    causal[d], dirs[d][5])

    lax.fori_loop(0, SWEEP_CHUNKS, one_chunk, None)


def _combine_kernel(hf_ref, hb_ref, c_ref, szm_ref, pxp_ref, px_ref, pxn_ref, szp_ref, x_ref,
                    invc_ref, pmat_ref, poolw_ref, mhw_ref, skipw_ref, wout_ref, gout_ref,
                    o_ref,
                    wout_sc):
    i = pl.program_id(1)
    n_tiles = pl.num_programs(1)
    tile = OUT_TILE

    @pl.when((pl.program_id(0) == 0) & (i == 0))
    def _():
        wout_sc[...] = wout_ref[...].astype(BF16)

    zero_halo = jnp.zeros((HALO, D_POOL), BF16)
    px_main = px_ref[0]
    ext = jnp.concatenate(
        [jnp.where(i == 0, zero_halo, pxp_ref[0]), px_main,
         jnp.where(i == n_tiles - 1, zero_halo, pxn_ref[0]),
         jnp.zeros((CHUNK - 2 * HALO, D_POOL), BF16)], axis=0)

    def branches(r0):
        rs = slice(r0, r0 + OUT_ROWS)

        inv_count = invc_ref[rs, :]
        y_p_parts = []
        for g in range(len(POOL_WINDOWS)):
            gs = slice(g * POOL_GROUP_DIM, (g + 1) * POOL_GROUP_DIM)
            total = jnp.concatenate(
                [jnp.dot(pmat_ref[g], ext[r0 + ch * CHUNK:r0 + (ch + 2) * CHUNK, gs],
                         preferred_element_type=F32) for ch in range(OUT_ROWS // CHUNK)], axis=0)
            pooled = total * inv_count[:, g:g + 1] - px_main[rs, gs].astype(F32)
            mixed = jnp.dot(pooled.astype(BF16), poolw_ref[g], preferred_element_type=F32)
            y_p_parts.append(mixed.astype(BF16) * szp_ref[0, rs, gs])
        y_p = jnp.concatenate(y_p_parts, axis=1)

        y_m_parts = []
        for h in range(N_HEADS):
            hs = slice(h * HEAD_DIM, (h + 1) * HEAD_DIM)
            ht = hf_ref[0, rs, hs].astype(F32) + hb_ref[0, rs, hs].astype(F32)
            mu = jnp.mean(ht, axis=-1, keepdims=True)
            dlt = ht - mu
            var = jnp.mean(dlt * dlt, axis=-1, keepdims=True)
            hn = (dlt * lax.rsqrt(var + EPS) * mhw_ref[:, hs]).astype(BF16)
            y_m_parts.append((hn + skipw_ref[:, hs] * c_ref[0, rs, hs]) * szm_ref[0, rs, hs])
        return y_p, jnp.concatenate(y_m_parts, axis=1)

    def project(r0, y_p, y_m):
        rs = slice(r0, r0 + OUT_ROWS)
        hres = x_ref[0, rs, :] + jnp.dot(y_p, wout_sc[0:D_POOL, :], preferred_element_type=F32) \
            + jnp.dot(y_m, wout_sc[D_POOL:, :], preferred_element_type=F32)
        ms = jnp.mean(hres * hres, axis=-1, keepdims=True)
        o_ref[0, rs, :] = hres * lax.rsqrt(ms + EPS) * gout_ref[...]

    starts = list(range(0, tile, OUT_ROWS))
    pending = branches(starts[0])
    for prev, nxt in zip(starts[:-1], starts[1:]):
        upcoming = branches(nxt)
        project(prev, *pending)
        pending = upcoming
    project(starts[-1], *pending)


def _block_diag_tiles(w):
    rows = w.reshape(3 * N_HEADS, HEAD_DIM, QKV_BLOCK)
    col = np.arange(HEAD_DIM)
    spread = jnp.asarray((col[None, :] % QKV_BLOCK == np.arange(QKV_BLOCK)[:, None]), w.dtype)
    tiled = jnp.einsum('tro,oc->trc', rows, spread, precision=lax.Precision.HIGHEST)
    same_block = jnp.asarray(col[:, None] // QKV_BLOCK == col[None, :] // QKV_BLOCK)
    return jnp.where(same_block[None], tiled, 0.0).reshape(3, N_HEADS, HEAD_DIM, HEAD_DIM)


def _gate_weights(w_gates, b_gates, tiles):
    n_gates = N_DIRS * 2 * N_HEADS
    place = np.zeros((n_gates, N_DIRS * DIR_LANES), np.float32)
    for d in range(N_DIRS):
        for g in range(2 * N_HEADS):
            lane = (I_LANE + g) if g < N_HEADS else (F_LANE + g - N_HEADS)
            place[d * 2 * N_HEADS + g, d * DIR_LANES + lane] = 1.0
    place = jnp.asarray(place)
    hi = lax.Precision.HIGHEST
    rows = jnp.transpose(w_gates, (1, 0, 2)).reshape(3, N_HEADS, HEAD_DIM, n_gates)
    per_map = jnp.einsum('ptrc,ptcg->ptrg', tiles, rows, precision=hi)
    folded = jnp.stack([per_map[0] + per_map[1], per_map[2]]).reshape(2, D_MLSTM, n_gates)
    return (jnp.einsum('krg,gl->krl', folded, place, precision=hi),
            jnp.dot(b_gates.reshape(1, n_gates), place, precision=hi))


def _pool_band_matrices():
    t = np.arange(CHUNK)[:, None]
    r = np.arange(2 * CHUNK)[None, :] - HALO
    mats = []
    for w in POOL_WINDOWS:
        left = (w - 1) // 2
        right = w - 1 - left
        mats.append(((r >= t - left) & (r <= t + right)).astype(np.float32))
    return jnp.asarray(np.stack(mats), dtype=BF16)


def _pool_inverse_counts(seq_len):
    t = np.arange(seq_len)
    inv = np.ones((seq_len, LANES), np.float32)
    for g, w in enumerate(POOL_WINDOWS):
        left = (w - 1) // 2
        right = w - 1 - left
        count = np.minimum(t + right, seq_len - 1) - np.maximum(t - left, 0) + 1
        inv[:, g] = 1.0 / count
    return jnp.asarray(inv)


def _conv_shift_matrix():
    t = np.arange(CHUNK)[:, None]
    r = np.arange(2 * CHUNK)[None, :] - HALO
    blocks = [r == t + (tap - CONV_WIDTH // 2) for tap in CONV_SIDE_TAPS]
    return jnp.asarray(np.concatenate(blocks, axis=0).astype(np.float32), dtype=BF16)


def _halo_specs(tile, n_halo, col):
    per = tile // HALO
    prev = pl.BlockSpec((1, HALO, D_MODEL), lambda b, i: (b, jnp.maximum(i * per - 1, 0), col))
    nxt = pl.BlockSpec((1, HALO, D_MODEL),
                       lambda b, i: (b, jnp.minimum((i + 1) * per, n_halo - 1), col))
    return prev, nxt


def kernel(x, norm_in_g, w_in, pool_w, pool_scale, conv_w, conv_b, w_q, w_k, w_v, w_gates,
           b_gates, mh_norm_w, skip_w, w_out, norm_out_g):
    B, S, D = x.shape
    assert D == D_MODEL and S % FRONT_TILE == 0 and S % OUT_TILE == 0 and FRONT_TILE % CHUNK == 0
    assert (FRONT_TILE // CHUNK) * GROUP_LANES <= DIR_LANES and CONV_WIDTH // 2 <= HALO
    assert norm_in_g.shape[0] == 1, "single-layer block"
    nc = S // CHUNK
    n_halo = S // HALO
    arb2 = pltpu.CompilerParams(dimension_semantics=("arbitrary", "arbitrary"),
                                vmem_limit_bytes=VMEM_LIMIT)

    tiles = _block_diag_tiles(jnp.stack([w_q[0], w_k[0], w_v[0]]))
    wg, bg = _gate_weights(w_gates[0], b_gates[0], tiles)
    qkv_scale = jnp.asarray([1.0, float(HEAD_DIM) ** -0.5, 1.0], F32).reshape(3, 1, 1, 1)
    wqkv = (tiles * qkv_scale).astype(BF16)
    conv_w8 = jnp.pad(conv_w[0], ((0, SUBLANES - CONV_WIDTH), (0, 0)))

    def const(shape):
        return pl.BlockSpec(shape, lambda b, i: (0,) * len(shape), pipeline_mode=pl.Buffered(1))

    x_prev, x_next = _halo_specs(FRONT_TILE, n_halo, 0)
    front_seq = pl.BlockSpec((1, FRONT_TILE, D), lambda b, i: (b, i, 0))
    front_cols = pl.BlockSpec((1, FRONT_TILE, DIR_LANES), lambda b, i: (b, i, 0))
    cpt = FRONT_TILE // CHUNK
    front_rows = pl.BlockSpec((1, cpt, SCAN_ROWS, CHUNK), lambda b, i: (b, i, 0, 0))
    front_kt = pl.BlockSpec((1, cpt, D, CHUNK), lambda b, i: (b, i, 0, 0))
    seq_bf = jax.ShapeDtypeStruct((B, S, D), BF16)
    cols_shape = jax.ShapeDtypeStruct((B, S, DIR_LANES), F32)
    rows_shape = jax.ShapeDtypeStruct((B, nc, SCAN_ROWS, CHUNK), F32)
    kt_shape = jax.ShapeDtypeStruct((B, nc, D, CHUNK), BF16)
    pool_x, silu_pz, silu_mz, q, kt, v, c, cols_f, cols_b, rows_f, rows_b = pl.pallas_call(
        _front_kernel,
        grid=(B, S // FRONT_TILE),
        in_specs=[x_prev, front_seq, x_next, const((1, D)), const((D, 4 * D)),
                  const((SUBLANES, D)), const((1, D)),
                  const((3, N_HEADS, HEAD_DIM, HEAD_DIM)),
                  const((2, D, N_DIRS * DIR_LANES)), const((1, N_DIRS * DIR_LANES))],
        out_specs=[front_seq, front_seq, front_seq, front_seq, front_kt, front_seq, front_seq,
                   front_cols, front_cols, front_rows, front_rows],
        out_shape=[seq_bf, seq_bf, seq_bf, seq_bf, kt_shape, seq_bf, seq_bf,
                   cols_shape, cols_shape, rows_shape, rows_shape],
        scratch_shapes=[pltpu.VMEM((D, 4 * D), BF16)],
        compiler_params=arb2,
        name="front",
    )(x, x, x, norm_in_g[0][None, :], w_in[0], conv_w8,
      conv_b[0][None, :], wqkv, wg.astype(BF16), bg)

    def last(rows, lane):
        return jnp.concatenate([rows[:, :, F_LANE:F_LANE + N_HEADS, lane],
                                rows[:, :, CM_LANE:CM_LANE + N_HEADS, lane]], axis=-1)

    chunk_scalars = jnp.concatenate([last(rows_f, CHUNK - 1), last(rows_b, 0)], axis=-1).reshape(-1)

    sweep_steps = nc // SWEEP_CHUNKS
    sweep_rows = SWEEP_CHUNKS * CHUNK

    def sweep_specs(idx):
        seq = pl.BlockSpec((1, sweep_rows, D), lambda b, j: (b, idx(j), 0))
        cols = pl.BlockSpec((1, sweep_rows, DIR_LANES), lambda b, j: (b, idx(j), 0))
        rows = pl.BlockSpec((1, SWEEP_CHUNKS, SUBLANES, CHUNK), lambda b, j: (b, idx(j), 0, 0))
        kt_spec = pl.BlockSpec((1, SWEEP_CHUNKS, D, CHUNK), lambda b, j: (b, idx(j), 0, 0))
        return seq, [seq, kt_spec, seq, cols, rows]

    seq_f, in_f = sweep_specs(lambda j: j)
    seq_b, in_b = sweep_specs(lambda j: sweep_steps - 1 - j)
    n_state = N_DIRS * N_HEADS
    h_fwd, h_bwd = pl.pallas_call(
        _sweep_kernel,
        grid=(B, sweep_steps),
        in_specs=[pl.BlockSpec(memory_space=pltpu.SMEM)] + in_f + in_b,
        out_specs=[seq_f, seq_b],
        out_shape=[seq_bf, seq_bf],
        scratch_shapes=[pltpu.VMEM((n_state, HEAD_DIM, HEAD_DIM), F32),
                        pltpu.VMEM((n_state, SUBLANES, HEAD_DIM), F32),
                        pltpu.SMEM((n_state,), F32)],
        compiler_params=arb2,
        name="sweep",
    )(chunk_scalars, q, kt, v, cols_f, rows_f, q, kt, v, cols_b, rows_b)

    px_prev, px_next = _halo_specs(OUT_TILE, n_halo, 0)
    pool_w_scaled = pool_w[0] * pool_scale[0].reshape(len(POOL_WINDOWS), 1, POOL_GROUP_DIM)

    out_seq = pl.BlockSpec((1, OUT_TILE, D), lambda b, i: (b, i, 0))
    out = pl.pallas_call(
        _combine_kernel,
        grid=(B, S // OUT_TILE),
        in_specs=[out_seq, out_seq, out_seq, out_seq,
                  px_prev, out_seq, px_next, out_seq, out_seq,
                  pl.BlockSpec((OUT_TILE, LANES), lambda b, i: (i, 0)),
                  const((len(POOL_WINDOWS), CHUNK, 2 * CHUNK)),
                  const((len(POOL_WINDOWS), POOL_GROUP_DIM, POOL_GROUP_DIM)),
                  const((1, D)), const((1, D)),
                  const((2 * D, D)), const((1, D))],
        out_specs=out_seq,
        out_shape=jax.ShapeDtypeStruct((B, S, D), F32),
        scratch_shapes=[pltpu.VMEM((2 * D, D), BF16)],
        compiler_params=arb2,
        name="combine",
    )(h_fwd, h_bwd, c, silu_mz, pool_x, pool_x, pool_x, silu_pz, x,
      _pool_inverse_counts(S), _pool_band_matrices(), pool_w_scaled.astype(BF16),
      mh_norm_w[0][None, :], skip_w[0][None, :].astype(BF16), w_out[0], norm_out_g[None, :])
    return out
```

```python
import numpy as np
import jax
import jax.numpy as jnp
from jax import lax
from jax.experimental import pallas as pl
from jax.experimental.pallas import tpu as pltpu

D_MODEL = 1024
D_POOL = 1024
D_MLSTM = 1024
POOL_WINDOWS = (2, 4, 8, 16)
POOL_GROUP_DIM = D_POOL // len(POOL_WINDOWS)
N_HEADS = 4
HEAD_DIM = 256
QKV_BLOCK = 4
CONV_WIDTH = 5
CHUNK = 128
N_DIRS = 2
EPS = 1e-6
LOG2E = 1.4426950408889634

LANES, SUBLANES = 128, 8
HALO = 16
DIR_LANES = LANES
GROUP_LANES = 16
I_LANE, CM_LANE, F_LANE = 0, 4, 8
SCAN_ROWS = 16
FRONT_TILE = 512
OUT_TILE = 512
OUT_ROWS = 256
SWEEP_GROUP = 4
SWEEP_CHUNKS = 8
VMEM_LIMIT = 58 * 1024 * 1024

F32 = jnp.float32
BF16 = jnp.bfloat16


def _silu(z):
    return z * (1.0 / (1.0 + jnp.exp(-z)))


def _log_sigmoid(g):
    return jnp.minimum(g, 0.0) - jnp.log1p(jnp.exp(-jnp.abs(g)))


def _token_scan(x, op, reverse):
    t = lax.broadcasted_iota(jnp.int32, x.shape, 0)
    k = 1
    while k < CHUNK:
        if reverse:
            shifted = pltpu.roll(x, CHUNK - k, 0)
            valid = t < CHUNK - k
        else:
            shifted = pltpu.roll(x, k, 0)
            valid = t >= k
        x = jnp.where(valid, op(x, shifted), x)
        k *= 2
    return x


def _front_kernel(xp_ref, x_ref, xn_ref, gin_ref, win_ref, convw_ref, convb_ref,
                  wqkv_ref, wg_ref, bg_ref,
                  px_ref, szp_ref, szm_ref, q_ref, kt_ref, v_ref, c_ref,
                  colsf_ref, colsb_ref, rowsf_ref, rowsb_ref,
                  win_sc):
    i = pl.program_id(1)
    n_tiles = pl.num_programs(1)
    tile = FRONT_TILE
    n_chunks = tile // CHUNK

    @pl.when((pl.program_id(0) == 0) & (i == 0))
    def _():
        for group in range(4):
            gs = slice(group * D_MODEL, (group + 1) * D_MODEL)
            win_sc[:, gs] = win_ref[:, gs].astype(BF16)

    x_ext = jnp.concatenate([xp_ref[0], x_ref[0], xn_ref[0]], axis=0)
    ms = jnp.mean(x_ext * x_ext, axis=-1, keepdims=True)
    u_ext = (x_ext * lax.rsqrt(ms + EPS) * gin_ref[...]).astype(BF16)
    u = u_ext[HALO:HALO + tile, :]

    def project(lhs, group):
        return jnp.dot(lhs, win_sc[:, group * D_MODEL:(group + 1) * D_MODEL],
                       preferred_element_type=F32)

    mx32 = project(u_ext, 2)
    mx_bf = mx32[HALO:HALO + tile, :].astype(BF16)
    szp_ref[0] = _silu(project(u, 1)).astype(BF16)
    zero_halo = jnp.zeros((HALO, D_MLSTM), F32)
    rows = tile + 2 * HALO
    ext = jnp.concatenate(
        [jnp.where(i == 0, zero_halo, mx32[0:HALO, :]), mx32[HALO:HALO + tile, :],
         jnp.where(i == n_tiles - 1, zero_halo, mx32[HALO + tile:, :])], axis=0)

    pad = CONV_WIDTH // 2
    conv = convb_ref[...] + ext[HALO:HALO + tile, :] * convw_ref[pad:pad + 1, :]
    for tap in range(CONV_WIDTH):
        if tap != pad:
            shifted = pltpu.roll(ext, (rows - (tap - pad)) % rows, 0)
            conv = conv + shifted[HALO:HALO + tile, :] * convw_ref[tap:tap + 1, :]
    c_bf = _silu(conv).astype(BF16)
    c_ref[0] = c_bf

    gates = bg_ref[...] + jnp.dot(c_bf, wg_ref[0], preferred_element_type=F32) \
        + jnp.dot(mx_bf, wg_ref[1], preferred_element_type=F32)

    sub = lax.broadcasted_iota(jnp.int32, (CHUNK, DIR_LANES), 1) % GROUP_LANES
    for d, (cols_ref, rows_ref) in enumerate(((colsf_ref, rowsf_ref), (colsb_ref, rowsb_ref))):
        ds = slice(d * DIR_LANES, (d + 1) * DIR_LANES)
        packed = gates[0:CHUNK, ds]
        for ch in range(1, n_chunks):
            packed = packed + pltpu.roll(gates[ch * CHUNK:(ch + 1) * CHUNK, ds], ch * GROUP_LANES, 1)
        b = _token_scan(_log_sigmoid(packed), jnp.add, reverse=(d == 1))
        a = packed - pltpu.roll(b, DIR_LANES - (F_LANE - I_LANE), 1)
        cm = _token_scan(a, jnp.maximum, reverse=(d == 1))
        scan = LOG2E * jnp.where(sub < CM_LANE, a,
                                 jnp.where(sub < F_LANE, pltpu.roll(cm, CM_LANE - I_LANE, 1), b))
        for ch in range(n_chunks):
            cols = scan if ch == 0 else pltpu.roll(scan, DIR_LANES - ch * GROUP_LANES, 1)
            cols_ref[0, ch * CHUNK:(ch + 1) * CHUNK, :] = cols
            rows_ref[0, ch] = cols.T[0:SCAN_ROWS, :]
    szm_ref[0] = _silu(project(u, 3)).astype(BF16)

    for h in range(N_HEADS):
        hs = slice(h * HEAD_DIM, (h + 1) * HEAD_DIM)
        q_ref[0, :, hs] = jnp.dot(c_bf[:, hs], wqkv_ref[0, h],
                                  preferred_element_type=F32).astype(BF16)
        v_ref[0, :, hs] = jnp.dot(mx_bf[:, hs], wqkv_ref[2, h],
                                  preferred_element_type=F32).astype(BF16)
        kh = jnp.dot(c_bf[:, hs], wqkv_ref[1, h], preferred_element_type=F32)
        for ch in range(n_chunks):
            kt_ref[0, ch, hs, :] = kh[ch * CHUNK:(ch + 1) * CHUNK, :].T.astype(BF16)

    px_ref[0] = project(u, 0).astype(BF16)


def _lane_bcast(tile, lane):
    return jnp.broadcast_to(tile[:, lane:lane + 1], tile.shape)


def _sweep_state_phase(h, off, carry, q_ref, kt_ref, v_ref, rows_ref, sc_ref, sc_base):
    hs = slice(h * HEAD_DIM, (h + 1) * HEAD_DIM)
    rs = pl.ds(pl.multiple_of(off * CHUNK, CHUNK), CHUNK)
    c_old, n_old, m = carry
    b_last = sc_ref[sc_base + h]
    cm_last = sc_ref[sc_base + N_HEADS + h]
    m_last = jnp.maximum(m, cm_last)

    a_row = rows_ref[0, off, I_LANE + h:I_LANE + h + 1, :]
    ws_row = jnp.exp2(a_row - m_last)
    decay = jnp.exp2(jnp.full((1, HEAD_DIM), m - m_last, F32))

    qh, vh = q_ref[0, rs, hs], v_ref[0, rs, hs]
    kt = kt_ref[0, off, hs, :]
    q_c = jnp.dot(qh, c_old.astype(BF16), preferred_element_type=F32)
    qk = jnp.dot(qh, kt, preferred_element_type=F32)

    ws_bf = ws_row.astype(BF16)
    kv = jnp.dot(kt * ws_bf, vh, preferred_element_type=F32)
    ws8 = jnp.broadcast_to(ws_bf, (SUBLANES, CHUNK))
    kn = lax.dot_general(ws8, kt, (((1,), (1,)), ((), ())), preferred_element_type=F32)
    new_carry = (decay * c_old + kv, decay * n_old + kn, b_last + m_last)
    return new_carry, (m, a_row, q_c, qk, n_old[0:1, :])


def _sweep_output_phase(h, off, state, q_ref, v_ref, cols_ref, causal, h_ref):
    hs = slice(h * HEAD_DIM, (h + 1) * HEAD_DIM)
    rs = pl.ds(pl.multiple_of(off * CHUNK, CHUNK), CHUNK)
    cols = cols_ref[0, rs, :]
    m, a_row, q_c, qk, n_row = state
    q_n = q_ref[0, rs, hs].astype(F32) * n_row
    q_n = q_n[:, :CHUNK] + q_n[:, CHUNK:]
    big_m = jnp.maximum(_lane_bcast(cols, CM_LANE + h), m)
    dmat = jnp.where(causal, jnp.exp2(a_row - big_m), 0.0)
    inter_w = jnp.exp2(m - big_m)
    exp_neg_mt = jnp.exp2(-(_lane_bcast(cols, F_LANE + h) + big_m))
    s = qk * dmat
    den = jnp.sum(s + inter_w * q_n, axis=-1, keepdims=True)
    inv = 1.0 / jnp.maximum(jnp.abs(den), exp_neg_mt[:, 0:1])
    inv_b = jnp.broadcast_to(inv, (CHUNK, CHUNK))
    num = jnp.dot(s.astype(BF16), v_ref[0, rs, hs], preferred_element_type=F32)
    out = (num + jnp.concatenate([inter_w, inter_w], axis=1) * q_c) \
        * jnp.concatenate([inv_b, inv_b], axis=1)
    h_ref[0, rs, hs] = out.astype(h_ref.dtype)


def _sweep_kernel(sc_ref,
                  qf_ref, ktf_ref, vf_ref, colsf_ref, rowsf_ref,
                  qb_ref, ktb_ref, vb_ref, colsb_ref, rowsb_ref,
                  hf_ref, hb_ref,
                  c_sc, n_sc, m_sc):
    b = pl.program_id(0)
    j = pl.program_id(1)
    nc = pl.num_programs(1) * SWEEP_CHUNKS

    @pl.when(j == 0)
    def _():
        c_sc[...] = jnp.zeros_like(c_sc)
        n_sc[...] = jnp.zeros_like(n_sc)
        for st in range(N_DIRS * N_HEADS):
            m_sc[st] = jnp.float32(0.0)

    per_chunk = N_DIRS * 2 * N_HEADS
    dirs = ((qf_ref, ktf_ref, vf_ref, colsf_ref, rowsf_ref, hf_ref),
            (qb_ref, ktb_ref, vb_ref, colsb_ref, rowsb_ref, hb_ref))
    units = [(d, h) for h in range(N_HEADS) for d in range(N_DIRS)]
    t_idx = lax.broadcasted_iota(jnp.int32, (CHUNK, CHUNK), 0)
    s_idx = lax.broadcasted_iota(jnp.int32, (CHUNK, CHUNK), 1)
    causal = (s_idx <= t_idx, s_idx >= t_idx)

    def one_chunk(sub, _):
        off = (sub, SWEEP_CHUNKS - 1 - sub)
        chunk_f = j * SWEEP_CHUNKS + sub
        base = ((b * nc + chunk_f) * per_chunk,
                (b * nc + (nc - 1 - chunk_f)) * per_chunk + 2 * N_HEADS)
        for g in range(0, len(units), SWEEP_GROUP):
            states = {}
            for d, h in units[g:g + SWEEP_GROUP]:
                st = d * N_HEADS + h
                q_ref, kt_ref, v_ref, _, rows_ref, _ = dirs[d]
                carry, states[d, h] = _sweep_state_phase(
                    h, off[d], (c_sc[st], n_sc[st], m_sc[st]), q_ref, kt_ref, v_ref, rows_ref,
                    sc_ref, base[d])
                c_sc[st], n_sc[st], m_sc[st] = carry
            for d, h in units[g:g + SWEEP_GROUP]:
                _sweep_output_phase(h, off[d], states[d, h], dirs[d][0], dirs[d][2], dirs[d][3],
                                    causal[d], dirs[d][5])

    lax.fori_loop(0, SWEEP_CHUNKS, one_chunk, None)


def _combine_kernel(hf_ref, hb_ref, c_ref, szm_ref, pxp_ref, px_ref, pxn_ref, szp_ref, x_ref,
                    invc_ref, pmat_ref, poolw_ref, mhw_ref, skipw_ref, wout_ref, gout_ref,
                    o_ref,
                    wout_sc):
    i = pl.program_id(1)
    n_tiles = pl.num_programs(1)
    tile = OUT_TILE

    @pl.when((pl.program_id(0) == 0) & (i == 0))
    def _():
        wout_sc[...] = wout_ref[...].astype(BF16)

    zero_halo = jnp.zeros((HALO, D_POOL), BF16)
    px_main = px_ref[0]
    ext = jnp.concatenate(
        [jnp.where(i == 0, zero_halo, pxp_ref[0]), px_main,
         jnp.where(i == n_tiles - 1, zero_halo, pxn_ref[0]),
         jnp.zeros((CHUNK - 2 * HALO, D_POOL), BF16)], axis=0)

    def branches(r0):
        rs = slice(r0, r0 + OUT_ROWS)

        inv_count = invc_ref[rs, :]
        y_p_parts = []
        for g in range(len(POOL_WINDOWS)):
            gs = slice(g * POOL_GROUP_DIM, (g + 1) * POOL_GROUP_DIM)
            total = jnp.concatenate(
                [jnp.dot(pmat_ref[g], ext[r0 + ch * CHUNK:r0 + (ch + 2) * CHUNK, gs],
                         preferred_element_type=F32) for ch in range(OUT_ROWS // CHUNK)], axis=0)
            pooled = total * inv_count[:, g:g + 1] - px_main[rs, gs].astype(F32)
            mixed = jnp.dot(pooled.astype(BF16), poolw_ref[g], preferred_element_type=F32)
            y_p_parts.append(mixed.astype(BF16) * szp_ref[0, rs, gs])
        y_p = jnp.concatenate(y_p_parts, axis=1)

        y_m_parts = []
        for h in range(N_HEADS):
            hs = slice(h * HEAD_DIM, (h + 1) * HEAD_DIM)
            ht = hf_ref[0, rs, hs].astype(F32) + hb_ref[0, rs, hs].astype(F32)
            mu = jnp.mean(ht, axis=-1, keepdims=True)
            dlt = ht - mu
            var = jnp.mean(dlt * dlt, axis=-1, keepdims=True)
            hn = (dlt * lax.rsqrt(var + EPS) * mhw_ref[:, hs]).astype(BF16)
            y_m_parts.append((hn + skipw_ref[:, hs] * c_ref[0, rs, hs]) * szm_ref[0, rs, hs])
        return y_p, jnp.concatenate(y_m_parts, axis=1)

    def project(r0, y_p, y_m):
        rs = slice(r0, r0 + OUT_ROWS)
        hres = x_ref[0, rs, :] + jnp.dot(y_p, wout_sc[0:D_POOL, :], preferred_element_type=F32) \
            + jnp.dot(y_m, wout_sc[D_POOL:, :], preferred_element_type=F32)
        ms = jnp.mean(hres * hres, axis=-1, keepdims=True)
        o_ref[0, rs, :] = hres * lax.rsqrt(ms + EPS) * gout_ref[...]

    starts = list(range(0, tile, OUT_ROWS))
    pending = branches(starts[0])
    for prev, nxt in zip(starts[:-1], starts[1:]):
        upcoming = branches(nxt)
        project(prev, *pending)
        pending = upcoming
    project(starts[-1], *pending)


def _block_diag_tiles(w):
    rows = w.reshape(3 * N_HEADS, HEAD_DIM, QKV_BLOCK)
    col = np.arange(HEAD_DIM)
    spread = jnp.asarray((col[None, :] % QKV_BLOCK == np.arange(QKV_BLOCK)[:, None]), w.dtype)
    tiled = jnp.einsum('tro,oc->trc', rows, spread, precision=lax.Precision.HIGHEST)
    same_block = jnp.asarray(col[:, None] // QKV_BLOCK == col[None, :] // QKV_BLOCK)
    return jnp.where(same_block[None], tiled, 0.0).reshape(3, N_HEADS, HEAD_DIM, HEAD_DIM)


def _gate_weights(w_gates, b_gates, tiles):
    n_gates = N_DIRS * 2 * N_HEADS
    place = np.zeros((n_gates, N_DIRS * DIR_LANES), np.float32)
    for d in range(N_DIRS):
        for g in range(2 * N_HEADS):
            lane = (I_LANE + g) if g < N_HEADS else (F_LANE + g - N_HEADS)
            place[d * 2 * N_HEADS + g, d * DIR_LANES + lane] = 1.0
    place = jnp.asarray(place)
    hi = lax.Precision.HIGHEST
    rows = jnp.transpose(w_gates, (1, 0, 2)).reshape(3, N_HEADS, HEAD_DIM, n_gates)
    per_map = jnp.einsum('ptrc,ptcg->ptrg', tiles, rows, precision=hi)
    folded = jnp.stack([per_map[0] + per_map[1], per_map[2]]).reshape(2, D_MLSTM, n_gates)
    return (jnp.einsum('krg,gl->krl', folded, place, precision=hi),
            jnp.dot(b_gates.reshape(1, n_gates), place, precision=hi))


def _pool_band_matrices():
    t = np.arange(CHUNK)[:, None]
    r = np.arange(2 * CHUNK)[None, :] - HALO
    mats = []
    for w in POOL_WINDOWS:
        left = (w - 1) // 2
        right = w - 1 - left
        mats.append(((r >= t - left) & (r <= t + right)).astype(np.float32))
    return jnp.asarray(np.stack(mats), dtype=BF16)


def _pool_inverse_counts(seq_len):
    t = np.arange(seq_len)
    inv = np.ones((seq_len, LANES), np.float32)
    for g, w in enumerate(POOL_WINDOWS):
        left = (w - 1) // 2
        right = w - 1 - left
        count = np.minimum(t + right, seq_len - 1) - np.maximum(t - left, 0) + 1
        inv[:, g] = 1.0 / count
    return jnp.asarray(inv)


def _halo_specs(tile, n_halo, col):
    per = tile // HALO
    prev = pl.BlockSpec((1, HALO, D_MODEL), lambda b, i: (b, jnp.maximum(i * per - 1, 0), col))
    nxt = pl.BlockSpec((1, HALO, D_MODEL),
                       lambda b, i: (b, jnp.minimum((i + 1) * per, n_halo - 1), col))
    return prev, nxt


def kernel(x, norm_in_g, w_in, pool_w, pool_scale, conv_w, conv_b, w_q, w_k, w_v, w_gates,
           b_gates, mh_norm_w, skip_w, w_out, norm_out_g):
    B, S, D = x.shape
    assert D == D_MODEL and S % FRONT_TILE == 0 and S % OUT_TILE == 0 and FRONT_TILE % CHUNK == 0
    assert (FRONT_TILE // CHUNK) * GROUP_LANES <= DIR_LANES and CONV_WIDTH // 2 <= HALO
    assert norm_in_g.shape[0] == 1, "single-layer block"
    nc = S // CHUNK
    n_halo = S // HALO
    arb2 = pltpu.CompilerParams(dimension_semantics=("arbitrary", "arbitrary"),
                                vmem_limit_bytes=VMEM_LIMIT)

    tiles = _block_diag_tiles(jnp.stack([w_q[0], w_k[0], w_v[0]]))
    wg, bg = _gate_weights(w_gates[0], b_gates[0], tiles)
    qkv_scale = jnp.asarray([1.0, float(HEAD_DIM) ** -0.5, 1.0], F32).reshape(3, 1, 1, 1)
    wqkv = (tiles * qkv_scale).astype(BF16)
    conv_w8 = jnp.pad(conv_w[0], ((0, SUBLANES - CONV_WIDTH), (0, 0)))

    def const(shape):
        return pl.BlockSpec(shape, lambda b, i: (0,) * len(shape), pipeline_mode=pl.Buffered(1))

    x_prev, x_next = _halo_specs(FRONT_TILE, n_halo, 0)
    front_seq = pl.BlockSpec((1, FRONT_TILE, D), lambda b, i: (b, i, 0))
    front_cols = pl.BlockSpec((1, FRONT_TILE, DIR_LANES), lambda b, i: (b, i, 0))
    cpt = FRONT_TILE // CHUNK
    front_rows = pl.BlockSpec((1, cpt, SCAN_ROWS, CHUNK), lambda b, i: (b, i, 0, 0))
    front_kt = pl.BlockSpec((1, cpt, D, CHUNK), lambda b, i: (b, i, 0, 0))
    seq_bf = jax.ShapeDtypeStruct((B, S, D), BF16)
    cols_shape = jax.ShapeDtypeStruct((B, S, DIR_LANES), F32)
    rows_shape = jax.ShapeDtypeStruct((B, nc, SCAN_ROWS, CHUNK), F32)
    kt_shape = jax.ShapeDtypeStruct((B, nc, D, CHUNK), BF16)
    pool_x, silu_pz, silu_mz, q, kt, v, c, cols_f, cols_b, rows_f, rows_b = pl.pallas_call(
        _front_kernel,
        grid=(B, S // FRONT_TILE),
        in_specs=[x_prev, front_seq, x_next, const((1, D)), const((D, 4 * D)),
                  const((SUBLANES, D)), const((1, D)),
                  const((3, N_HEADS, HEAD_DIM, HEAD_DIM)),
                  const((2, D, N_DIRS * DIR_LANES)), const((1, N_DIRS * DIR_LANES))],
        out_specs=[front_seq, front_seq, front_seq, front_seq, front_kt, front_seq, front_seq,
                   front_cols, front_cols, front_rows, front_rows],
        out_shape=[seq_bf, seq_bf, seq_bf, seq_bf, kt_shape, seq_bf, seq_bf,
                   cols_shape, cols_shape, rows_shape, rows_shape],
        scratch_shapes=[pltpu.VMEM((D, 4 * D), BF16)],
        compiler_params=arb2,
        name="front",
    )(x, x, x, norm_in_g[0][None, :], w_in[0], conv_w8,
      conv_b[0][None, :], wqkv, wg.astype(BF16), bg)

    def last(rows, lane):
        return jnp.concatenate([rows[:, :, F_LANE:F_LANE + N_HEADS, lane],
                                rows[:, :, CM_LANE:CM_LANE + N_HEADS, lane]], axis=-1)

    chunk_scalars = jnp.concatenate([last(rows_f, CHUNK - 1), last(rows_b, 0)], axis=-1).reshape(-1)

    sweep_steps = nc // SWEEP_CHUNKS
    sweep_rows = SWEEP_CHUNKS * CHUNK

    def sweep_specs(idx):
        seq = pl.BlockSpec((1, sweep_rows, D), lambda b, j: (b, idx(j), 0))
        cols = pl.BlockSpec((1, sweep_rows, DIR_LANES), lambda b, j: (b, idx(j), 0))
        rows = pl.BlockSpec((1, SWEEP_CHUNKS, SUBLANES, CHUNK), lambda b, j: (b, idx(j), 0, 0))
        kt_spec = pl.BlockSpec((1, SWEEP_CHUNKS, D, CHUNK), lambda b, j: (b, idx(j), 0, 0))
        return seq, [seq, kt_spec, seq, cols, rows]

    seq_f, in_f = sweep_specs(lambda j: j)
    seq_b, in_b = sweep_specs(lambda j: sweep_steps - 1 - j)
    n_state = N_DIRS * N_HEADS
    h_fwd, h_bwd = pl.pallas_call(
        _sweep_kernel,
        grid=(B, sweep_steps),
        in_specs=[pl.BlockSpec(memory_space=pltpu.SMEM)] + in_f + in_b,
        out_specs=[seq_f, seq_b],
        out_shape=[seq_bf, seq_bf],
        scratch_shapes=[pltpu.VMEM((n_state, HEAD_DIM, HEAD_DIM), F32),
                        pltpu.VMEM((n_state, SUBLANES, HEAD_DIM), F32),
                        pltpu.SMEM((n_state,), F32)],
        compiler_params=arb2,
        name="sweep",
    )(chunk_scalars, q, kt, v, cols_f, rows_f, q, kt, v, cols_b, rows_b)

    px_prev, px_next = _halo_specs(OUT_TILE, n_halo, 0)
    pool_w_scaled = pool_w[0] * pool_scale[0].reshape(len(POOL_WINDOWS), 1, POOL_GROUP_DIM)

    out_seq = pl.BlockSpec((1, OUT_TILE, D), lambda b, i: (b, i, 0))
    out = pl.pallas_call(
        _combine_kernel,
        grid=(B, S // OUT_TILE),
        in_specs=[out_seq, out_seq, out_seq, out_seq,
                  px_prev, out_seq, px_next, out_seq, out_seq,
                  pl.BlockSpec((OUT_TILE, LANES), lambda b, i: (i, 0)),
                  const((len(POOL_WINDOWS), CHUNK, 2 * CHUNK)),
                  const((len(POOL_WINDOWS), POOL_GROUP_DIM, POOL_GROUP_DIM)),
                  const((1, D)), const((1, D)),
                  const((2 * D, D)), const((1, D))],
        out_specs=out_seq,
        out_shape=jax.ShapeDtypeStruct((B, S, D), F32),
        scratch_shapes=[pltpu.VMEM((2 * D, D), BF16)],
        compiler_params=arb2,
        name="combine",
    )(h_fwd, h_bwd, c, silu_mz, pool_x, pool_x, pool_x, silu_pz, x,
      _pool_inverse_counts(S), _pool_band_matrices(), pool_w_scaled.astype(BF16),
      mh_norm_w[0][None, :], skip_w[0][None, :].astype(BF16), w_out[0], norm_out_g[None, :])
    return out
```

```python
import numpy as np
import jax
import jax.numpy as jnp
from jax import lax
from jax.experimental import pallas as pl
from jax.experimental.pallas import tpu as pltpu

D_MODEL = 1024
D_POOL = 1024
D_MLSTM = 1024
POOL_WINDOWS = (2, 4, 8, 16)
POOL_GROUP_DIM = D_POOL // len(POOL_WINDOWS)
N_HEADS = 4
HEAD_DIM = 256
QKV_BLOCK = 4
CONV_WIDTH = 5
CHUNK = 128
N_DIRS = 2
EPS = 1e-6
LOG2E = 1.4426950408889634

LANES, SUBLANES = 128, 8
HALO = 16
DIR_LANES = LANES
GROUP_LANES = 16
I_LANE, CM_LANE, F_LANE = 0, 4, 8
SCAN_ROWS = 16
FRONT_TILE = 512
OUT_TILE = 1024
OUT_ROWS = 512
SWEEP_GROUP = 4
SWEEP_CHUNKS = 8
VMEM_LIMIT = 58 * 1024 * 1024

F32 = jnp.float32
BF16 = jnp.bfloat16


def _silu(z):
    return z * (1.0 / (1.0 + jnp.exp(-z)))


def _log_sigmoid(g):
    return jnp.minimum(g, 0.0) - jnp.log1p(jnp.exp(-jnp.abs(g)))


def _token_scan(x, op, reverse):
    t = lax.broadcasted_iota(jnp.int32, x.shape, 0)
    k = 1
    while k < CHUNK:
        if reverse:
            shifted = pltpu.roll(x, CHUNK - k, 0)
            valid = t < CHUNK - k
        else:
            shifted = pltpu.roll(x, k, 0)
            valid = t >= k
        x = jnp.where(valid, op(x, shifted), x)
        k *= 2
    return x


def _front_kernel(xp_ref, x_ref, xn_ref, gin_ref, win_ref, convw_ref, convb_ref,
                  wqkv_ref, wg_ref, bg_ref,
                  px_ref, szp_ref, szm_ref, q_ref, kt_ref, v_ref, c_ref,
                  colsf_ref, colsb_ref, rowsf_ref, rowsb_ref,
                  win_sc):
    i = pl.program_id(1)
    n_tiles = pl.num_programs(1)
    tile = FRONT_TILE
    n_chunks = tile // CHUNK

    @pl.when((pl.program_id(0) == 0) & (i == 0))
    def _():
        for group in range(4):
            gs = slice(group * D_MODEL, (group + 1) * D_MODEL)
            win_sc[:, gs] = win_ref[:, gs].astype(BF16)

    x_ext = jnp.concatenate([xp_ref[0], x_ref[0], xn_ref[0]], axis=0)
    ms = jnp.mean(x_ext * x_ext, axis=-1, keepdims=True)
    u_ext = (x_ext * lax.rsqrt(ms + EPS) * gin_ref[...]).astype(BF16)
    u = u_ext[HALO:HALO + tile, :]

    def project(lhs, group):
        return jnp.dot(lhs, win_sc[:, group * D_MODEL:(group + 1) * D_MODEL],
                       preferred_element_type=F32)

    mx32 = project(u_ext, 2)
    mx_bf = mx32[HALO:HALO + tile, :].astype(BF16)
    szp_ref[0] = _silu(project(u, 1)).astype(BF16)
    zero_halo = jnp.zeros((HALO, D_MLSTM), F32)
    rows = tile + 2 * HALO
    ext = jnp.concatenate(
        [jnp.where(i == 0, zero_halo, mx32[0:HALO, :]), mx32[HALO:HALO + tile, :],
         jnp.where(i == n_tiles - 1, zero_halo, mx32[HALO + tile:, :])], axis=0)

    pad = CONV_WIDTH // 2
    conv = convb_ref[...] + ext[HALO:HALO + tile, :] * convw_ref[pad:pad + 1, :]
    for tap in range(CONV_WIDTH):
        if tap != pad:
            shifted = pltpu.roll(ext, (rows - (tap - pad)) % rows, 0)
            conv = conv + shifted[HALO:HALO + tile, :] * convw_ref[tap:tap + 1, :]
    c_bf = _silu(conv).astype(BF16)
    c_ref[0] = c_bf

    gates = bg_ref[...] + jnp.dot(c_bf, wg_ref[0], preferred_element_type=F32) \
        + jnp.dot(mx_bf, wg_ref[1], preferred_element_type=F32)

    sub = lax.broadcasted_iota(jnp.int32, (CHUNK, DIR_LANES), 1) % GROUP_LANES
    for d, (cols_ref, rows_ref) in enumerate(((colsf_ref, rowsf_ref), (colsb_ref, rowsb_ref))):
        ds = slice(d * DIR_LANES, (d + 1) * DIR_LANES)
        packed = gates[0:CHUNK, ds]
        for ch in range(1, n_chunks):
            packed = packed + pltpu.roll(gates[ch * CHUNK:(ch + 1) * CHUNK, ds], ch * GROUP_LANES, 1)
        b = _token_scan(_log_sigmoid(packed), jnp.add, reverse=(d == 1))
        a = packed - pltpu.roll(b, DIR_LANES - (F_LANE - I_LANE), 1)
        cm = _token_scan(a, jnp.maximum, reverse=(d == 1))
        scan = LOG2E * jnp.where(sub < CM_LANE, a,
                                 jnp.where(sub < F_LANE, pltpu.roll(cm, CM_LANE - I_LANE, 1), b))
        for ch in range(n_chunks):
            cols = scan if ch == 0 else pltpu.roll(scan, DIR_LANES - ch * GROUP_LANES, 1)
            cols_ref[0, ch * CHUNK:(ch + 1) * CHUNK, :] = cols
            rows_ref[0, ch] = cols.T[0:SCAN_ROWS, :]
    szm_ref[0] = _silu(project(u, 3)).astype(BF16)

    for h in range(N_HEADS):
        hs = slice(h * HEAD_DIM, (h + 1) * HEAD_DIM)
        q_ref[0, :, hs] = jnp.dot(c_bf[:, hs], wqkv_ref[0, h],
                                  preferred_element_type=F32).astype(BF16)
        v_ref[0, :, hs] = jnp.dot(mx_bf[:, hs], wqkv_ref[2, h],
                                  preferred_element_type=F32).astype(BF16)
        kh = jnp.dot(c_bf[:, hs], wqkv_ref[1, h], preferred_element_type=F32)
        for ch in range(n_chunks):
            kt_ref[0, ch, hs, :] = kh[ch * CHUNK:(ch + 1) * CHUNK, :].T.astype(BF16)

    px_ref[0] = project(u, 0).astype(BF16)


def _lane_bcast(tile, lane):
    return jnp.broadcast_to(tile[:, lane:lane + 1], tile.shape)


def _sweep_state_phase(h, off, carry, q_ref, kt_ref, v_ref, rows_ref, sc_ref, sc_base):
    hs = slice(h * HEAD_DIM, (h + 1) * HEAD_DIM)
    rs = pl.ds(pl.multiple_of(off * CHUNK, CHUNK), CHUNK)
    c_old, n_old, m = carry
    b_last = sc_ref[sc_base + h]
    cm_last = sc_ref[sc_base + N_HEADS + h]
    m_last = jnp.maximum(m, cm_last)

    a_row = rows_ref[0, off, I_LANE + h:I_LANE + h + 1, :]
    ws_row = jnp.exp2(a_row - m_last)
    decay = jnp.exp2(jnp.full((1, HEAD_DIM), m - m_last, F32))

    qh, vh = q_ref[0, rs, hs], v_ref[0, rs, hs]
    kt = kt_ref[0, off, hs, :]
    q_c = jnp.dot(qh, c_old.astype(BF16), preferred_element_type=F32)
    qk = jnp.dot(qh, kt, preferred_element_type=F32)

    ws_bf = ws_row.astype(BF16)
    kv = jnp.dot(kt * ws_bf, vh, preferred_element_type=F32)
    ws8 = jnp.broadcast_to(ws_bf, (SUBLANES, CHUNK))
    kn = lax.dot_general(ws8, kt, (((1,), (1,)), ((), ())), preferred_element_type=F32)
    new_carry = (decay * c_old + kv, decay * n_old + kn, b_last + m_last)
    return new_carry, (m, a_row, q_c, qk, n_old[0:1, :])


def _sweep_output_phase(h, off, state, q_ref, v_ref, cols_ref, causal, h_ref):
    hs = slice(h * HEAD_DIM, (h + 1) * HEAD_DIM)
    rs = pl.ds(pl.multiple_of(off * CHUNK, CHUNK), CHUNK)
    cols = cols_ref[0, rs, :]
    m, a_row, q_c, qk, n_row = state
    q_n = q_ref[0, rs, hs].astype(F32) * n_row
    q_n = q_n[:, :CHUNK] + q_n[:, CHUNK:]
    big_m = jnp.maximum(_lane_bcast(cols, CM_LANE + h), m)
    dmat = jnp.where(causal, jnp.exp2(a_row - big_m), 0.0)
    inter_w = jnp.exp2(m - big_m)
    exp_neg_mt = jnp.exp2(-(_lane_bcast(cols, F_LANE + h) + big_m))
    s = qk * dmat
    den = jnp.sum(s + inter_w * q_n, axis=-1, keepdims=True)
    inv = 1.0 / jnp.maximum(jnp.abs(den), exp_neg_mt[:, 0:1])
    inv_b = jnp.broadcast_to(inv, (CHUNK, CHUNK))
    num = jnp.dot(s.astype(BF16), v_ref[0, rs, hs], preferred_element_type=F32)
    out = (num + jnp.concatenate([inter_w, inter_w], axis=1) * q_c) \
        * jnp.concatenate([inv_b, inv_b], axis=1)
    h_ref[0, rs, hs] = out.astype(h_ref.dtype)


def _sweep_kernel(sc_ref,
                  qf_ref, ktf_ref, vf_ref, colsf_ref, rowsf_ref,
                  qb_ref, ktb_ref, vb_ref, colsb_ref, rowsb_ref,
                  hf_ref, hb_ref,
                  c_sc, n_sc, m_sc):
    b = pl.program_id(0)
    j = pl.program_id(1)
    nc = pl.num_programs(1) * SWEEP_CHUNKS

    @pl.when(j == 0)
    def _():
        c_sc[...] = jnp.zeros_like(c_sc)
        n_sc[...] = jnp.zeros_like(n_sc)
        for st in range(N_DIRS * N_HEADS):
            m_sc[st] = jnp.float32(0.0)

    per_chunk = N_DIRS * 2 * N_HEADS
    dirs = ((qf_ref, ktf_ref, vf_ref, colsf_ref, rowsf_ref, hf_ref),
            (qb_ref, ktb_ref, vb_ref, colsb_ref, rowsb_ref, hb_ref))
    units = [(d, h) for h in range(N_HEADS) for d in range(N_DIRS)]
    t_idx = lax.broadcasted_iota(jnp.int32, (CHUNK, CHUNK), 0)
    s_idx = lax.broadcasted_iota(jnp.int32, (CHUNK, CHUNK), 1)
    causal = (s_idx <= t_idx, s_idx >= t_idx)

    def one_chunk(sub, _):
        off = (sub, SWEEP_CHUNKS - 1 - sub)
        chunk_f = j * SWEEP_CHUNKS + sub
        base = ((b * nc + chunk_f) * per_chunk,
                (b * nc + (nc - 1 - chunk_f)) * per_chunk + 2 * N_HEADS)
        for g in range(0, len(units), SWEEP_GROUP):
            states = {}
            for d, h in units[g:g + SWEEP_GROUP]:
                st = d * N_HEADS + h
                q_ref, kt_ref, v_ref, _, rows_ref, _ = dirs[d]
                carry, states[d, h] = _sweep_state_phase(
                    h, off[d], (c_sc[st], n_sc[st], m_sc[st]), q_ref, kt_ref, v_ref, rows_ref,
                    sc_ref, base[d])
                c_sc[st], n_sc[st], m_sc[st] = carry
            for d, h in units[g:g + SWEEP_GROUP]:
                _sweep_output_phase(h, off[d], states[d, h], dirs[d][0], dirs[d][2], dirs[d][3],
                                    causal[d], dirs[d][5])

    lax.fori_loop(0, SWEEP_CHUNKS, one_chunk, None)


def _combine_kernel(hf_ref, hb_ref, c_ref, szm_ref, pxp_ref, px_ref, pxn_ref, szp_ref, x_ref,
                    invc_ref, pmat_ref, poolw_ref, mhw_ref, skipw_ref, wout_ref, gout_ref,
                    o_ref):
    i = pl.program_id(1)
    n_tiles = pl.num_programs(1)
    tile = OUT_TILE

    zero_halo = jnp.zeros((HALO, D_POOL), BF16)
    px_main = px_ref[0]
    ext = jnp.concatenate(
        [jnp.where(i == 0, zero_halo, pxp_ref[0]), px_main,
         jnp.where(i == n_tiles - 1, zero_halo, pxn_ref[0]),
         jnp.zeros((CHUNK - 2 * HALO, D_POOL), BF16)], axis=0)

    def branches(r0):
        rs = slice(r0, r0 + OUT_ROWS)

        inv_count = invc_ref[rs, :]
        y_p_parts = []
        for g in range(len(POOL_WINDOWS)):
            gs = slice(g * POOL_GROUP_DIM, (g + 1) * POOL_GROUP_DIM)
            total = jnp.concatenate(
                [jnp.dot(pmat_ref[g], ext[r0 + ch * CHUNK:r0 + (ch + 2) * CHUNK, gs],
                         preferred_element_type=F32) for ch in range(OUT_ROWS // CHUNK)], axis=0)
            pooled = total * inv_count[:, g:g + 1] - px_main[rs, gs].astype(F32)
            mixed = jnp.dot(pooled.astype(BF16), poolw_ref[g], preferred_element_type=F32)
            y_p_parts.append(mixed.astype(BF16) * szp_ref[0, rs, gs])
        y_p = jnp.concatenate(y_p_parts, axis=1)

        y_m_parts = []
        for h in range(N_HEADS):
            hs = slice(h * HEAD_DIM, (h + 1) * HEAD_DIM)
            ht = hf_ref[0, rs, hs].astype(F32) + hb_ref[0, rs, hs].astype(F32)
            mu = jnp.mean(ht, axis=-1, keepdims=True)
            dlt = ht - mu
            var = jnp.mean(dlt * dlt, axis=-1, keepdims=True)
            hn = (dlt * lax.rsqrt(var + EPS) * mhw_ref[:, hs]).astype(BF16)
            y_m_parts.append((hn + skipw_ref[:, hs] * c_ref[0, rs, hs]) * szm_ref[0, rs, hs])
        return y_p, jnp.concatenate(y_m_parts, axis=1)

    def project(r0, y_p, y_m):
        rs = slice(r0, r0 + OUT_ROWS)
        hres = x_ref[0, rs, :] + jnp.dot(y_p, wout_ref[0:D_POOL, :], preferred_element_type=F32) \
            + jnp.dot(y_m, wout_ref[D_POOL:, :], preferred_element_type=F32)
        ms = jnp.mean(hres * hres, axis=-1, keepdims=True)
        o_ref[0, rs, :] = hres * lax.rsqrt(ms + EPS) * gout_ref[...]

    starts = list(range(0, tile, OUT_ROWS))
    pending = branches(starts[0])
    for prev, nxt in zip(starts[:-1], starts[1:]):
        upcoming = branches(nxt)
        project(prev, *pending)
        pending = upcoming
    project(starts[-1], *pending)


def _block_diag_tiles(w):
    rows = w.reshape(3 * N_HEADS, HEAD_DIM, QKV_BLOCK)
    col = np.arange(HEAD_DIM)
    spread = jnp.asarray((col[None, :] % QKV_BLOCK == np.arange(QKV_BLOCK)[:, None]), w.dtype)
    tiled = jnp.einsum('tro,oc->trc', rows, spread, precision=lax.Precision.HIGHEST)
    same_block = jnp.asarray(col[:, None] // QKV_BLOCK == col[None, :] // QKV_BLOCK)
    return jnp.where(same_block[None], tiled, 0.0).reshape(3, N_HEADS, HEAD_DIM, HEAD_DIM)


def _gate_weights(w_gates, b_gates, tiles):
    n_gates = N_DIRS * 2 * N_HEADS
    place = np.zeros((n_gates, N_DIRS * DIR_LANES), np.float32)
    for d in range(N_DIRS):
        for g in range(2 * N_HEADS):
            lane = (I_LANE + g) if g < N_HEADS else (F_LANE + g - N_HEADS)
            place[d * 2 * N_HEADS + g, d * DIR_LANES + lane] = 1.0
    place = jnp.asarray(place)
    hi = lax.Precision.HIGHEST
    rows = jnp.transpose(w_gates, (1, 0, 2)).reshape(3, N_HEADS, HEAD_DIM, n_gates)
    per_map = jnp.einsum('ptrc,ptcg->ptrg', tiles, rows, precision=hi)
    folded = jnp.stack([per_map[0] + per_map[1], per_map[2]]).reshape(2, D_MLSTM, n_gates)
    return (jnp.einsum('krg,gl->krl', folded, place, precision=hi),
            jnp.dot(b_gates.reshape(1, n_gates), place, precision=hi))


def _pool_band_matrices():
    t = np.arange(CHUNK)[:, None]
    r = np.arange(2 * CHUNK)[None, :] - HALO
    mats = []
    for w in POOL_WINDOWS:
        left = (w - 1) // 2
        right = w - 1 - left
        mats.append(((r >= t - left) & (r <= t + right)).astype(np.float32))
    return jnp.asarray(np.stack(mats), dtype=BF16)


def _pool_inverse_counts(seq_len):
    t = np.arange(seq_len)
    inv = np.ones((seq_len, LANES), np.float32)
    for g, w in enumerate(POOL_WINDOWS):
        left = (w - 1) // 2
        right = w - 1 - left
        count = np.minimum(t + right, seq_len - 1) - np.maximum(t - left, 0) + 1
        inv[:, g] = 1.0 / count
    return jnp.asarray(inv)


def _halo_specs(tile, n_halo, col):
    per = tile // HALO
    prev = pl.BlockSpec((1, HALO, D_MODEL), lambda b, i: (b, jnp.maximum(i * per - 1, 0), col))
    nxt = pl.BlockSpec((1, HALO, D_MODEL),
                       lambda b, i: (b, jnp.minimum((i + 1) * per, n_halo - 1), col))
    return prev, nxt


def kernel(x, norm_in_g, w_in, pool_w, pool_scale, conv_w, conv_b, w_q, w_k, w_v, w_gates,
           b_gates, mh_norm_w, skip_w, w_out, norm_out_g):
    B, S, D = x.shape
    assert D == D_MODEL and S % FRONT_TILE == 0 and S % OUT_TILE == 0 and FRONT_TILE % CHUNK == 0
    assert (FRONT_TILE // CHUNK) * GROUP_LANES <= DIR_LANES and CONV_WIDTH // 2 <= HALO
    assert norm_in_g.shape[0] == 1, "single-layer block"
    nc = S // CHUNK
    n_halo = S // HALO
    arb2 = pltpu.CompilerParams(dimension_semantics=("arbitrary", "arbitrary"),
                                vmem_limit_bytes=VMEM_LIMIT)

    tiles = _block_diag_tiles(jnp.stack([w_q[0], w_k[0], w_v[0]]))
    wg, bg = _gate_weights(w_gates[0], b_gates[0], tiles)
    qkv_scale = jnp.asarray([1.0, float(HEAD_DIM) ** -0.5, 1.0], F32).reshape(3, 1, 1, 1)
    wqkv = (tiles * qkv_scale).astype(BF16)
    conv_w8 = jnp.pad(conv_w[0], ((0, SUBLANES - CONV_WIDTH), (0, 0)))

    def const(shape):
        return pl.BlockSpec(shape, lambda b, i: (0,) * len(shape), pipeline_mode=pl.Buffered(1))

    x_prev, x_next = _halo_specs(FRONT_TILE, n_halo, 0)
    front_seq = pl.BlockSpec((1, FRONT_TILE, D), lambda b, i: (b, i, 0))
    front_cols = pl.BlockSpec((1, FRONT_TILE, DIR_LANES), lambda b, i: (b, i, 0))
    cpt = FRONT_TILE // CHUNK
    front_rows = pl.BlockSpec((1, cpt, SCAN_ROWS, CHUNK), lambda b, i: (b, i, 0, 0))
    front_kt = pl.BlockSpec((1, cpt, D, CHUNK), lambda b, i: (b, i, 0, 0))
    seq_bf = jax.ShapeDtypeStruct((B, S, D), BF16)
    cols_shape = jax.ShapeDtypeStruct((B, S, DIR_LANES), F32)
    rows_shape = jax.ShapeDtypeStruct((B, nc, SCAN_ROWS, CHUNK), F32)
    kt_shape = jax.ShapeDtypeStruct((B, nc, D, CHUNK), BF16)
    pool_x, silu_pz, silu_mz, q, kt, v, c, cols_f, cols_b, rows_f, rows_b = pl.pallas_call(
        _front_kernel,
        grid=(B, S // FRONT_TILE),
        in_specs=[x_prev, front_seq, x_next, const((1, D)), const((D, 4 * D)),
                  const((SUBLANES, D)), const((1, D)),
                  const((3, N_HEADS, HEAD_DIM, HEAD_DIM)),
                  const((2, D, N_DIRS * DIR_LANES)), const((1, N_DIRS * DIR_LANES))],
        out_specs=[front_seq, front_seq, front_seq, front_seq, front_kt, front_seq, front_seq,
                   front_cols, front_cols, front_rows, front_rows],
        out_shape=[seq_bf, seq_bf, seq_bf, seq_bf, kt_shape, seq_bf, seq_bf,
                   cols_shape, cols_shape, rows_shape, rows_shape],
        scratch_shapes=[pltpu.VMEM((D, 4 * D), BF16)],
        compiler_params=arb2,
        name="front",
    )(x, x, x, norm_in_g[0][None, :], w_in[0], conv_w8,
      conv_b[0][None, :], wqkv, wg.astype(BF16), bg)

    def last(rows, lane):
        return jnp.concatenate([rows[:, :, F_LANE:F_LANE + N_HEADS, lane],
                                rows[:, :, CM_LANE:CM_LANE + N_HEADS, lane]], axis=-1)

    chunk_scalars = jnp.concatenate([last(rows_f, CHUNK - 1), last(rows_b, 0)], axis=-1).reshape(-1)

    sweep_steps = nc // SWEEP_CHUNKS
    sweep_rows = SWEEP_CHUNKS * CHUNK

    def sweep_specs(idx):
        seq = pl.BlockSpec((1, sweep_rows, D), lambda b, j: (b, idx(j), 0))
        cols = pl.BlockSpec((1, sweep_rows, DIR_LANES), lambda b, j: (b, idx(j), 0))
        rows = pl.BlockSpec((1, SWEEP_CHUNKS, SUBLANES, CHUNK), lambda b, j: (b, idx(j), 0, 0))
        kt_spec = pl.BlockSpec((1, SWEEP_CHUNKS, D, CHUNK), lambda b, j: (b, idx(j), 0, 0))
        return seq, [seq, kt_spec, seq, cols, rows]

    seq_f, in_f = sweep_specs(lambda j: j)
    seq_b, in_b = sweep_specs(lambda j: sweep_steps - 1 - j)
    n_state = N_DIRS * N_HEADS
    h_fwd, h_bwd = pl.pallas_call(
        _sweep_kernel,
        grid=(B, sweep_steps),
        in_specs=[pl.BlockSpec(memory_space=pltpu.SMEM)] + in_f + in_b,
        out_specs=[seq_f, seq_b],
        out_shape=[seq_bf, seq_bf],
        scratch_shapes=[pltpu.VMEM((n_state, HEAD_DIM, HEAD_DIM), F32),
                        pltpu.VMEM((n_state, SUBLANES, HEAD_DIM), F32),
                        pltpu.SMEM((n_state,), F32)],
        compiler_params=arb2,
        name="sweep",
    )(chunk_scalars, q, kt, v, cols_f, rows_f, q, kt, v, cols_b, rows_b)

    px_prev, px_next = _halo_specs(OUT_TILE, n_halo, 0)
    pool_w_scaled = pool_w[0] * pool_scale[0].reshape(len(POOL_WINDOWS), 1, POOL_GROUP_DIM)

    out_seq = pl.BlockSpec((1, OUT_TILE, D), lambda b, i: (b, i, 0))
    out = pl.pallas_call(
        _combine_kernel,
        grid=(B, S // OUT_TILE),
        in_specs=[out_seq, out_seq, out_seq, out_seq,
                  px_prev, out_seq, px_next, out_seq, out_seq,
                  pl.BlockSpec((OUT_TILE, LANES), lambda b, i: (i, 0)),
                  const((len(POOL_WINDOWS), CHUNK, 2 * CHUNK)),
                  const((len(POOL_WINDOWS), POOL_GROUP_DIM, POOL_GROUP_DIM)),
                  const((1, D)), const((1, D)),
                  const((2 * D, D)), const((1, D))],
        out_specs=out_seq,
        out_shape=jax.ShapeDtypeStruct((B, S, D), F32),
        compiler_params=arb2,
        name="combine",
    )(h_fwd, h_bwd, c, silu_mz, pool_x, pool_x, pool_x, silu_pz, x,
      _pool_inverse_counts(S), _pool_band_matrices(), pool_w_scaled.astype(BF16),
      mh_norm_w[0][None, :], skip_w[0][None, :].astype(BF16), w_out[0].astype(BF16),
      norm_out_g[None, :])
    return out
```

```python
import numpy as np
import jax
import jax.numpy as jnp
from jax import lax
from jax.experimental import pallas as pl
from jax.experimental.pallas import tpu as pltpu

D_MODEL = 1024
D_POOL = 1024
D_MLSTM = 1024
POOL_WINDOWS = (2, 4, 8, 16)
POOL_GROUP_DIM = D_POOL // len(POOL_WINDOWS)
N_HEADS = 4
HEAD_DIM = 256
QKV_BLOCK = 4
CONV_WIDTH = 5
CHUNK = 128
N_DIRS = 2
EPS = 1e-6
LOG2E = 1.4426950408889634

LANES, SUBLANES = 128, 8
HALO = 16
DIR_LANES = LANES
GROUP_LANES = 16
I_LANE, CM_LANE, F_LANE = 0, 4, 8
SCAN_ROWS = 16
FRONT_TILE = 512
OUT_TILE = 1024
OUT_ROWS = 512
SWEEP_GROUP = 4
SWEEP_CHUNKS = 8
VMEM_LIMIT = 58 * 1024 * 1024

F32 = jnp.float32
BF16 = jnp.bfloat16


def _silu(z):
    return z * (1.0 / (1.0 + jnp.exp(-z)))


def _log_sigmoid(g):
    return jnp.minimum(g, 0.0) - jnp.log1p(jnp.exp(-jnp.abs(g)))


def _token_scan(x, op, reverse):
    t = lax.broadcasted_iota(jnp.int32, x.shape, 0)
    k = 1
    while k < CHUNK:
        if reverse:
            shifted = pltpu.roll(x, CHUNK - k, 0)
            valid = t < CHUNK - k
        else:
            shifted = pltpu.roll(x, k, 0)
            valid = t >= k
        x = jnp.where(valid, op(x, shifted), x)
        k *= 2
    return x


def _front_kernel(xp_ref, x_ref, xn_ref, gin_ref, win_ref, convw_ref, convb_ref,
                  wqkv_ref, wg_ref, bg_ref, invc_ref, pmat_ref, poolw_ref,
                  yp_ref, szm_ref, q_ref, kt_ref, v_ref, c_ref,
                  colsf_ref, colsb_ref, rowsf_ref, rowsb_ref,
                  win_sc):
    i = pl.program_id(1)
    n_tiles = pl.num_programs(1)
    tile = FRONT_TILE
    n_chunks = tile // CHUNK

    @pl.when((pl.program_id(0) == 0) & (i == 0))
    def _():
        for group in range(4):
            gs = slice(group * D_MODEL, (group + 1) * D_MODEL)
            win_sc[:, gs] = win_ref[:, gs].astype(BF16)

    x_ext = jnp.concatenate([xp_ref[0], x_ref[0], xn_ref[0]], axis=0)
    ms = jnp.mean(x_ext * x_ext, axis=-1, keepdims=True)
    u_ext = (x_ext * lax.rsqrt(ms + EPS) * gin_ref[...]).astype(BF16)
    u = u_ext[HALO:HALO + tile, :]

    def project(lhs, group):
        return jnp.dot(lhs, win_sc[:, group * D_MODEL:(group + 1) * D_MODEL],
                       preferred_element_type=F32)

    mx32 = project(u_ext, 2)
    mx_bf = mx32[HALO:HALO + tile, :].astype(BF16)
    szp = _silu(project(u, 1)).astype(BF16)
    zero_halo = jnp.zeros((HALO, D_MLSTM), F32)
    rows = tile + 2 * HALO
    ext = jnp.concatenate(
        [jnp.where(i == 0, zero_halo, mx32[0:HALO, :]), mx32[HALO:HALO + tile, :],
         jnp.where(i == n_tiles - 1, zero_halo, mx32[HALO + tile:, :])], axis=0)

    pad = CONV_WIDTH // 2
    conv = convb_ref[...] + ext[HALO:HALO + tile, :] * convw_ref[pad:pad + 1, :]
    for tap in range(CONV_WIDTH):
        if tap != pad:
            shifted = pltpu.roll(ext, (rows - (tap - pad)) % rows, 0)
            conv = conv + shifted[HALO:HALO + tile, :] * convw_ref[tap:tap + 1, :]
    c_bf = _silu(conv).astype(BF16)
    c_ref[0] = c_bf

    gates = bg_ref[...] + jnp.dot(c_bf, wg_ref[0], preferred_element_type=F32) \
        + jnp.dot(mx_bf, wg_ref[1], preferred_element_type=F32)

    sub = lax.broadcasted_iota(jnp.int32, (CHUNK, DIR_LANES), 1) % GROUP_LANES
    for d, (cols_ref, rows_ref) in enumerate(((colsf_ref, rowsf_ref), (colsb_ref, rowsb_ref))):
        ds = slice(d * DIR_LANES, (d + 1) * DIR_LANES)
        packed = gates[0:CHUNK, ds]
        for ch in range(1, n_chunks):
            packed = packed + pltpu.roll(gates[ch * CHUNK:(ch + 1) * CHUNK, ds], ch * GROUP_LANES, 1)
        b = _token_scan(_log_sigmoid(packed), jnp.add, reverse=(d == 1))
        a = packed - pltpu.roll(b, DIR_LANES - (F_LANE - I_LANE), 1)
        cm = _token_scan(a, jnp.maximum, reverse=(d == 1))
        scan = LOG2E * jnp.where(sub < CM_LANE, a,
                                 jnp.where(sub < F_LANE, pltpu.roll(cm, CM_LANE - I_LANE, 1), b))
        for ch in range(n_chunks):
            cols = scan if ch == 0 else pltpu.roll(scan, DIR_LANES - ch * GROUP_LANES, 1)
            cols_ref[0, ch * CHUNK:(ch + 1) * CHUNK, :] = cols
            rows_ref[0, ch] = cols.T[0:SCAN_ROWS, :]
    szm_ref[0] = _silu(project(u, 3)).astype(BF16)

    for h in range(N_HEADS):
        hs = slice(h * HEAD_DIM, (h + 1) * HEAD_DIM)
        q_ref[0, :, hs] = jnp.dot(c_bf[:, hs], wqkv_ref[0, h],
                                  preferred_element_type=F32).astype(BF16)
        v_ref[0, :, hs] = jnp.dot(mx_bf[:, hs], wqkv_ref[2, h],
                                  preferred_element_type=F32).astype(BF16)
        kh = jnp.dot(c_bf[:, hs], wqkv_ref[1, h], preferred_element_type=F32)
        for ch in range(n_chunks):
            kt_ref[0, ch, hs, :] = kh[ch * CHUNK:(ch + 1) * CHUNK, :].T.astype(BF16)

    px_bf = project(u_ext, 0).astype(BF16)
    px_main = px_bf[HALO:HALO + tile, :]
    zero_px = jnp.zeros((HALO, D_POOL), BF16)
    pext = jnp.concatenate(
        [jnp.where(i == 0, zero_px, px_bf[0:HALO, :]), px_main,
         jnp.where(i == n_tiles - 1, zero_px, px_bf[HALO + tile:, :]),
         jnp.zeros((CHUNK - 2 * HALO, D_POOL), BF16)], axis=0)
    inv_count = invc_ref[...]
    for g in range(len(POOL_WINDOWS)):
        gs = slice(g * POOL_GROUP_DIM, (g + 1) * POOL_GROUP_DIM)
        total = jnp.concatenate(
            [jnp.dot(pmat_ref[g], pext[ch * CHUNK:(ch + 2) * CHUNK, gs], preferred_element_type=F32)
             for ch in range(n_chunks)], axis=0)
        pooled = total * inv_count[:, g:g + 1] - px_main[:, gs].astype(F32)
        mixed = jnp.dot(pooled.astype(BF16), poolw_ref[g], preferred_element_type=F32)
        yp_ref[0, :, gs] = mixed.astype(BF16) * szp[:, gs]


def _lane_bcast(tile, lane):
    return jnp.broadcast_to(tile[:, lane:lane + 1], tile.shape)


def _sweep_state_phase(h, off, carry, q_ref, kt_ref, v_ref, rows_ref, sc_ref, sc_base):
    hs = slice(h * HEAD_DIM, (h + 1) * HEAD_DIM)
    rs = pl.ds(pl.multiple_of(off * CHUNK, CHUNK), CHUNK)
    c_old, n_old, m = carry
    b_last = sc_ref[sc_base + h]
    cm_last = sc_ref[sc_base + N_HEADS + h]
    m_last = jnp.maximum(m, cm_last)

    a_row = rows_ref[0, off, I_LANE + h:I_LANE + h + 1, :]
    ws_row = jnp.exp2(a_row - m_last)
    decay = jnp.exp2(jnp.full((1, HEAD_DIM), m - m_last, F32))

    qh, vh = q_ref[0, rs, hs], v_ref[0, rs, hs]
    kt = kt_ref[0, off, hs, :]
    q_c = jnp.dot(qh, c_old.astype(BF16), preferred_element_type=F32)
    qk = jnp.dot(qh, kt, preferred_element_type=F32)

    ws_bf = ws_row.astype(BF16)
    kv = jnp.dot(kt * ws_bf, vh, preferred_element_type=F32)
    ws8 = jnp.broadcast_to(ws_bf, (SUBLANES, CHUNK))
    kn = lax.dot_general(ws8, kt, (((1,), (1,)), ((), ())), preferred_element_type=F32)
    new_carry = (decay * c_old + kv, decay * n_old + kn, b_last + m_last)
    return new_carry, (m, a_row, q_c, qk, n_old[0:1, :])


def _sweep_output_phase(h, off, state, q_ref, v_ref, cols_ref, causal, h_ref):
    hs = slice(h * HEAD_DIM, (h + 1) * HEAD_DIM)
    rs = pl.ds(pl.multiple_of(off * CHUNK, CHUNK), CHUNK)
    cols = cols_ref[0, rs, :]
    m, a_row, q_c, qk, n_row = state
    q_n = q_ref[0, rs, hs].astype(F32) * n_row
    q_n = q_n[:, :CHUNK] + q_n[:, CHUNK:]
    big_m = jnp.maximum(_lane_bcast(cols, CM_LANE + h), m)
    dmat = jnp.where(causal, jnp.exp2(a_row - big_m), 0.0)
    inter_w = jnp.exp2(m - big_m)
    exp_neg_mt = jnp.exp2(-(_lane_bcast(cols, F_LANE + h) + big_m))
    s = qk * dmat
    den = jnp.sum(s + inter_w * q_n, axis=-1, keepdims=True)
    inv = 1.0 / jnp.maximum(jnp.abs(den), exp_neg_mt[:, 0:1])
    inv_b = jnp.broadcast_to(inv, (CHUNK, CHUNK))
    num = jnp.dot(s.astype(BF16), v_ref[0, rs, hs], preferred_element_type=F32)
    out = (num + jnp.concatenate([inter_w, inter_w], axis=1) * q_c) \
        * jnp.concatenate([inv_b, inv_b], axis=1)
    h_ref[0, rs, hs] = out.astype(h_ref.dtype)


def _sweep_kernel(sc_ref,
                  qf_ref, ktf_ref, vf_ref, colsf_ref, rowsf_ref,
                  qb_ref, ktb_ref, vb_ref, colsb_ref, rowsb_ref,
                  hf_ref, hb_ref,
                  c_sc, n_sc, m_sc):
    b = pl.program_id(0)
    j = pl.program_id(1)
    nc = pl.num_programs(1) * SWEEP_CHUNKS

    @pl.when(j == 0)
    def _():
        c_sc[...] = jnp.zeros_like(c_sc)
        n_sc[...] = jnp.zeros_like(n_sc)
        for st in range(N_DIRS * N_HEADS):
            m_sc[st] = jnp.float32(0.0)

    per_chunk = N_DIRS * 2 * N_HEADS
    dirs = ((qf_ref, ktf_ref, vf_ref, colsf_ref, rowsf_ref, hf_ref),
            (qb_ref, ktb_ref, vb_ref, colsb_ref, rowsb_ref, hb_ref))
    units = [(d, h) for h in range(N_HEADS) for d in range(N_DIRS)]
    t_idx = lax.broadcasted_iota(jnp.int32, (CHUNK, CHUNK), 0)
    s_idx = lax.broadcasted_iota(jnp.int32, (CHUNK, CHUNK), 1)
    causal = (s_idx <= t_idx, s_idx >= t_idx)

    def one_chunk(sub, _):
        off = (sub, SWEEP_CHUNKS - 1 - sub)
        chunk_f = j * SWEEP_CHUNKS + sub
        base = ((b * nc + chunk_f) * per_chunk,
                (b * nc + (nc - 1 - chunk_f)) * per_chunk + 2 * N_HEADS)
        for g in range(0, len(units), SWEEP_GROUP):
            states = {}
            for d, h in units[g:g + SWEEP_GROUP]:
                st = d * N_HEADS + h
                q_ref, kt_ref, v_ref, _, rows_ref, _ = dirs[d]
                carry, states[d, h] = _sweep_state_phase(
                    h, off[d], (c_sc[st], n_sc[st], m_sc[st]), q_ref, kt_ref, v_ref, rows_ref,
                    sc_ref, base[d])
                c_sc[st], n_sc[st], m_sc[st] = carry
            for d, h in units[g:g + SWEEP_GROUP]:
                _sweep_output_phase(h, off[d], states[d, h], dirs[d][0], dirs[d][2], dirs[d][3],
                                    causal[d], dirs[d][5])

    lax.fori_loop(0, SWEEP_CHUNKS, one_chunk, None)


def _combine_kernel(hf_ref, hb_ref, c_ref, szm_ref, yp_ref, x_ref,
                    mhw_ref, skipw_ref, wout_ref, gout_ref,
                    o_ref):
    tile = OUT_TILE

    def branches(r0):
        rs = slice(r0, r0 + OUT_ROWS)
        y_p = yp_ref[0, rs, :]

        y_m_parts = []
        for h in range(N_HEADS):
            hs = slice(h * HEAD_DIM, (h + 1) * HEAD_DIM)
            ht = hf_ref[0, rs, hs].astype(F32) + hb_ref[0, rs, hs].astype(F32)
            mu = jnp.mean(ht, axis=-1, keepdims=True)
            dlt = ht - mu
            var = jnp.mean(dlt * dlt, axis=-1, keepdims=True)
            hn = (dlt * lax.rsqrt(var + EPS) * mhw_ref[:, hs]).astype(BF16)
            y_m_parts.append((hn + skipw_ref[:, hs] * c_ref[0, rs, hs]) * szm_ref[0, rs, hs])
        return y_p, jnp.concatenate(y_m_parts, axis=1)

    def project(r0, y_p, y_m):
        rs = slice(r0, r0 + OUT_ROWS)
        hres = x_ref[0, rs, :] + jnp.dot(y_p, wout_ref[0:D_POOL, :], preferred_element_type=F32) \
            + jnp.dot(y_m, wout_ref[D_POOL:, :], preferred_element_type=F32)
        ms = jnp.mean(hres * hres, axis=-1, keepdims=True)
        o_ref[0, rs, :] = hres * lax.rsqrt(ms + EPS) * gout_ref[...]

    starts = list(range(0, tile, OUT_ROWS))
    pending = branches(starts[0])
    for prev, nxt in zip(starts[:-1], starts[1:]):
        upcoming = branches(nxt)
        project(prev, *pending)
        pending = upcoming
    project(starts[-1], *pending)


def _block_diag_tiles(w):
    rows = w.reshape(3 * N_HEADS, HEAD_DIM, QKV_BLOCK)
    col = np.arange(HEAD_DIM)
    spread = jnp.asarray((col[None, :] % QKV_BLOCK == np.arange(QKV_BLOCK)[:, None]), w.dtype)
    tiled = jnp.einsum('tro,oc->trc', rows, spread, precision=lax.Precision.HIGHEST)
    same_block = jnp.asarray(col[:, None] // QKV_BLOCK == col[None, :] // QKV_BLOCK)
    return jnp.where(same_block[None], tiled, 0.0).reshape(3, N_HEADS, HEAD_DIM, HEAD_DIM)


def _gate_weights(w_gates, b_gates, tiles):
    n_gates = N_DIRS * 2 * N_HEADS
    place = np.zeros((n_gates, N_DIRS * DIR_LANES), np.float32)
    for d in range(N_DIRS):
        for g in range(2 * N_HEADS):
            lane = (I_LANE + g) if g < N_HEADS else (F_LANE + g - N_HEADS)
            place[d * 2 * N_HEADS + g, d * DIR_LANES + lane] = 1.0
    place = jnp.asarray(place)
    hi = lax.Precision.HIGHEST
    rows = jnp.transpose(w_gates, (1, 0, 2)).reshape(3, N_HEADS, HEAD_DIM, n_gates)
    per_map = jnp.einsum('ptrc,ptcg->ptrg', tiles, rows, precision=hi)
    folded = jnp.stack([per_map[0] + per_map[1], per_map[2]]).reshape(2, D_MLSTM, n_gates)
    return (jnp.einsum('krg,gl->krl', folded, place, precision=hi),
            jnp.dot(b_gates.reshape(1, n_gates), place, precision=hi))


def _pool_band_matrices():
    t = np.arange(CHUNK)[:, None]
    r = np.arange(2 * CHUNK)[None, :] - HALO
    mats = []
    for w in POOL_WINDOWS:
        left = (w - 1) // 2
        right = w - 1 - left
        mats.append(((r >= t - left) & (r <= t + right)).astype(np.float32))
    return jnp.asarray(np.stack(mats), dtype=BF16)


def _pool_inverse_counts(seq_len):
    t = np.arange(seq_len)
    inv = np.ones((seq_len, LANES), np.float32)
    for g, w in enumerate(POOL_WINDOWS):
        left = (w - 1) // 2
        right = w - 1 - left
        count = np.minimum(t + right, seq_len - 1) - np.maximum(t - left, 0) + 1
        inv[:, g] = 1.0 / count
    return jnp.asarray(inv)


def _halo_specs(tile, n_halo, col):
    per = tile // HALO
    prev = pl.BlockSpec((1, HALO, D_MODEL), lambda b, i: (b, jnp.maximum(i * per - 1, 0), col))
    nxt = pl.BlockSpec((1, HALO, D_MODEL),
                       lambda b, i: (b, jnp.minimum((i + 1) * per, n_halo - 1), col))
    return prev, nxt


def kernel(x, norm_in_g, w_in, pool_w, pool_scale, conv_w, conv_b, w_q, w_k, w_v, w_gates,
           b_gates, mh_norm_w, skip_w, w_out, norm_out_g):
    B, S, D = x.shape
    assert D == D_MODEL and S % FRONT_TILE == 0 and S % OUT_TILE == 0 and FRONT_TILE % CHUNK == 0
    assert (FRONT_TILE // CHUNK) * GROUP_LANES <= DIR_LANES and CONV_WIDTH // 2 <= HALO
    assert norm_in_g.shape[0] == 1, "single-layer block"
    nc = S // CHUNK
    n_halo = S // HALO
    arb2 = pltpu.CompilerParams(dimension_semantics=("arbitrary", "arbitrary"),
                                vmem_limit_bytes=VMEM_LIMIT)

    tiles = _block_diag_tiles(jnp.stack([w_q[0], w_k[0], w_v[0]]))
    wg, bg = _gate_weights(w_gates[0], b_gates[0], tiles)
    qkv_scale = jnp.asarray([1.0, float(HEAD_DIM) ** -0.5, 1.0], F32).reshape(3, 1, 1, 1)
    wqkv = (tiles * qkv_scale).astype(BF16)
    conv_w8 = jnp.pad(conv_w[0], ((0, SUBLANES - CONV_WIDTH), (0, 0)))

    def const(shape):
        return pl.BlockSpec(shape, lambda b, i: (0,) * len(shape), pipeline_mode=pl.Buffered(1))

    x_prev, x_next = _halo_specs(FRONT_TILE, n_halo, 0)
    front_seq = pl.BlockSpec((1, FRONT_TILE, D), lambda b, i: (b, i, 0))
    front_cols = pl.BlockSpec((1, FRONT_TILE, DIR_LANES), lambda b, i: (b, i, 0))
    cpt = FRONT_TILE // CHUNK
    front_rows = pl.BlockSpec((1, cpt, SCAN_ROWS, CHUNK), lambda b, i: (b, i, 0, 0))
    front_kt = pl.BlockSpec((1, cpt, D, CHUNK), lambda b, i: (b, i, 0, 0))
    seq_bf = jax.ShapeDtypeStruct((B, S, D), BF16)
    cols_shape = jax.ShapeDtypeStruct((B, S, DIR_LANES), F32)
    rows_shape = jax.ShapeDtypeStruct((B, nc, SCAN_ROWS, CHUNK), F32)
    kt_shape = jax.ShapeDtypeStruct((B, nc, D, CHUNK), BF16)
    pool_w_scaled = pool_w[0] * pool_scale[0].reshape(len(POOL_WINDOWS), 1, POOL_GROUP_DIM)
    y_pool, silu_mz, q, kt, v, c, cols_f, cols_b, rows_f, rows_b = pl.pallas_call(
        _front_kernel,
        grid=(B, S // FRONT_TILE),
        in_specs=[x_prev, front_seq, x_next, const((1, D)), const((D, 4 * D)),
                  const((SUBLANES, D)), const((1, D)),
                  const((3, N_HEADS, HEAD_DIM, HEAD_DIM)),
                  const((2, D, N_DIRS * DIR_LANES)), const((1, N_DIRS * DIR_LANES)),
                  pl.BlockSpec((FRONT_TILE, LANES), lambda b, i: (i, 0)),
                  const((len(POOL_WINDOWS), CHUNK, 2 * CHUNK)),
                  const((len(POOL_WINDOWS), POOL_GROUP_DIM, POOL_GROUP_DIM))],
        out_specs=[front_seq, front_seq, front_seq, front_kt, front_seq, front_seq,
                   front_cols, front_cols, front_rows, front_rows],
        out_shape=[seq_bf, seq_bf, seq_bf, kt_shape, seq_bf, seq_bf,
                   cols_shape, cols_shape, rows_shape, rows_shape],
        scratch_shapes=[pltpu.VMEM((D, 4 * D), BF16)],
        compiler_params=arb2,
        name="front",
    )(x, x, x, norm_in_g[0][None, :], w_in[0], conv_w8,
      conv_b[0][None, :], wqkv, wg.astype(BF16), bg,
      _pool_inverse_counts(S), _pool_band_matrices(), pool_w_scaled.astype(BF16))

    def last(rows, lane):
        return jnp.concatenate([rows[:, :, F_LANE:F_LANE + N_HEADS, lane],
                                rows[:, :, CM_LANE:CM_LANE + N_HEADS, lane]], axis=-1)

    chunk_scalars = jnp.concatenate([last(rows_f, CHUNK - 1), last(rows_b, 0)], axis=-1).reshape(-1)

    sweep_steps = nc // SWEEP_CHUNKS
    sweep_rows = SWEEP_CHUNKS * CHUNK

    def sweep_specs(idx):
        seq = pl.BlockSpec((1, sweep_rows, D), lambda b, j: (b, idx(j), 0))
        cols = pl.BlockSpec((1, sweep_rows, DIR_LANES), lambda b, j: (b, idx(j), 0))
        rows = pl.BlockSpec((1, SWEEP_CHUNKS, SUBLANES, CHUNK), lambda b, j: (b, idx(j), 0, 0))
        kt_spec = pl.BlockSpec((1, SWEEP_CHUNKS, D, CHUNK), lambda b, j: (b, idx(j), 0, 0))
        return seq, [seq, kt_spec, seq, cols, rows]

    seq_f, in_f = sweep_specs(lambda j: j)
    seq_b, in_b = sweep_specs(lambda j: sweep_steps - 1 - j)
    n_state = N_DIRS * N_HEADS
    h_fwd, h_bwd = pl.pallas_call(
        _sweep_kernel,
        grid=(B, sweep_steps),
        in_specs=[pl.BlockSpec(memory_space=pltpu.SMEM)] + in_f + in_b,
        out_specs=[seq_f, seq_b],
        out_shape=[seq_bf, seq_bf],
        scratch_shapes=[pltpu.VMEM((n_state, HEAD_DIM, HEAD_DIM), F32),
                        pltpu.VMEM((n_state, SUBLANES, HEAD_DIM), F32),
                        pltpu.SMEM((n_state,), F32)],
        compiler_params=arb2,
        name="sweep",
    )(chunk_scalars, q, kt, v, cols_f, rows_f, q, kt, v, cols_b, rows_b)

    out_seq = pl.BlockSpec((1, OUT_TILE, D), lambda b, i: (b, i, 0))
    out = pl.pallas_call(
        _combine_kernel,
        grid=(B, S // OUT_TILE),
        in_specs=[out_seq, out_seq, out_seq, out_seq, out_seq, out_seq,
                  const((1, D)), const((1, D)),
                  const((2 * D, D)), const((1, D))],
        out_specs=out_seq,
        out_shape=jax.ShapeDtypeStruct((B, S, D), F32),
        compiler_params=arb2,
        name="combine",
    )(h_fwd, h_bwd, c, silu_mz, y_pool, x,
      mh_norm_w[0][None, :], skip_w[0][None, :].astype(BF16), w_out[0].astype(BF16),
      norm_out_g[None, :])
    return out
```
